```python
import jax, jax.numpy as jnp
from jax import lax
import numpy as np

D_MODEL = 1024
BATCH = 2
SEQ = 8192
DEPTH = 1

GRID_W = 64
D_MIX = D_MODEL
ATTN_WIDTH = D_MIX // 2
REC_WIDTH = D_MIX - ATTN_WIDTH
HEAD_DIM = 128
N_Q_HEADS = ATTN_WIDTH // HEAD_DIM
N_KV_HEADS = 2
KV_GROUPS = N_Q_HEADS // N_KV_HEADS
ROPE_THETA = 10000.0
ROPE_AXIS_DIM = HEAD_DIM // 2
Q_BLOCK = 128
REC_HEAD_DIM = 128
N_REC_HEADS = REC_WIDTH // REC_HEAD_DIM
REC_CHUNK = 64
D_FF = 2816
EPS = 1e-6
IN_SPLITS = (N_Q_HEADS * HEAD_DIM,
             N_KV_HEADS * HEAD_DIM,
             N_KV_HEADS * HEAD_DIM,
             REC_WIDTH,
             REC_WIDTH,
             REC_WIDTH,
             REC_WIDTH,
             REC_WIDTH)
D_IN_MIX = 512 + 256 + 256 + 5 * 512

kernel_name = "hybrid_gqa_hgrn2_macaron_encoder"


def rms_norm(x, gain):
    xf = x.astype(jnp.float32)
    y = xf * lax.rsqrt(jnp.mean(xf * xf, axis=-1, keepdims=True) + EPS)
    return (y * gain.astype(jnp.float32)).astype(x.dtype)


def swiglu_half_step(x, gain, w_in, w_out):
    h = rms_norm(x, gain)
    gate, up = jnp.split(h @ w_in, 2, axis=-1)
    return (jax.nn.silu(gate) * up) @ w_out


def axial_rope_tables(seq_len):
    rows = seq_len // GRID_W
    row_ids = jnp.repeat(jnp.arange(rows, dtype=jnp.float32), GRID_W)
    col_ids = jnp.tile(jnp.arange(GRID_W, dtype=jnp.float32), rows)
    inv_freq = ROPE_THETA ** (-jnp.arange(0, ROPE_AXIS_DIM, 2, dtype=jnp.float32) / ROPE_AXIS_DIM)
    ang_r = row_ids[:, None] * inv_freq[None, :]
    ang_c = col_ids[:, None] * inv_freq[None, :]
    return jnp.cos(ang_r), jnp.sin(ang_r), jnp.cos(ang_c), jnp.sin(ang_c)


def _rotate(xp, cos, sin):
    x1, x2 = jnp.split(xp, 2, axis=-1)
    return jnp.concatenate([x1 * cos - x2 * sin, x2 * cos + x1 * sin], axis=-1)


def apply_axial_rope(x, tables):
    cos_r, sin_r, cos_c, sin_c = (t[:, None, :] for t in tables)
    xf = x.astype(jnp.float32)
    out = jnp.concatenate([_rotate(xf[..., :ROPE_AXIS_DIM], cos_r, sin_r),
                           _rotate(xf[..., ROPE_AXIS_DIM:], cos_c, sin_c)], axis=-1)
    return out.astype(x.dtype)


def gqa_bidirectional(q, k, v, q_gain, k_gain):
    bsz, seq_len = q.shape[0], q.shape[1]
    tables = axial_rope_tables(seq_len)
    q = apply_axial_rope(rms_norm(q, q_gain), tables) * (HEAD_DIM ** -0.5)
    k = apply_axial_rope(rms_norm(k, k_gain), tables)
    n_blk = seq_len // Q_BLOCK
    qb = q.reshape(bsz, n_blk, Q_BLOCK, N_KV_HEADS, KV_GROUPS, HEAD_DIM)
    qb = jnp.moveaxis(qb, 1, 0)

    def one_block(q_blk):
        s = jnp.einsum('bqkgd,bskd->bkgqs', q_blk, k).astype(jnp.float32)
        p = jax.nn.softmax(s, axis=-1).astype(v.dtype)
        return jnp.einsum('bkgqs,bskd->bqkgd', p, v)

    o = lax.map(one_block, qb)
    return jnp.moveaxis(o, 0, 1).reshape(bsz, seq_len, ATTN_WIDTH)


def hgrn2_chunk_scan(q, k, v, log_f):
    n, seq_len, h, dk = q.shape
    dv = v.shape[-1]
    nc = seq_len // REC_CHUNK

    def to_chunks(a):
        a = a.astype(jnp.float32).reshape(n, nc, REC_CHUNK, h, a.shape[-1])
        return jnp.transpose(a, (1, 0, 3, 2, 4))

    causal = jnp.tril(jnp.ones((REC_CHUNK, REC_CHUNK), dtype=bool))[:, :, None]

    def step(state, inp):
        qc, kc, vc, gc = inp
        b = jnp.cumsum(gc, axis=-2)
        inter = jnp.einsum('nhcd,nhde->nhce', qc * jnp.exp(b), state)
        diff = b[..., :, None, :] - b[..., None, :, :]
        decay = jnp.exp(jnp.where(causal, diff, -jnp.inf))
        scores = jnp.einsum('nhtd,nhtsd,nhsd->nhts', qc, decay, kc)
        intra = jnp.einsum('nhts,nhse->nhte', scores, vc)
        b_last = b[..., -1:, :]
        new_state = jnp.exp(b_last[..., 0, :])[..., None] * state + \
            jnp.einsum('nhsd,nhse->nhde', kc * jnp.exp(b_last - b), vc)
        return new_state, inter + intra

    s0 = jnp.zeros((n, h, dk, dv), jnp.float32)
    _, o = lax.scan(step, s0, (to_chunks(q), to_chunks(k), to_chunks(v), to_chunks(log_f)))
    o = jnp.transpose(o, (1, 0, 3, 2, 4)).reshape(n, seq_len, h, dv)
    return o.astype(v.dtype)


def hgrn2_bidirectional(q, zf_fwd, zf_bwd, i, g, lb, out_gain):
    bsz, seq_len = q.shape[0], q.shape[1]
    shp = (bsz, seq_len, N_REC_HEADS, REC_HEAD_DIM)

    def gates(z, lower):
        f = lower + (1.0 - lower) * jax.nn.sigmoid(z.astype(jnp.float32))
        return jnp.log(f).reshape(shp), (1.0 - f).reshape(shp)

    logf_fwd, k_fwd = gates(zf_fwd, lb[0])
    logf_bwd, k_bwd = gates(zf_bwd, lb[1])
    q4, i4 = q.reshape(shp), i.reshape(shp)
    flip = lambda a: jnp.flip(a, axis=1)
    q_dir = jnp.concatenate([q4, flip(q4)], axis=0)
    k_dir = jnp.concatenate([k_fwd, flip(k_bwd)], axis=0)
    v_dir = jnp.concatenate([i4, flip(i4)], axis=0)
    g_dir = jnp.concatenate([logf_fwd, flip(logf_bwd)], axis=0)
    o_dir = hgrn2_chunk_scan(q_dir, k_dir, v_dir, g_dir)
    o = o_dir[:bsz] + flip(o_dir[bsz:])
    o = rms_norm(o, out_gain).reshape(bsz, seq_len, REC_WIDTH)
    return o * jax.nn.silu(g)


def hybrid_mixer(x, layer, norm_gain, w_in, q_gain, k_gain, attn_out_gain, rec_lb_logits,
                 rec_out_gain, w_out):
    bsz, seq_len = x.shape[0], x.shape[1]
    h = rms_norm(x, norm_gain)
    proj = h @ w_in
    offsets = [int(o) for o in np.cumsum(IN_SPLITS)[:-1]]
    aq, ak, av, rq, rf_fwd, rf_bwd, ri, rg = jnp.split(proj, offsets, axis=-1)
    attn = gqa_bidirectional(aq.reshape(bsz, seq_len, N_Q_HEADS, HEAD_DIM),
                             ak.reshape(bsz, seq_len, N_KV_HEADS, HEAD_DIM),
                             av.reshape(bsz, seq_len, N_KV_HEADS, HEAD_DIM), q_gain, k_gain)
    attn = rms_norm(attn, attn_out_gain)
    lb_all = jnp.cumsum(jax.nn.softmax(rec_lb_logits.astype(jnp.float32), axis=1), axis=1)
    lb = lb_all[:, layer]
    rec = hgrn2_bidirectional(rq, rf_fwd, rf_bwd, ri, rg, lb, rec_out_gain)
    return jnp.concatenate([attn, rec.astype(attn.dtype)], axis=-1) @ w_out


def setup_inputs(seed: int = 0) -> dict:
    key = jax.random.key(seed)
    ks = jax.random.split(key, 16)
    nrm = lambda k, shape, s: jax.random.normal(k, shape, jnp.float32) * s
    gain = lambda k, shape: 1.0 + 0.02 * jax.random.normal(k, shape, jnp.float32)
    return {
        "x": nrm(ks[0], (BATCH, SEQ, D_MODEL), 1.0),
        "ffn1_norm": gain(ks[1], (DEPTH, D_MODEL)),
        "ffn1_w_in": nrm(ks[2], (DEPTH, D_MODEL, 2 * D_FF), D_MODEL ** -0.5),
        "ffn1_w_out": nrm(ks[3], (DEPTH, D_FF, D_MODEL), D_FF ** -0.5),
        "mix_norm": gain(ks[4], (DEPTH, D_MODEL)),
        "w_in_mix": nrm(ks[5], (DEPTH, D_MODEL, D_IN_MIX), D_MODEL ** -0.5),
        "attn_q_norm": gain(ks[6], (DEPTH, HEAD_DIM)),
        "attn_k_norm": gain(ks[7], (DEPTH, HEAD_DIM)),
        "attn_out_norm": gain(ks[8], (DEPTH, ATTN_WIDTH)),
        "rec_lb_logits": nrm(ks[9], (2, DEPTH + 1, REC_WIDTH), 0.5),
        "rec_out_norm": gain(ks[10], (DEPTH, REC_HEAD_DIM)),
        "w_out_mix": nrm(ks[11], (DEPTH, D_MIX, D_MODEL), D_MIX ** -0.5),
        "ffn2_norm": gain(ks[12], (DEPTH, D_MODEL)),
        "ffn2_w_in": nrm(ks[13], (DEPTH, D_MODEL, 2 * D_FF), D_MODEL ** -0.5),
        "ffn2_w_out": nrm(ks[14], (DEPTH, D_FF, D_MODEL), D_FF ** -0.5),
        "final_norm": gain(ks[15], (DEPTH, D_MODEL)),
    }


def reference(x, ffn1_norm, ffn1_w_in, ffn1_w_out, mix_norm, w_in_mix, attn_q_norm, attn_k_norm,
              attn_out_norm, rec_lb_logits, rec_out_norm, w_out_mix, ffn2_norm, ffn2_w_in,
              ffn2_w_out, final_norm):
    for l in range(DEPTH):
        x = x + 0.5 * swiglu_half_step(x, ffn1_norm[l], ffn1_w_in[l], ffn1_w_out[l])
        x = x + hybrid_mixer(x, l, mix_norm[l], w_in_mix[l], attn_q_norm[l], attn_k_norm[l],
                             attn_out_norm[l], rec_lb_logits, rec_out_norm[l], w_out_mix[l])
        x = x + 0.5 * swiglu_half_step(x, ffn2_norm[l], ffn2_w_in[l], ffn2_w_out[l])
        x = rms_norm(x, final_norm[l])
    return x
```

```python
import functools

import jax
import jax.numpy as jnp
import numpy as np
from jax import lax
from jax.experimental import pallas as pl
from jax.experimental.pallas import tpu as pltpu

F32 = jnp.float32
BF16 = jnp.bfloat16

EPS = 1e-6
GRID_W = 64
HEAD_DIM = 128
N_Q_HEADS = 4
N_KV_HEADS = 2
KV_GROUPS = N_Q_HEADS // N_KV_HEADS
ATTN_WIDTH = N_Q_HEADS * HEAD_DIM
KV_WIDTH = N_KV_HEADS * HEAD_DIM
ROPE_THETA = 10000.0
ROPE_AXIS_DIM = HEAD_DIM // 2
REC_HEAD_DIM = 128
N_REC_HEADS = 4
REC_WIDTH = N_REC_HEADS * REC_HEAD_DIM

V7X_LANES = 128

FFN_TOKENS = 1024
FFN_FF_CHUNK = 256
MIX_IN_TOKENS = 512
MIX_OUT_TOKENS = 1024
ATTN_Q_TOKENS = 512
ATTN_KV_TOKENS = 1024
REC_CHUNK = 64
REC_LEVELS = 6

VMEM_LIMIT = 56 * 1024 * 1024


def _rms(x, gain):
    return x * lax.rsqrt(jnp.mean(x * x, axis=-1, keepdims=True) + EPS) * gain


def _dot(a, b):
    return jnp.dot(a, b, preferred_element_type=F32)


def _dot_nt(a, b):
    return lax.dot_general(a, b, (((1,), (1,)), ((), ())), preferred_element_type=F32)


def _dot_tn(a, b):
    return lax.dot_general(a, b, (((0,), (0,)), ((), ())), preferred_element_type=F32)


def _ffn_kernel(*refs, final_norm):
    if final_norm:
        x_ref, gain_ref, wg_ref, wu_ref, wo_ref, fgain_ref, out_ref, h_scr, acc_scr = refs
    else:
        x_ref, gain_ref, wg_ref, wu_ref, wo_ref, out_ref, h_scr, acc_scr = refs
    j = pl.program_id(1)

    @pl.when(j == 0)
    def _():
        h_scr[...] = _rms(x_ref[...], gain_ref[...]).astype(BF16)
        acc_scr[...] = jnp.zeros_like(acc_scr)

    h = h_scr[...]
    gate = _dot(h, wg_ref[...])
    up = _dot(h, wu_ref[...])
    act = (gate * jax.nn.sigmoid(gate)) * up
    acc_scr[...] += _dot(act.astype(BF16), wo_ref[...])

    @pl.when(j == pl.num_programs(1) - 1)
    def _():
        y = x_ref[...] + 0.5 * acc_scr[...]
        if final_norm:
            y = _rms(y, fgain_ref[...])
        out_ref[...] = y


def _ffn(x2d, gain, w_in, w_out, final_gain=None):
    n_tok, d = x2d.shape
    d_ff = w_out.shape[0]
    n_ff = d_ff // FFN_FF_CHUNK
    final_norm = final_gain is not None
    in_specs = [
        pl.BlockSpec((FFN_TOKENS, d), lambda i, j: (i, 0)),
        pl.BlockSpec((1, d), lambda i, j: (0, 0)),
        pl.BlockSpec((d, FFN_FF_CHUNK), lambda i, j: (0, j)),
        pl.BlockSpec((d, FFN_FF_CHUNK), lambda i, j: (0, j + n_ff)),
        pl.BlockSpec((FFN_FF_CHUNK, d), lambda i, j: (j, 0)),
    ]
    args = [x2d, gain.reshape(1, d), w_in, w_in, w_out]
    if final_norm:
        in_specs.append(pl.BlockSpec((1, d), lambda i, j: (0, 0)))
        args.append(final_gain.reshape(1, d))
    return pl.pallas_call(
        functools.partial(_ffn_kernel, final_norm=final_norm),
        grid=(n_tok // FFN_TOKENS, n_ff),
        in_specs=in_specs,
        out_specs=pl.BlockSpec((FFN_TOKENS, d), lambda i, j: (i, 0)),
        out_shape=jax.ShapeDtypeStruct((n_tok, d), F32),
        scratch_shapes=[pltpu.VMEM((FFN_TOKENS, d), BF16), pltpu.VMEM((FFN_TOKENS, d), F32)],
        compiler_params=pltpu.CompilerParams(
            dimension_semantics=("parallel", "arbitrary"), vmem_limit_bytes=VMEM_LIMIT),
        name="ffn_final" if final_norm else "ffn",
    )(*args)


def _rope(x, cos, sin_signed, first_half):
    swapped = jnp.where(first_half, pltpu.roll(x, 96, 1), pltpu.roll(x, 32, 1))
    return x * cos + swapped * sin_signed


def _mix_in_kernel(x_ref, gain_ref, w_ref, qg_ref, kg_ref, cos_ref, sin_ref,
                   q_ref, k_ref, v_ref, rq_ref, zf_ref, zb_ref, ri_ref, rg_ref):
    h = _rms(x_ref[...], gain_ref[...]).astype(BF16)
    cos = cos_ref[...]
    sin = sin_ref[...]
    lane = lax.broadcasted_iota(jnp.int32, cos.shape, 1)
    first_half = (lane % (ROPE_AXIS_DIM)) < (ROPE_AXIS_DIM // 2)

    def proj(c0, width):
        return _dot(h, w_ref[:, c0:c0 + width])

    aq = proj(0, ATTN_WIDTH)
    for hh in range(N_Q_HEADS):
        sl = slice(hh * HEAD_DIM, (hh + 1) * HEAD_DIM)
        qh = _rope(_rms(aq[:, sl], qg_ref[...]), cos, sin, first_half) * (HEAD_DIM ** -0.5)
        q_ref[:, sl] = qh.astype(BF16)
    c = ATTN_WIDTH
    ak = proj(c, KV_WIDTH)
    for hh in range(N_KV_HEADS):
        sl = slice(hh * HEAD_DIM, (hh + 1) * HEAD_DIM)
        k_ref[:, sl] = _rope(_rms(ak[:, sl], kg_ref[...]), cos, sin, first_half).astype(BF16)
    c += KV_WIDTH
    v_ref[...] = proj(c, KV_WIDTH).astype(BF16)
    c += KV_WIDTH
    for out_ref in (rq_ref, zf_ref, zb_ref, ri_ref, rg_ref):
        out_ref[...] = proj(c, REC_WIDTH)
        c += REC_WIDTH


def _rope_tables(seq_len):
    rows = seq_len // GRID_W
    row_ids = jnp.repeat(jnp.arange(rows, dtype=F32), GRID_W)
    col_ids = jnp.tile(jnp.arange(GRID_W, dtype=F32), rows)
    inv_freq = ROPE_THETA ** (-jnp.arange(0, ROPE_AXIS_DIM, 2, dtype=F32) / ROPE_AXIS_DIM)
    ang_r = row_ids[:, None] * inv_freq[None, :]
    ang_c = col_ids[:, None] * inv_freq[None, :]
    cos = jnp.concatenate([jnp.cos(ang_r), jnp.cos(ang_r), jnp.cos(ang_c), jnp.cos(ang_c)], axis=-1)
    sin = jnp.concatenate([-jnp.sin(ang_r), jnp.sin(ang_r), -jnp.sin(ang_c), jnp.sin(ang_c)], axis=-1)
    return cos, sin


def _mix_in(x2d, gain, w_in, q_gain, k_gain, seq_len):
    n_tok, d = x2d.shape
    d_in = w_in.shape[1]
    tm = MIX_IN_TOKENS
    cos, sin = _rope_tables(seq_len)
    tiles_per_seq = seq_len // tm
    tok = lambda i: (i, 0)
    const = lambda i: (0, 0)
    table = lambda i: (i % tiles_per_seq, 0)
    out_shape = [
        jax.ShapeDtypeStruct((n_tok, ATTN_WIDTH), BF16),
        jax.ShapeDtypeStruct((n_tok, KV_WIDTH), BF16),
        jax.ShapeDtypeStruct((n_tok, KV_WIDTH), BF16),
    ] + [jax.ShapeDtypeStruct((n_tok, REC_WIDTH), F32)] * 5
    out_specs = [
        pl.BlockSpec((tm, ATTN_WIDTH), tok),
        pl.BlockSpec((tm, KV_WIDTH), tok),
        pl.BlockSpec((tm, KV_WIDTH), tok),
    ] + [pl.BlockSpec((tm, REC_WIDTH), tok)] * 5
    return pl.pallas_call(
        _mix_in_kernel,
        grid=(n_tok // tm,),
        in_specs=[
            pl.BlockSpec((tm, d), tok),
            pl.BlockSpec((1, d), const),
            pl.BlockSpec((d, d_in), const),
            pl.BlockSpec((1, HEAD_DIM), const),
            pl.BlockSpec((1, HEAD_DIM), const),
            pl.BlockSpec((tm, HEAD_DIM), table),
            pl.BlockSpec((tm, HEAD_DIM), table),
        ],
        out_specs=out_specs,
        out_shape=out_shape,
        compiler_params=pltpu.CompilerParams(
            dimension_semantics=("parallel",), vmem_limit_bytes=VMEM_LIMIT),
        name="mix_in",
    )(x2d, gain.reshape(1, d), w_in, q_gain.reshape(1, HEAD_DIM), k_gain.reshape(1, HEAD_DIM), cos, sin)


def _attn_kernel(q_ref, k_ref, v_ref, o_ref, m_scr, l_scr, acc_scr):
    ki = pl.program_id(3)
    tq = q_ref.shape[1]

    @pl.when(ki == 0)
    def _():
        m_scr[...] = jnp.full_like(m_scr, -jnp.inf)
        l_scr[...] = jnp.zeros_like(l_scr)
        acc_scr[...] = jnp.zeros_like(acc_scr)

    q = q_ref[0]
    q2 = jnp.concatenate([q[:, g * HEAD_DIM:(g + 1) * HEAD_DIM] for g in range(KV_GROUPS)], axis=0)
    s = _dot_nt(q2, k_ref[0])
    m_prev = m_scr[...]
    m_new = jnp.maximum(m_prev, jnp.max(s, axis=-1, keepdims=True))
    alpha = jnp.exp(m_prev - m_new)
    p = jnp.exp(s - m_new)
    l_scr[...] = alpha * l_scr[...] + jnp.sum(p, axis=-1, keepdims=True)
    acc_scr[...] = alpha * acc_scr[...] + _dot(p.astype(BF16), v_ref[0])
    m_scr[...] = m_new

    @pl.when(ki == pl.num_programs(3) - 1)
    def _():
        o = acc_scr[...] / l_scr[...]
        for g in range(KV_GROUPS):
            o_ref[0, :, g * HEAD_DIM:(g + 1) * HEAD_DIM] = o[g * tq:(g + 1) * tq]


def _attention(q, k, v):
    bsz, seq_len, _ = q.shape
    tq, tk = ATTN_Q_TOKENS, ATTN_KV_TOKENS
    gw = KV_GROUPS * HEAD_DIM
    return pl.pallas_call(
        _attn_kernel,
        grid=(bsz, N_KV_HEADS, seq_len // tq, seq_len // tk),
        in_specs=[
            pl.BlockSpec((1, tq, gw), lambda b, h, qi, ki: (b, qi, h)),
            pl.BlockSpec((1, tk, HEAD_DIM), lambda b, h, qi, ki: (b, ki, h)),
            pl.BlockSpec((1, tk, HEAD_DIM), lambda b, h, qi, ki: (b, ki, h)),
        ],
        out_specs=pl.BlockSpec((1, tq, gw), lambda b, h, qi, ki: (b, qi, h)),
        out_shape=jax.ShapeDtypeStruct((bsz, seq_len, ATTN_WIDTH), F32),
        scratch_shapes=[
            pltpu.VMEM((KV_GROUPS * tq, 1), F32),
            pltpu.VMEM((KV_GROUPS * tq, 1), F32),
            pltpu.VMEM((KV_GROUPS * tq, HEAD_DIM), F32),
        ],
        compiler_params=pltpu.CompilerParams(
            dimension_semantics=("parallel", "parallel", "parallel", "arbitrary"),
            vmem_limit_bytes=VMEM_LIMIT),
        name="attn",
    )(q, k, v)


def _rec_constants():
    c = REC_CHUNK
    t = np.arange(c)[:, None]
    u = np.arange(c)[None, :]
    blocks = [(u <= t)]
    masks = []
    for lvl in range(REC_LEVELS):
        h = 1 << lvl
        base_t = t - t % (2 * h)
        mid = base_t + h
        upper = t >= mid
        blocks.append(np.where(upper, (u >= mid) & (u <= t), (u > t) & (u < mid)))
        masks.append((t // (2 * h) == u // (2 * h)) & (t % (2 * h) >= h) & (u % (2 * h) < h))
    blocks.append(u > t)
    masks.append(t == u)
    fwd = np.concatenate([b.astype(np.float32) for b in blocks], axis=0)
    fwd_masks = np.stack([m.astype(np.float32) for m in masks])
    n_rows = fwd.shape[0]
    bwd = fwd.reshape(-1, c, c)[:, ::-1, ::-1].reshape(n_rows, c)
    bwd_masks = fwd_masks[:, ::-1, ::-1]
    total = np.ones((8, c), np.float32)
    mats = np.stack([np.concatenate([fwd, total]), np.concatenate([bwd, total])])
    mats = np.concatenate([mats, mats, mats], axis=-1)
    return mats, np.stack([fwd_masks, bwd_masks])


def _split3(x):
    hi = x.astype(BF16)
    r1 = x - hi.astype(F32)
    mid = r1.astype(BF16)
    lo = (r1 - mid.astype(F32)).astype(BF16)
    return jnp.concatenate([hi, mid, lo], axis=0)


def _rec_kernel(qf_ref, qb_ref, vf_ref, vb_ref, zf_ref, zb_ref, lbl_ref, mat_ref, mask_ref,
                of_ref, ob_ref, state_scr, *, layer):
    c = REC_CHUNK

    @pl.when(pl.program_id(1) == 0)
    def _():
        state_scr[...] = jnp.zeros_like(state_scr)

    dirs = ((qf_ref, vf_ref, zf_ref, of_ref), (qb_ref, vb_ref, zb_ref, ob_ref))
    for d, (q_ref, v_ref, z_ref, o_ref) in enumerate(dirs):
        logits = lbl_ref[d]
        e = jnp.exp(logits - jnp.max(logits, axis=0, keepdims=True))
        lb = jnp.sum(e[:layer + 1], axis=0, keepdims=True) / jnp.sum(e, axis=0, keepdims=True)
        f = lb + (1.0 - lb) * jax.nn.sigmoid(z_ref[0])
        g = jnp.log(f)
        kk = 1.0 - f
        sums = _dot(mat_ref[d], _split3(g))
        q = q_ref[0]
        v = v_ref[0].astype(BF16)
        for hh in range(N_REC_HEADS):
            sl = slice(hh * REC_HEAD_DIM, (hh + 1) * REC_HEAD_DIM)
            qh = q[:, sl]
            kh = kk[:, sl]
            vh = v[:, sl]
            scores = mask_ref[d, REC_LEVELS] * _dot_nt(qh.astype(BF16), kh.astype(BF16))
            for lvl in range(REC_LEVELS):
                el = jnp.exp(sums[(lvl + 1) * c:(lvl + 2) * c, sl])
                scores += mask_ref[d, lvl] * _dot_nt((qh * el).astype(BF16), (kh * el).astype(BF16))
            cum = sums[0:c, sl]
            rem = sums[(REC_LEVELS + 1) * c:(REC_LEVELS + 2) * c, sl]
            tot = sums[(REC_LEVELS + 2) * c:(REC_LEVELS + 2) * c + 1, sl]
            state_t = state_scr[d * N_REC_HEADS + hh]
            inter = _dot_nt((qh * jnp.exp(cum)).astype(BF16), state_t.astype(BF16))
            o_ref[0, :, sl] = inter + _dot(scores.astype(BF16), vh)
            k_tail = (kh * jnp.exp(rem)).astype(BF16)
            state_scr[d * N_REC_HEADS + hh] = state_t * jnp.exp(tot) + _dot_tn(vh, k_tail)


def _hgrn2(rq, zf, zb, ri, lb_logits, layer):
    bsz, seq_len, width = rq.shape
    c = REC_CHUNK
    nc = seq_len // c
    mats, masks = _rec_constants()
    mats = jnp.asarray(mats, BF16)
    masks = jnp.asarray(masks, F32)
    fwd = lambda b, i: (b, i, 0)
    bwd = lambda b, i: (b, nc - 1 - i, 0)
    blk = (1, c, width)
    full3 = lambda b, i: (0, 0, 0)
    full4 = lambda b, i: (0, 0, 0, 0)
    return pl.pallas_call(
        functools.partial(_rec_kernel, layer=layer),
        grid=(bsz, nc),
        in_specs=[
            pl.BlockSpec(blk, fwd), pl.BlockSpec(blk, bwd),
            pl.BlockSpec(blk, fwd), pl.BlockSpec(blk, bwd),
            pl.BlockSpec(blk, fwd), pl.BlockSpec(blk, bwd),
            pl.BlockSpec(lb_logits.shape, full3),
            pl.BlockSpec(mats.shape, full3),
            pl.BlockSpec(masks.shape, full4),
        ],
        out_specs=[pl.BlockSpec(blk, fwd), pl.BlockSpec(blk, bwd)],
        out_shape=[jax.ShapeDtypeStruct((bsz, seq_len, width), F32)] * 2,
        scratch_shapes=[pltpu.VMEM((2 * N_REC_HEADS, REC_HEAD_DIM, REC_HEAD_DIM), F32)],
        compiler_params=pltpu.CompilerParams(
            dimension_semantics=("parallel", "arbitrary"), vmem_limit_bytes=VMEM_LIMIT),
        name="hgrn2",
    )(rq, rq, ri, ri, zf, zb, lb_logits, mats, masks)


def _mix_out_kernel(x_ref, attn_ref, of_ref, ob_ref, rg_ref, ag_ref, rgain_ref, w_ref, out_ref):
    a = _rms(attn_ref[...], ag_ref[...]).astype(BF16)
    o = of_ref[...] + ob_ref[...]
    gate = rg_ref[...]
    gate = gate * jax.nn.sigmoid(gate)
    y = x_ref[...] + _dot(a, w_ref[0:ATTN_WIDTH, :])
    for hh in range(N_REC_HEADS):
        sl = slice(hh * REC_HEAD_DIM, (hh + 1) * REC_HEAD_DIM)
        r = (_rms(o[:, sl], rgain_ref[...]) * gate[:, sl]).astype(BF16)
        y += _dot(r, w_ref[ATTN_WIDTH + hh * REC_HEAD_DIM:ATTN_WIDTH + (hh + 1) * REC_HEAD_DIM, :])
    out_ref[...] = y


def _mix_out(x2d, attn, o_f, o_b, rg, attn_gain, rec_gain, w_out):
    n_tok, d = x2d.shape
    tm = MIX_OUT_TOKENS
    tok = lambda i: (i, 0)
    const = lambda i: (0, 0)
    return pl.pallas_call(
        _mix_out_kernel,
        grid=(n_tok // tm,),
        in_specs=[
            pl.BlockSpec((tm, d), tok),
            pl.BlockSpec((tm, ATTN_WIDTH), tok),
            pl.BlockSpec((tm, REC_WIDTH), tok),
            pl.BlockSpec((tm, REC_WIDTH), tok),
            pl.BlockSpec((tm, REC_WIDTH), tok),
            pl.BlockSpec((1, ATTN_WIDTH), const),
            pl.BlockSpec((1, REC_HEAD_DIM), const),
            pl.BlockSpec(w_out.shape, const),
        ],
        out_specs=pl.BlockSpec((tm, d), tok),
        out_shape=jax.ShapeDtypeStruct((n_tok, d), F32),
        compiler_params=pltpu.CompilerParams(
            dimension_semantics=("parallel",), vmem_limit_bytes=VMEM_LIMIT),
        name="mix_out",
    )(x2d, attn, o_f, o_b, rg, attn_gain.reshape(1, ATTN_WIDTH), rec_gain.reshape(1, REC_HEAD_DIM), w_out)


def kernel(x, ffn1_norm, ffn1_w_in, ffn1_w_out, mix_norm, w_in_mix, attn_q_norm, attn_k_norm,
           attn_out_norm, rec_lb_logits, rec_out_norm, w_out_mix, ffn2_norm, ffn2_w_in,
           ffn2_w_out, final_norm):
    bsz, seq_len, d = x.shape
    depth = ffn1_norm.shape[0]
    n_tok = bsz * seq_len
    h = x.reshape(n_tok, d)
    for l in range(depth):
        h = _ffn(h, ffn1_norm[l], ffn1_w_in[l].astype(BF16), ffn1_w_out[l].astype(BF16))
        q, k, v, rq, zf, zb, ri, rg = _mix_in(
            h, mix_norm[l], w_in_mix[l].astype(BF16), attn_q_norm[l], attn_k_norm[l], seq_len)
        to3 = lambda a: a.reshape(bsz, seq_len, a.shape[-1])
        attn = _attention(to3(q), to3(k), to3(v))
        o_f, o_b = _hgrn2(to3(rq), to3(zf), to3(zb), to3(ri), rec_lb_logits, l)
        to2 = lambda a: a.reshape(n_tok, a.shape[-1])
        h = _mix_out(h, to2(attn), to2(o_f), to2(o_b), rg, attn_out_norm[l], rec_out_norm[l],
                     w_out_mix[l].astype(BF16))
        h = _ffn(h, ffn2_norm[l], ffn2_w_in[l].astype(BF16), ffn2_w_out[l].astype(BF16),
                 final_gain=final_norm[l])
    return h.reshape(bsz, seq_len, d)
```

```python
import functools

import jax
import jax.numpy as jnp
import numpy as np
from jax import lax
from jax.experimental import pallas as pl
from jax.experimental.pallas import tpu as pltpu

F32 = jnp.float32
BF16 = jnp.bfloat16

EPS = 1e-6
GRID_W = 64
HEAD_DIM = 128
N_Q_HEADS = 4
N_KV_HEADS = 2
KV_GROUPS = N_Q_HEADS // N_KV_HEADS
ATTN_WIDTH = N_Q_HEADS * HEAD_DIM
KV_WIDTH = N_KV_HEADS * HEAD_DIM
ROPE_THETA = 10000.0
ROPE_AXIS_DIM = HEAD_DIM // 2
REC_HEAD_DIM = 128
N_REC_HEADS = 4
REC_WIDTH = N_REC_HEADS * REC_HEAD_DIM

V7X_LANES = 128

FFN_TOKENS = 1024
FFN_FF_CHUNK = 256
MIX_IN_TOKENS = 512
MIX_OUT_TOKENS = 1024
ATTN_Q_TOKENS = 512
ATTN_KV_TOKENS = 1024
ATTN_ROW_CHUNK = 128
LOG2_E = 1.4426950408889634
REC_CHUNK = 64
REC_LEVELS = 6

VMEM_LIMIT = 56 * 1024 * 1024


def _rms(x, gain):
    return x * lax.rsqrt(jnp.mean(x * x, axis=-1, keepdims=True) + EPS) * gain


def _dot(a, b):
    return jnp.dot(a, b, preferred_element_type=F32)


def _dot_nt(a, b):
    return lax.dot_general(a, b, (((1,), (1,)), ((), ())), preferred_element_type=F32)


def _dot_tn(a, b):
    return lax.dot_general(a, b, (((0,), (0,)), ((), ())), preferred_element_type=F32)


def _ffn_kernel(*refs, final_norm):
    if final_norm:
        x_ref, gain_ref, wg_ref, wu_ref, wo_ref, fgain_ref, out_ref, h_scr, acc_scr = refs
    else:
        x_ref, gain_ref, wg_ref, wu_ref, wo_ref, out_ref, h_scr, acc_scr = refs
    j = pl.program_id(1)

    @pl.when(j == 0)
    def _():
        h_scr[...] = _rms(x_ref[...], gain_ref[...]).astype(BF16)
        acc_scr[...] = jnp.zeros_like(acc_scr)

    h = h_scr[...]
    gate = _dot(h, wg_ref[...])
    up = _dot(h, wu_ref[...])
    act = (gate * jax.nn.sigmoid(gate)) * up
    acc_scr[...] += _dot(act.astype(BF16), wo_ref[...])

    @pl.when(j == pl.num_programs(1) - 1)
    def _():
        y = x_ref[...] + 0.5 * acc_scr[...]
        if final_norm:
            y = _rms(y, fgain_ref[...])
        out_ref[...] = y


def _ffn(x2d, gain, w_in, w_out, final_gain=None):
    n_tok, d = x2d.shape
    d_ff = w_out.shape[0]
    n_ff = d_ff // FFN_FF_CHUNK
    final_norm = final_gain is not None
    in_specs = [
        pl.BlockSpec((FFN_TOKENS, d), lambda i, j: (i, 0)),
        pl.BlockSpec((1, d), lambda i, j: (0, 0)),
        pl.BlockSpec((d, FFN_FF_CHUNK), lambda i, j: (0, j)),
        pl.BlockSpec((d, FFN_FF_CHUNK), lambda i, j: (0, j + n_ff)),
        pl.BlockSpec((FFN_FF_CHUNK, d), lambda i, j: (j, 0)),
    ]
    args = [x2d, gain.reshape(1, d), w_in, w_in, w_out]
    if final_norm:
        in_specs.append(pl.BlockSpec((1, d), lambda i, j: (0, 0)))
        args.append(final_gain.reshape(1, d))
    return pl.pallas_call(
        functools.partial(_ffn_kernel, final_norm=final_norm),
        grid=(n_tok // FFN_TOKENS, n_ff),
        in_specs=in_specs,
        out_specs=pl.BlockSpec((FFN_TOKENS, d), lambda i, j: (i, 0)),
        out_shape=jax.ShapeDtypeStruct((n_tok, d), F32),
        scratch_shapes=[pltpu.VMEM((FFN_TOKENS, d), BF16), pltpu.VMEM((FFN_TOKENS, d), F32)],
        compiler_params=pltpu.CompilerParams(
            dimension_semantics=("parallel", "arbitrary"), vmem_limit_bytes=VMEM_LIMIT),
        name="ffn_final" if final_norm else "ffn",
    )(*args)


def _rope(x, cos, sin_signed, first_half):
    swapped = jnp.where(first_half, pltpu.roll(x, 96, 1), pltpu.roll(x, 32, 1))
    return x * cos + swapped * sin_signed


def _mix_in_kernel(x_ref, gain_ref, w_ref, qg_ref, kg_ref, cos_ref, sin_ref,
                   q_ref, k_ref, v_ref, rq_ref, zf_ref, zb_ref, ri_ref, rg_ref):
    h = _rms(x_ref[...], gain_ref[...]).astype(BF16)
    cos = cos_ref[...]
    sin = sin_ref[...]
    lane = lax.broadcasted_iota(jnp.int32, cos.shape, 1)
    first_half = (lane % (ROPE_AXIS_DIM)) < (ROPE_AXIS_DIM // 2)

    def proj(c0, width):
        return _dot(h, w_ref[:, c0:c0 + width])

    aq = proj(0, ATTN_WIDTH)
    for hh in range(N_Q_HEADS):
        sl = slice(hh * HEAD_DIM, (hh + 1) * HEAD_DIM)
        qh = _rope(_rms(aq[:, sl], qg_ref[...]), cos, sin, first_half) * (HEAD_DIM ** -0.5 * LOG2_E)
        q_ref[:, sl] = qh.astype(BF16)
    c = ATTN_WIDTH
    ak = proj(c, KV_WIDTH)
    for hh in range(N_KV_HEADS):
        sl = slice(hh * HEAD_DIM, (hh + 1) * HEAD_DIM)
        k_ref[:, sl] = _rope(_rms(ak[:, sl], kg_ref[...]), cos, sin, first_half).astype(BF16)
    c += KV_WIDTH
    av = proj(c, KV_WIDTH).astype(BF16)
    for hh in range(N_KV_HEADS):
        v_ref[:, 2 * hh * HEAD_DIM:(2 * hh + 1) * HEAD_DIM] = av[:, hh * HEAD_DIM:(hh + 1) * HEAD_DIM]
        v_ref[:, (2 * hh + 1) * HEAD_DIM:(2 * hh + 2) * HEAD_DIM] = jnp.ones((av.shape[0], HEAD_DIM), BF16)
    c += KV_WIDTH
    for out_ref in (rq_ref, zf_ref, zb_ref, ri_ref, rg_ref):
        out_ref[...] = proj(c, REC_WIDTH)
        c += REC_WIDTH


def _rope_tables(seq_len):
    rows = seq_len // GRID_W
    row_ids = jnp.repeat(jnp.arange(rows, dtype=F32), GRID_W)
    col_ids = jnp.tile(jnp.arange(GRID_W, dtype=F32), rows)
    inv_freq = ROPE_THETA ** (-jnp.arange(0, ROPE_AXIS_DIM, 2, dtype=F32) / ROPE_AXIS_DIM)
    ang_r = row_ids[:, None] * inv_freq[None, :]
    ang_c = col_ids[:, None] * inv_freq[None, :]
    cos = jnp.concatenate([jnp.cos(ang_r), jnp.cos(ang_r), jnp.cos(ang_c), jnp.cos(ang_c)], axis=-1)
    sin = jnp.concatenate([-jnp.sin(ang_r), jnp.sin(ang_r), -jnp.sin(ang_c), jnp.sin(ang_c)], axis=-1)
    return cos, sin


def _mix_in(x2d, gain, w_in, q_gain, k_gain, seq_len):
    n_tok, d = x2d.shape
    d_in = w_in.shape[1]
    tm = MIX_IN_TOKENS
    cos, sin = _rope_tables(seq_len)
    tiles_per_seq = seq_len // tm
    tok = lambda i: (i, 0)
    const = lambda i: (0, 0)
    table = lambda i: (i % tiles_per_seq, 0)
    out_shape = [
        jax.ShapeDtypeStruct((n_tok, ATTN_WIDTH), BF16),
        jax.ShapeDtypeStruct((n_tok, KV_WIDTH), BF16),
        jax.ShapeDtypeStruct((n_tok, 2 * KV_WIDTH), BF16),
    ] + [jax.ShapeDtypeStruct((n_tok, REC_WIDTH), F32)] * 5
    out_specs = [
        pl.BlockSpec((tm, ATTN_WIDTH), tok),
        pl.BlockSpec((tm, KV_WIDTH), tok),
        pl.BlockSpec((tm, 2 * KV_WIDTH), tok),
    ] + [pl.BlockSpec((tm, REC_WIDTH), tok)] * 5
    return pl.pallas_call(
        _mix_in_kernel,
        grid=(n_tok // tm,),
        in_specs=[
            pl.BlockSpec((tm, d), tok),
            pl.BlockSpec((1, d), const),
            pl.BlockSpec((d, d_in), const),
            pl.BlockSpec((1, HEAD_DIM), const),
            pl.BlockSpec((1, HEAD_DIM), const),
            pl.BlockSpec((tm, HEAD_DIM), table),
            pl.BlockSpec((tm, HEAD_DIM), table),
        ],
        out_specs=out_specs,
        out_shape=out_shape,
        compiler_params=pltpu.CompilerParams(
            dimension_semantics=("parallel",), vmem_limit_bytes=VMEM_LIMIT),
        name="mix_in",
    )(x2d, gain.reshape(1, d), w_in, q_gain.reshape(1, HEAD_DIM), k_gain.reshape(1, HEAD_DIM), cos, sin)


def _attn_kernel(q_ref, k_ref, v_ref, o_ref, q2_scr, s_scr, p_scr, mt_scr, m_scr, acc_scr):
    tq = q_ref.shape[1]
    rows = KV_GROUPS * tq
    tk = s_scr.shape[2]
    n_kv = k_ref.shape[1] // tk

    for g in range(KV_GROUPS):
        q2_scr[g * tq:(g + 1) * tq, :] = q_ref[0, :, g * HEAD_DIM:(g + 1) * HEAD_DIM]
    m_scr[...] = jnp.full_like(m_scr, -jnp.inf)
    acc_scr[...] = jnp.zeros_like(acc_scr)

    def scores(j):
        s = _dot_nt(q2_scr[...], k_ref[0, j * tk:(j + 1) * tk, :])
        s_scr[j % 2] = s
        mt_scr[j % 2] = jnp.broadcast_to(jnp.max(s, axis=-1, keepdims=True), (rows, V7X_LANES))

    def accumulate(j):
        slot = j % 2
        m_prev = m_scr[...]
        m_new = jnp.maximum(m_prev, mt_scr[slot])
        m_scr[...] = m_new
        for r in range(0, rows, ATTN_ROW_CHUNK):
            m_r = m_new[r:r + ATTN_ROW_CHUNK]
            for c in range(0, tk, V7X_LANES):
                s_piece = s_scr[slot, r:r + ATTN_ROW_CHUNK, c:c + V7X_LANES]
                p_scr[slot, r:r + ATTN_ROW_CHUNK, c:c + V7X_LANES] = jnp.exp2(s_piece - m_r).astype(BF16)
        alpha = jnp.exp2(m_prev - m_new)
        pv = _dot(p_scr[slot], v_ref[0, j * tk:(j + 1) * tk, :])
        for half in range(2):
            sl = slice(half * HEAD_DIM, (half + 1) * HEAD_DIM)
            acc_scr[:, sl] = alpha * acc_scr[:, sl] + pv[:, sl]

    scores(0)
    for j in range(n_kv):
        if j + 1 < n_kv:
            scores(j + 1)
        accumulate(j)

    o = acc_scr[:, 0:HEAD_DIM] / acc_scr[:, HEAD_DIM:2 * HEAD_DIM]
    for g in range(KV_GROUPS):
        o_ref[0, :, g * HEAD_DIM:(g + 1) * HEAD_DIM] = o[g * tq:(g + 1) * tq]


def _attention(q, k, v_ext):
    bsz, seq_len, _ = q.shape
    tq, tk = ATTN_Q_TOKENS, ATTN_KV_TOKENS
    gw = KV_GROUPS * HEAD_DIM
    rows = KV_GROUPS * tq
    return pl.pallas_call(
        _attn_kernel,
        grid=(bsz, N_KV_HEADS, seq_len // tq),
        in_specs=[
            pl.BlockSpec((1, tq, gw), lambda b, h, qi: (b, qi, h)),
            pl.BlockSpec((1, seq_len, HEAD_DIM), lambda b, h, qi: (b, 0, h)),
            pl.BlockSpec((1, seq_len, 2 * HEAD_DIM), lambda b, h, qi: (b, 0, h)),
        ],
        out_specs=pl.BlockSpec((1, tq, gw), lambda b, h, qi: (b, qi, h)),
        out_shape=jax.ShapeDtypeStruct((bsz, seq_len, ATTN_WIDTH), F32),
        scratch_shapes=[
            pltpu.VMEM((rows, HEAD_DIM), BF16),
            pltpu.VMEM((2, rows, tk), F32),
            pltpu.VMEM((2, rows, tk), BF16),
            pltpu.VMEM((2, rows, V7X_LANES), F32),
            pltpu.VMEM((rows, V7X_LANES), F32),
            pltpu.VMEM((rows, 2 * HEAD_DIM), F32),
        ],
        compiler_params=pltpu.CompilerParams(
            dimension_semantics=("parallel", "parallel", "arbitrary"),
            vmem_limit_bytes=VMEM_LIMIT),
        name="attn",
    )(q, k, v_ext)


def _rec_constants():
    c = REC_CHUNK
    t = np.arange(c)[:, None]
    u = np.arange(c)[None, :]
    blocks = [(u <= t)]
    masks = []
    for lvl in range(REC_LEVELS):
        h = 1 << lvl
        base_t = t - t % (2 * h)
        mid = base_t + h
        upper = t >= mid
        blocks.append(np.where(upper, (u >= mid) & (u <= t), (u > t) & (u < mid)))
        masks.append((t // (2 * h) == u // (2 * h)) & (t % (2 * h) >= h) & (u % (2 * h) < h))
    blocks.append(u > t)
    masks.append(t == u)
    fwd = np.concatenate([b.astype(np.float32) for b in blocks], axis=0)
    fwd_masks = np.stack([m.astype(np.float32) for m in masks])
    n_rows = fwd.shape[0]
    bwd = fwd.reshape(-1, c, c)[:, ::-1, ::-1].reshape(n_rows, c)
    bwd_masks = fwd_masks[:, ::-1, ::-1]
    total = np.ones((8, c), np.float32)
    mats = np.stack([np.concatenate([fwd, total]), np.concatenate([bwd, total])])
    mats = np.concatenate([mats, mats, mats], axis=-1)
    return mats, np.stack([fwd_masks, bwd_masks])


def _split3(x):
    hi = x.astype(BF16)
    r1 = x - hi.astype(F32)
    mid = r1.astype(BF16)
    lo = (r1 - mid.astype(F32)).astype(BF16)
    return jnp.concatenate([hi, mid, lo], axis=0)


def _rec_kernel(qf_ref, qb_ref, vf_ref, vb_ref, zf_ref, zb_ref, lbl_ref, mat_ref, mask_ref,
                of_ref, ob_ref, state_scr, *, layer):
    c = REC_CHUNK

    @pl.when(pl.program_id(1) == 0)
    def _():
        state_scr[...] = jnp.zeros_like(state_scr)

    dirs = ((qf_ref, vf_ref, zf_ref, of_ref), (qb_ref, vb_ref, zb_ref, ob_ref))
    for d, (q_ref, v_ref, z_ref, o_ref) in enumerate(dirs):
        logits = lbl_ref[d]
        e = jnp.exp(logits - jnp.max(logits, axis=0, keepdims=True))
        lb = jnp.sum(e[:layer + 1], axis=0, keepdims=True) / jnp.sum(e, axis=0, keepdims=True)
        f = lb + (1.0 - lb) * jax.nn.sigmoid(z_ref[0])
        g = jnp.log(f)
        kk = 1.0 - f
        sums = _dot(mat_ref[d], _split3(g))
        q = q_ref[0]
        v = v_ref[0].astype(BF16)
        for hh in range(N_REC_HEADS):
            sl = slice(hh * REC_HEAD_DIM, (hh + 1) * REC_HEAD_DIM)
            qh = q[:, sl]
            kh = kk[:, sl]
            vh = v[:, sl]
            scores = mask_ref[d, REC_LEVELS] * _dot_nt(qh.astype(BF16), kh.astype(BF16))
            for lvl in range(REC_LEVELS):
                el = jnp.exp(sums[(lvl + 1) * c:(lvl + 2) * c, sl])
                scores += mask_ref[d, lvl] * _dot_nt((qh * el).astype(BF16), (kh * el).astype(BF16))
            cum = sums[0:c, sl]
            rem = sums[(REC_LEVELS + 1) * c:(REC_LEVELS + 2) * c, sl]
            tot = sums[(REC_LEVELS + 2) * c:(REC_LEVELS + 2) * c + 1, sl]
            state_t = state_scr[d * N_REC_HEADS + hh]
            inter = _dot_nt((qh * jnp.exp(cum)).astype(BF16), state_t.astype(BF16))
            o_ref[0, :, sl] = inter + _dot(scores.astype(BF16), vh)
            k_tail = (kh * jnp.exp(rem)).astype(BF16)
            state_scr[d * N_REC_HEADS + hh] = state_t * jnp.exp(tot) + _dot_tn(vh, k_tail)


def _hgrn2(rq, zf, zb, ri, lb_logits, layer):
    bsz, seq_len, width = rq.shape
    c = REC_CHUNK
    nc = seq_len // c
    mats, masks = _rec_constants()
    mats = jnp.asarray(mats, BF16)
    masks = jnp.asarray(masks, F32)
    fwd = lambda b, i: (b, i, 0)
    bwd = lambda b, i: (b, nc - 1 - i, 0)
    blk = (1, c, width)
    full3 = lambda b, i: (0, 0, 0)
    full4 = lambda b, i: (0, 0, 0, 0)
    return pl.pallas_call(
        functools.partial(_rec_kernel, layer=layer),
        grid=(bsz, nc),
        in_specs=[
            pl.BlockSpec(blk, fwd), pl.BlockSpec(blk, bwd),
            pl.BlockSpec(blk, fwd), pl.BlockSpec(blk, bwd),
            pl.BlockSpec(blk, fwd), pl.BlockSpec(blk, bwd),
            pl.BlockSpec(lb_logits.shape, full3),
            pl.BlockSpec(mats.shape, full3),
            pl.BlockSpec(masks.shape, full4),
        ],
        out_specs=[pl.BlockSpec(blk, fwd), pl.BlockSpec(blk, bwd)],
        out_shape=[jax.ShapeDtypeStruct((bsz, seq_len, width), F32)] * 2,
        scratch_shapes=[pltpu.VMEM((2 * N_REC_HEADS, REC_HEAD_DIM, REC_HEAD_DIM), F32)],
        compiler_params=pltpu.CompilerParams(
            dimension_semantics=("parallel", "arbitrary"), vmem_limit_bytes=VMEM_LIMIT),
        name="hgrn2",
    )(rq, rq, ri, ri, zf, zb, lb_logits, mats, masks)


def _mix_out_kernel(x_ref, attn_ref, of_ref, ob_ref, rg_ref, ag_ref, rgain_ref, w_ref, out_ref):
    a = _rms(attn_ref[...], ag_ref[...]).astype(BF16)
    o = of_ref[...] + ob_ref[...]
    gate = rg_ref[...]
    gate = gate * jax.nn.sigmoid(gate)
    y = x_ref[...] + _dot(a, w_ref[0:ATTN_WIDTH, :])
    for hh in range(N_REC_HEADS):
        sl = slice(hh * REC_HEAD_DIM, (hh + 1) * REC_HEAD_DIM)
        r = (_rms(o[:, sl], rgain_ref[...]) * gate[:, sl]).astype(BF16)
        y += _dot(r, w_ref[ATTN_WIDTH + hh * REC_HEAD_DIM:ATTN_WIDTH + (hh + 1) * REC_HEAD_DIM, :])
    out_ref[...] = y


def _mix_out(x2d, attn, o_f, o_b, rg, attn_gain, rec_gain, w_out):
    n_tok, d = x2d.shape
    tm = MIX_OUT_TOKENS
    tok = lambda i: (i, 0)
    const = lambda i: (0, 0)
    return pl.pallas_call(
        _mix_out_kernel,
        grid=(n_tok // tm,),
        in_specs=[
            pl.BlockSpec((tm, d), tok),
            pl.BlockSpec((tm, ATTN_WIDTH), tok),
            pl.BlockSpec((tm, REC_WIDTH), tok),
            pl.BlockSpec((tm, REC_WIDTH), tok),
            pl.BlockSpec((tm, REC_WIDTH), tok),
            pl.BlockSpec((1, ATTN_WIDTH), const),
            pl.BlockSpec((1, REC_HEAD_DIM), const),
            pl.BlockSpec(w_out.shape, const),
        ],
        out_specs=pl.BlockSpec((tm, d), tok),
        out_shape=jax.ShapeDtypeStruct((n_tok, d), F32),
        compiler_params=pltpu.CompilerParams(
            dimension_semantics=("parallel",), vmem_limit_bytes=VMEM_LIMIT),
        name="mix_out",
    )(x2d, attn, o_f, o_b, rg, attn_gain.reshape(1, ATTN_WIDTH), rec_gain.reshape(1, REC_HEAD_DIM), w_out)


def kernel(x, ffn1_norm, ffn1_w_in, ffn1_w_out, mix_norm, w_in_mix, attn_q_norm, attn_k_norm,
           attn_out_norm, rec_lb_logits, rec_out_norm, w_out_mix, ffn2_norm, ffn2_w_in,
           ffn2_w_out, final_norm):
    bsz, seq_len, d = x.shape
    depth = ffn1_norm.shape[0]
    n_tok = bsz * seq_len
    h = x.reshape(n_tok, d)
    for l in range(depth):
        h = _ffn(h, ffn1_norm[l], ffn1_w_in[l].astype(BF16), ffn1_w_out[l].astype(BF16))
        q, k, v, rq, zf, zb, ri, rg = _mix_in(
            h, mix_norm[l], w_in_mix[l].astype(BF16), attn_q_norm[l], attn_k_norm[l], seq_len)
        to3 = lambda a: a.reshape(bsz, seq_len, a.shape[-1])
        attn = _attention(to3(q), to3(k), to3(v))
        o_f, o_b = _hgrn2(to3(rq), to3(zf), to3(zb), to3(ri), rec_lb_logits, l)
        to2 = lambda a: a.reshape(n_tok, a.shape[-1])
        h = _mix_out(h, to2(attn), to2(o_f), to2(o_b), rg, attn_out_norm[l], rec_out_norm[l],
                     w_out_mix[l].astype(BF16))
        h = _ffn(h, ffn2_norm[l], ffn2_w_in[l].astype(BF16), ffn2_w_out[l].astype(BF16),
                 final_gain=final_norm[l])
    return h.reshape(bsz, seq_len, d)
```

```python
import functools

import jax
import jax.numpy as jnp
import numpy as np
from jax import lax
from jax.experimental import pallas as pl
from jax.experimental.pallas import tpu as pltpu

F32 = jnp.float32
BF16 = jnp.bfloat16

EPS = 1e-6
GRID_W = 64
HEAD_DIM = 128
N_Q_HEADS = 4
N_KV_HEADS = 2
KV_GROUPS = N_Q_HEADS // N_KV_HEADS
ATTN_WIDTH = N_Q_HEADS * HEAD_DIM
KV_WIDTH = N_KV_HEADS * HEAD_DIM
ROPE_THETA = 10000.0
ROPE_AXIS_DIM = HEAD_DIM // 2
REC_HEAD_DIM = 128
N_REC_HEADS = 4
REC_WIDTH = N_REC_HEADS * REC_HEAD_DIM

V7X_LANES = 128

FFN_TOKENS = 1024
FFN_SUB_ROWS = 512
FFN_MIX_TOKENS = 512
FFN_MIX_SUB_ROWS = 256
FFN_FF_CHUNK = 512
MIX_IN_TOKENS = 512
ATTN_Q_TOKENS = 512
ATTN_KV_TOKENS = 1024
ATTN_ROW_CHUNK = 128
LOG2_E = 1.4426950408889634
REC_CHUNK = 64
REC_LEVELS = 6

VMEM_LIMIT = 56 * 1024 * 1024


def _rms(x, gain):
    return x * lax.rsqrt(jnp.mean(x * x, axis=-1, keepdims=True) + EPS) * gain


def _dot(a, b):
    return jnp.dot(a, b, preferred_element_type=F32)


def _dot_nt(a, b):
    return lax.dot_general(a, b, (((1,), (1,)), ((), ())), preferred_element_type=F32)


def _dot_tn(a, b):
    return lax.dot_general(a, b, (((0,), (0,)), ((), ())), preferred_element_type=F32)


def _mixer_out_rows(rows, attn_ref, of_ref, ob_ref, rg_ref, ag_ref, rgain_ref, wmix_ref):
    parts = [_rms(attn_ref[rows, :], ag_ref[...]).astype(BF16)]
    for hh in range(N_REC_HEADS):
        sl = slice(hh * REC_HEAD_DIM, (hh + 1) * REC_HEAD_DIM)
        gate = rg_ref[rows, sl]
        o = of_ref[rows, sl] + ob_ref[rows, sl]
        parts.append((_rms(o, rgain_ref[...]) * (gate * jax.nn.sigmoid(gate))).astype(BF16))
    return _dot(jnp.concatenate(parts, axis=-1), wmix_ref[...])


def _ffn_kernel(*refs, mixer, final_norm, sub_rows, ff_chunks):
    refs = list(refs)
    x_ref = refs.pop(0)
    mix_refs = [refs.pop(0) for _ in range(7)] if mixer else None
    gain_ref, win_ref, wout_ref = refs.pop(0), refs.pop(0), refs.pop(0)
    fgain_ref = refs.pop(0) if final_norm else None
    out_ref, acc_scr = refs
    d_ff = wout_ref.shape[0]

    for sub in range(x_ref.shape[0] // sub_rows):
        rows = slice(sub * sub_rows, (sub + 1) * sub_rows)
        x = x_ref[rows, :]
        res_ref = x_ref
        if mixer:
            x = x + _mixer_out_rows(rows, *mix_refs)
            out_ref[rows, :] = x
            res_ref = out_ref
        h = _rms(x, gain_ref[...]).astype(BF16)
        c0 = 0
        for width in ff_chunks:
            gate = _dot(h, win_ref[:, c0:c0 + width])
            up = _dot(h, win_ref[:, d_ff + c0:d_ff + c0 + width])
            act = ((gate * jax.nn.sigmoid(gate)) * up).astype(BF16)
            contrib = _dot(act, wout_ref[c0:c0 + width, :])
            if c0 == 0:
                acc_scr[sub] = contrib
            else:
                acc_scr[sub] += contrib
            c0 += width
        y = res_ref[rows, :] + 0.5 * acc_scr[sub]
        if final_norm:
            y = _rms(y, fgain_ref[...])
        out_ref[rows, :] = y


def _ffn_chunks(d_ff):
    n_full, rest = divmod(d_ff, FFN_FF_CHUNK)
    assert rest % V7X_LANES == 0
    return (FFN_FF_CHUNK,) * n_full + ((rest,) if rest else ())


def _ffn(x2d, gain, w_in, w_out, *, tokens, sub_rows, mixer_args=None, final_gain=None):
    n_tok, d = x2d.shape
    d_ff = w_out.shape[0]
    tok = lambda i: (i, 0)
    const = lambda i: (0, 0)
    resident = lambda shape: pl.BlockSpec(shape, const, pipeline_mode=pl.Buffered(1))
    in_specs = [pl.BlockSpec((tokens, d), tok)]
    args = [x2d]
    if mixer_args is not None:
        attn, o_f, o_b, rg, attn_gain, rec_gain, w_mix = mixer_args
        in_specs += [pl.BlockSpec((tokens, ATTN_WIDTH), tok)] + [pl.BlockSpec((tokens, REC_WIDTH), tok)] * 3
        in_specs += [pl.BlockSpec((1, ATTN_WIDTH), const), pl.BlockSpec((1, REC_HEAD_DIM), const),
                     resident(w_mix.shape)]
        args += [attn, o_f, o_b, rg, attn_gain.reshape(1, ATTN_WIDTH), rec_gain.reshape(1, REC_HEAD_DIM), w_mix]
    in_specs += [pl.BlockSpec((1, d), const), resident(w_in.shape), resident(w_out.shape)]
    args += [gain.reshape(1, d), w_in, w_out]
    if final_gain is not None:
        in_specs.append(pl.BlockSpec((1, d), const))
        args.append(final_gain.reshape(1, d))
    body = functools.partial(
        _ffn_kernel, mixer=mixer_args is not None, final_norm=final_gain is not None,
        sub_rows=sub_rows, ff_chunks=_ffn_chunks(d_ff))
    return pl.pallas_call(
        body,
        grid=(n_tok // tokens,),
        in_specs=in_specs,
        out_specs=pl.BlockSpec((tokens, d), tok),
        out_shape=jax.ShapeDtypeStruct((n_tok, d), F32),
        scratch_shapes=[pltpu.VMEM((tokens // sub_rows, sub_rows, d), F32)],
        compiler_params=pltpu.CompilerParams(
            dimension_semantics=("parallel",), vmem_limit_bytes=VMEM_LIMIT),
        name="ffn_mix" if mixer_args is not None else "ffn",
    )(*args)


def _rope(x, cos, sin_signed, first_half):
    swapped = jnp.where(first_half, pltpu.roll(x, 96, 1), pltpu.roll(x, 32, 1))
    return x * cos + swapped * sin_signed


def _mix_in_kernel(x_ref, gain_ref, w_ref, qg_ref, kg_ref, cos_ref, sin_ref,
                   q_ref, k_ref, v_ref, rq_ref, zf_ref, zb_ref, ri_ref, rg_ref):
    h = _rms(x_ref[...], gain_ref[...]).astype(BF16)
    cos = cos_ref[...]
    sin = sin_ref[...]
    lane = lax.broadcasted_iota(jnp.int32, cos.shape, 1)
    first_half = (lane % (ROPE_AXIS_DIM)) < (ROPE_AXIS_DIM // 2)

    def proj(c0, width):
        return _dot(h, w_ref[:, c0:c0 + width])

    aq = proj(0, ATTN_WIDTH)
    for hh in range(N_Q_HEADS):
        sl = slice(hh * HEAD_DIM, (hh + 1) * HEAD_DIM)
        qh = _rope(_rms(aq[:, sl], qg_ref[...]), cos, sin, first_half) * (HEAD_DIM ** -0.5 * LOG2_E)
        q_ref[:, sl] = qh.astype(BF16)
    c = ATTN_WIDTH
    ak = proj(c, KV_WIDTH)
    for hh in range(N_KV_HEADS):
        sl = slice(hh * HEAD_DIM, (hh + 1) * HEAD_DIM)
        k_ref[:, sl] = _rope(_rms(ak[:, sl], kg_ref[...]), cos, sin, first_half).astype(BF16)
    c += KV_WIDTH
    av = proj(c, KV_WIDTH).astype(BF16)
    for hh in range(N_KV_HEADS):
        v_ref[:, 2 * hh * HEAD_DIM:(2 * hh + 1) * HEAD_DIM] = av[:, hh * HEAD_DIM:(hh + 1) * HEAD_DIM]
        v_ref[:, (2 * hh + 1) * HEAD_DIM:(2 * hh + 2) * HEAD_DIM] = jnp.ones((av.shape[0], HEAD_DIM), BF16)
    c += KV_WIDTH
    for out_ref in (rq_ref, zf_ref, zb_ref, ri_ref, rg_ref):
        out_ref[...] = proj(c, REC_WIDTH)
        c += REC_WIDTH


def _rope_tables(seq_len):
    rows = seq_len // GRID_W
    inv_freq = ROPE_THETA ** (-jnp.arange(0, ROPE_AXIS_DIM, 2, dtype=F32) / ROPE_AXIS_DIM)
    ang_r = jnp.arange(rows, dtype=F32)[:, None] * inv_freq[None, :]
    ang_c = jnp.arange(GRID_W, dtype=F32)[:, None] * inv_freq[None, :]
    half = ROPE_AXIS_DIM // 2

    def expand(row_part, col_part):
        r = jnp.broadcast_to(row_part[:, None, :], (rows, GRID_W, 2 * half))
        c = jnp.broadcast_to(col_part[None, :, :], (rows, GRID_W, 2 * half))
        return jnp.concatenate([r, c], axis=-1).reshape(seq_len, 4 * half)

    cos = expand(jnp.concatenate([jnp.cos(ang_r)] * 2, -1), jnp.concatenate([jnp.cos(ang_c)] * 2, -1))
    sin = expand(jnp.concatenate([-jnp.sin(ang_r), jnp.sin(ang_r)], -1),
                 jnp.concatenate([-jnp.sin(ang_c), jnp.sin(ang_c)], -1))
    return cos, sin


def _mix_in(x2d, gain, w_in, q_gain, k_gain, seq_len):
    n_tok, d = x2d.shape
    d_in = w_in.shape[1]
    tm = MIX_IN_TOKENS
    cos, sin = _rope_tables(seq_len)
    tiles_per_seq = seq_len // tm
    tok = lambda i: (i, 0)
    const = lambda i: (0, 0)
    table = lambda i: (i % tiles_per_seq, 0)
    out_shape = [
        jax.ShapeDtypeStruct((n_tok, ATTN_WIDTH), BF16),
        jax.ShapeDtypeStruct((n_tok, KV_WIDTH), BF16),
        jax.ShapeDtypeStruct((n_tok, 2 * KV_WIDTH), BF16),
    ] + [jax.ShapeDtypeStruct((n_tok, REC_WIDTH), F32)] * 5
    out_specs = [
        pl.BlockSpec((tm, ATTN_WIDTH), tok),
        pl.BlockSpec((tm, KV_WIDTH), tok),
        pl.BlockSpec((tm, 2 * KV_WIDTH), tok),
    ] + [pl.BlockSpec((tm, REC_WIDTH), tok)] * 5
    return pl.pallas_call(
        _mix_in_kernel,
        grid=(n_tok // tm,),
        in_specs=[
            pl.BlockSpec((tm, d), tok),
            pl.BlockSpec((1, d), const),
            pl.BlockSpec((d, d_in), const),
            pl.BlockSpec((1, HEAD_DIM), const),
            pl.BlockSpec((1, HEAD_DIM), const),
            pl.BlockSpec((tm, HEAD_DIM), table),
            pl.BlockSpec((tm, HEAD_DIM), table),
        ],
        out_specs=out_specs,
        out_shape=out_shape,
        compiler_params=pltpu.CompilerParams(
            dimension_semantics=("parallel",), vmem_limit_bytes=VMEM_LIMIT),
        name="mix_in",
    )(x2d, gain.reshape(1, d), w_in, q_gain.reshape(1, HEAD_DIM), k_gain.reshape(1, HEAD_DIM), cos, sin)


def _attn_kernel(q_ref, k_ref, v_ref, o_ref, q2_scr, s_scr, p_scr, mt_scr, m_scr, acc_scr):
    tq = q_ref.shape[1]
    rows = KV_GROUPS * tq
    tk = s_scr.shape[2]
    n_kv = k_ref.shape[1] // tk

    for g in range(KV_GROUPS):
        q2_scr[g * tq:(g + 1) * tq, :] = q_ref[0, :, g * HEAD_DIM:(g + 1) * HEAD_DIM]
    m_scr[...] = jnp.full_like(m_scr, -jnp.inf)
    acc_scr[...] = jnp.zeros_like(acc_scr)

    def scores(j):
        s = _dot_nt(q2_scr[...], k_ref[0, j * tk:(j + 1) * tk, :])
        s_scr[j % 2] = s
        mt_scr[j % 2] = jnp.broadcast_to(jnp.max(s, axis=-1, keepdims=True), (rows, V7X_LANES))

    def accumulate(j):
        slot = j % 2
        m_prev = m_scr[...]
        m_new = jnp.maximum(m_prev, mt_scr[slot])
        m_scr[...] = m_new
        for r in range(0, rows, ATTN_ROW_CHUNK):
            m_r = m_new[r:r + ATTN_ROW_CHUNK]
            for c in range(0, tk, V7X_LANES):
                s_piece = s_scr[slot, r:r + ATTN_ROW_CHUNK, c:c + V7X_LANES]
                p_scr[slot, r:r + ATTN_ROW_CHUNK, c:c + V7X_LANES] = jnp.exp2(s_piece - m_r).astype(BF16)
        alpha = jnp.exp2(m_prev - m_new)
        pv = _dot(p_scr[slot], v_ref[0, j * tk:(j + 1) * tk, :])
        for half in range(2):
            sl = slice(half * HEAD_DIM, (half + 1) * HEAD_DIM)
            acc_scr[:, sl] = alpha * acc_scr[:, sl] + pv[:, sl]

    scores(0)
    for j in range(n_kv):
        if j + 1 < n_kv:
            scores(j + 1)
        accumulate(j)

    o = acc_scr[:, 0:HEAD_DIM] / acc_scr[:, HEAD_DIM:2 * HEAD_DIM]
    for g in range(KV_GROUPS):
        o_ref[0, :, g * HEAD_DIM:(g + 1) * HEAD_DIM] = o[g * tq:(g + 1) * tq]


def _attention(q, k, v_ext):
    bsz, seq_len, _ = q.shape
    tq, tk = ATTN_Q_TOKENS, ATTN_KV_TOKENS
    gw = KV_GROUPS * HEAD_DIM
    rows = KV_GROUPS * tq
    return pl.pallas_call(
        _attn_kernel,
        grid=(bsz, N_KV_HEADS, seq_len // tq),
        in_specs=[
            pl.BlockSpec((1, tq, gw), lambda b, h, qi: (b, qi, h)),
            pl.BlockSpec((1, seq_len, HEAD_DIM), lambda b, h, qi: (b, 0, h)),
            pl.BlockSpec((1, seq_len, 2 * HEAD_DIM), lambda b, h, qi: (b, 0, h)),
        ],
        out_specs=pl.BlockSpec((1, tq, gw), lambda b, h, qi: (b, qi, h)),
        out_shape=jax.ShapeDtypeStruct((bsz, seq_len, ATTN_WIDTH), F32),
        scratch_shapes=[
            pltpu.VMEM((rows, HEAD_DIM), BF16),
            pltpu.VMEM((2, rows, tk), F32),
            pltpu.VMEM((2, rows, tk), BF16),
            pltpu.VMEM((2, rows, V7X_LANES), F32),
            pltpu.VMEM((rows, V7X_LANES), F32),
            pltpu.VMEM((rows, 2 * HEAD_DIM), F32),
        ],
        compiler_params=pltpu.CompilerParams(
            dimension_semantics=("parallel", "parallel", "arbitrary"),
            vmem_limit_bytes=VMEM_LIMIT),
        name="attn",
    )(q, k, v_ext)


def _rec_constants():
    c = REC_CHUNK
    t = np.arange(c)[:, None]
    u = np.arange(c)[None, :]
    blocks = [(u <= t)]
    masks = []
    for lvl in range(REC_LEVELS):
        h = 1 << lvl
        base_t = t - t % (2 * h)
        mid = base_t + h
        upper = t >= mid
        blocks.append(np.where(upper, (u >= mid) & (u <= t), (u > t) & (u < mid)))
        masks.append((t // (2 * h) == u // (2 * h)) & (t % (2 * h) >= h) & (u % (2 * h) < h))
    blocks.append(u > t)
    masks.append(t == u)
    fwd = np.concatenate([b.astype(np.float32) for b in blocks], axis=0)
    fwd_masks = np.stack([m.astype(np.float32) for m in masks])
    n_rows = fwd.shape[0]
    bwd = fwd.reshape(-1, c, c)[:, ::-1, ::-1].reshape(n_rows, c)
    bwd_masks = fwd_masks[:, ::-1, ::-1]
    total = np.ones((8, c), np.float32)
    mats = np.stack([np.concatenate([fwd, total]), np.concatenate([bwd, total])])
    mats = np.concatenate([mats, mats, mats], axis=-1)
    return mats, np.stack([fwd_masks, bwd_masks])


def _split3(x):
    hi = x.astype(BF16)
    r1 = x - hi.astype(F32)
    mid = r1.astype(BF16)
    lo = (r1 - mid.astype(F32)).astype(BF16)
    return jnp.concatenate([hi, mid, lo], axis=0)


def _rec_kernel(qf_ref, qb_ref, vf_ref, vb_ref, zf_ref, zb_ref, lbl_ref, mat_ref, mask_ref,
                of_ref, ob_ref, state_scr, *, layer):
    c = REC_CHUNK

    @pl.when(pl.program_id(1) == 0)
    def _():
        state_scr[...] = jnp.zeros_like(state_scr)

    dirs = ((qf_ref, vf_ref, zf_ref, of_ref), (qb_ref, vb_ref, zb_ref, ob_ref))
    for d, (q_ref, v_ref, z_ref, o_ref) in enumerate(dirs):
        logits = lbl_ref[d]
        e = jnp.exp(logits - jnp.max(logits, axis=0, keepdims=True))
        lb = jnp.sum(e[:layer + 1], axis=0, keepdims=True) / jnp.sum(e, axis=0, keepdims=True)
        f = lb + (1.0 - lb) * jax.nn.sigmoid(z_ref[0])
        g = jnp.log(f)
        kk = 1.0 - f
        sums = _dot(mat_ref[d], _split3(g))
        q = q_ref[0]
        v = v_ref[0].astype(BF16)
        for hh in range(N_REC_HEADS):
            sl = slice(hh * REC_HEAD_DIM, (hh + 1) * REC_HEAD_DIM)
            qh = q[:, sl]
            kh = kk[:, sl]
            vh = v[:, sl]
            scores = mask_ref[d, REC_LEVELS] * _dot_nt(qh.astype(BF16), kh.astype(BF16))
            for lvl in range(REC_LEVELS):
                el = jnp.exp(sums[(lvl + 1) * c:(lvl + 2) * c, sl])
                scores += mask_ref[d, lvl] * _dot_nt((qh * el).astype(BF16), (kh * el).astype(BF16))
            cum = sums[0:c, sl]
            rem = sums[(REC_LEVELS + 1) * c:(REC_LEVELS + 2) * c, sl]
            tot = sums[(REC_LEVELS + 2) * c:(REC_LEVELS + 2) * c + 1, sl]
            state_t = state_scr[d * N_REC_HEADS + hh]
            inter = _dot_nt((qh * jnp.exp(cum)).astype(BF16), state_t.astype(BF16))
            o_ref[0, :, sl] = inter + _dot(scores.astype(BF16), vh)
            k_tail = (kh * jnp.exp(rem)).astype(BF16)
            state_scr[d * N_REC_HEADS + hh] = state_t * jnp.exp(tot) + _dot_tn(vh, k_tail)


def _hgrn2(rq, zf, zb, ri, lb_logits, layer):
    bsz, seq_len, width = rq.shape
    c = REC_CHUNK
    nc = seq_len // c
    mats, masks = _rec_constants()
    mats = jnp.asarray(mats, BF16)
    masks = jnp.asarray(masks, F32)
    fwd = lambda b, i: (b, i, 0)
    bwd = lambda b, i: (b, nc - 1 - i, 0)
    blk = (1, c, width)
    full3 = lambda b, i: (0, 0, 0)
    full4 = lambda b, i: (0, 0, 0, 0)
    return pl.pallas_call(
        functools.partial(_rec_kernel, layer=layer),
        grid=(bsz, nc),
        in_specs=[
            pl.BlockSpec(blk, fwd), pl.BlockSpec(blk, bwd),
            pl.BlockSpec(blk, fwd), pl.BlockSpec(blk, bwd),
            pl.BlockSpec(blk, fwd), pl.BlockSpec(blk, bwd),
            pl.BlockSpec(lb_logits.shape, full3),
            pl.BlockSpec(mats.shape, full3),
            pl.BlockSpec(masks.shape, full4),
        ],
        out_specs=[pl.BlockSpec(blk, fwd), pl.BlockSpec(blk, bwd)],
        out_shape=[jax.ShapeDtypeStruct((bsz, seq_len, width), F32)] * 2,
        scratch_shapes=[pltpu.VMEM((2 * N_REC_HEADS, REC_HEAD_DIM, REC_HEAD_DIM), F32)],
        compiler_params=pltpu.CompilerParams(
            dimension_semantics=("parallel", "arbitrary"), vmem_limit_bytes=VMEM_LIMIT),
        name="hgrn2",
    )(rq, rq, ri, ri, zf, zb, lb_logits, mats, masks)


def kernel(x, ffn1_norm, ffn1_w_in, ffn1_w_out, mix_norm, w_in_mix, attn_q_norm, attn_k_norm,
           attn_out_norm, rec_lb_logits, rec_out_norm, w_out_mix, ffn2_norm, ffn2_w_in,
           ffn2_w_out, final_norm):
    bsz, seq_len, d = x.shape
    depth = ffn1_norm.shape[0]
    n_tok = bsz * seq_len
    h = x.reshape(n_tok, d)
    to3 = lambda a: a.reshape(bsz, seq_len, a.shape[-1])
    to2 = lambda a: a.reshape(n_tok, a.shape[-1])
    for l in range(depth):
        h = _ffn(h, ffn1_norm[l], ffn1_w_in[l].astype(BF16), ffn1_w_out[l].astype(BF16),
                 tokens=FFN_TOKENS, sub_rows=FFN_SUB_ROWS)
        q, k, v_ext, rq, zf, zb, ri, rg = _mix_in(
            h, mix_norm[l], w_in_mix[l].astype(BF16), attn_q_norm[l], attn_k_norm[l], seq_len)
        attn = _attention(to3(q), to3(k), to3(v_ext))
        o_f, o_b = _hgrn2(to3(rq), to3(zf), to3(zb), to3(ri), rec_lb_logits, l)
        mixer_args = (to2(attn), to2(o_f), to2(o_b), rg, attn_out_norm[l], rec_out_norm[l],
                      w_out_mix[l].astype(BF16))
        h = _ffn(h, ffn2_norm[l], ffn2_w_in[l].astype(BF16), ffn2_w_out[l].astype(BF16),
                 tokens=FFN_MIX_TOKENS, sub_rows=FFN_MIX_SUB_ROWS, mixer_args=mixer_args,
                 final_gain=final_norm[l])
    return h.reshape(bsz, seq_len, d)
```

```python
import functools

import jax
import jax.numpy as jnp
import numpy as np
from jax import lax
from jax.experimental import pallas as pl
from jax.experimental.pallas import tpu as pltpu

F32 = jnp.float32
BF16 = jnp.bfloat16

EPS = 1e-6
GRID_W = 64
HEAD_DIM = 128
N_Q_HEADS = 4
N_KV_HEADS = 2
KV_GROUPS = N_Q_HEADS // N_KV_HEADS
ATTN_WIDTH = N_Q_HEADS * HEAD_DIM
KV_WIDTH = N_KV_HEADS * HEAD_DIM
ROPE_THETA = 10000.0
ROPE_AXIS_DIM = HEAD_DIM // 2
REC_HEAD_DIM = 128
N_REC_HEADS = 4
REC_WIDTH = N_REC_HEADS * REC_HEAD_DIM

V7X_LANES = 128

FFN_TOKENS = 1024
FFN_SUB_ROWS = 512
FFN_MIX_TOKENS = 512
FFN_MIX_SUB_ROWS = 256
FFN_FF_CHUNK = 512
MIX_IN_TOKENS = 512
ATTN_Q_TOKENS = 512
ATTN_KV_TOKENS = 1024
ATTN_ROW_CHUNK = 128
LOG2_E = 1.4426950408889634
REC_CHUNK = 64
REC_LEVELS = 6
REC_FAST_RANGE = 100.0

VMEM_LIMIT = 56 * 1024 * 1024


def _rms(x, gain):
    return x * lax.rsqrt(jnp.mean(x * x, axis=-1, keepdims=True) + EPS) * gain


def _dot(a, b):
    return jnp.dot(a, b, preferred_element_type=F32)


def _dot_nt(a, b):
    return lax.dot_general(a, b, (((1,), (1,)), ((), ())), preferred_element_type=F32)


def _dot_tn(a, b):
    return lax.dot_general(a, b, (((0,), (0,)), ((), ())), preferred_element_type=F32)


def _mixer_out_rows(rows, attn_ref, of_ref, ob_ref, rg_ref, ag_ref, rgain_ref, wmix_ref):
    parts = [_rms(attn_ref[rows, :], ag_ref[...]).astype(BF16)]
    for hh in range(N_REC_HEADS):
        sl = slice(hh * REC_HEAD_DIM, (hh + 1) * REC_HEAD_DIM)
        gate = rg_ref[rows, sl]
        o = of_ref[rows, sl] + ob_ref[rows, sl]
        parts.append((_rms(o, rgain_ref[...]) * (gate * jax.nn.sigmoid(gate))).astype(BF16))
    return _dot(jnp.concatenate(parts, axis=-1), wmix_ref[...])


def _ffn_kernel(*refs, mixer, final_norm, sub_rows, ff_chunks):
    refs = list(refs)
    x_ref = refs.pop(0)
    mix_refs = [refs.pop(0) for _ in range(7)] if mixer else None
    gain_ref, win_ref, wout_ref = refs.pop(0), refs.pop(0), refs.pop(0)
    fgain_ref = refs.pop(0) if final_norm else None
    out_ref, acc_scr = refs
    d_ff = wout_ref.shape[0]

    for sub in range(x_ref.shape[0] // sub_rows):
        rows = slice(sub * sub_rows, (sub + 1) * sub_rows)
        x = x_ref[rows, :]
        res_ref = x_ref
        if mixer:
            x = x + _mixer_out_rows(rows, *mix_refs)
            out_ref[rows, :] = x
            res_ref = out_ref
        h = _rms(x, gain_ref[...]).astype(BF16)
        c0 = 0
        for width in ff_chunks:
            gate = _dot(h, win_ref[:, c0:c0 + width])
            up = _dot(h, win_ref[:, d_ff + c0:d_ff + c0 + width])
            act = ((gate * jax.nn.sigmoid(gate)) * up).astype(BF16)
            contrib = _dot(act, wout_ref[c0:c0 + width, :])
            if c0 == 0:
                acc_scr[sub] = contrib
            else:
                acc_scr[sub] += contrib
            c0 += width
        y = res_ref[rows, :] + 0.5 * acc_scr[sub]
        if final_norm:
            y = _rms(y, fgain_ref[...])
        out_ref[rows, :] = y


def _ffn_chunks(d_ff):
    n_full, rest = divmod(d_ff, FFN_FF_CHUNK)
    assert rest % V7X_LANES == 0
    return (FFN_FF_CHUNK,) * n_full + ((rest,) if rest else ())


def _ffn(x2d, gain, w_in, w_out, *, tokens, sub_rows, mixer_args=None, final_gain=None):
    n_tok, d = x2d.shape
    d_ff = w_out.shape[0]
    tok = lambda i: (i, 0)
    const = lambda i: (0, 0)
    resident = lambda shape: pl.BlockSpec(shape, const, pipeline_mode=pl.Buffered(1))
    in_specs = [pl.BlockSpec((tokens, d), tok)]
    args = [x2d]
    if mixer_args is not None:
        attn, o_f, o_b, rg, attn_gain, rec_gain, w_mix = mixer_args
        in_specs += [pl.BlockSpec((tokens, ATTN_WIDTH), tok)] + [pl.BlockSpec((tokens, REC_WIDTH), tok)] * 3
        in_specs += [pl.BlockSpec((1, ATTN_WIDTH), const), pl.BlockSpec((1, REC_HEAD_DIM), const),
                     resident(w_mix.shape)]
        args += [attn, o_f, o_b, rg, attn_gain.reshape(1, ATTN_WIDTH), rec_gain.reshape(1, REC_HEAD_DIM), w_mix]
    in_specs += [pl.BlockSpec((1, d), const), resident(w_in.shape), resident(w_out.shape)]
    args += [gain.reshape(1, d), w_in, w_out]
    if final_gain is not None:
        in_specs.append(pl.BlockSpec((1, d), const))
        args.append(final_gain.reshape(1, d))
    body = functools.partial(
        _ffn_kernel, mixer=mixer_args is not None, final_norm=final_gain is not None,
        sub_rows=sub_rows, ff_chunks=_ffn_chunks(d_ff))
    return pl.pallas_call(
        body,
        grid=(n_tok // tokens,),
        in_specs=in_specs,
        out_specs=pl.BlockSpec((tokens, d), tok),
        out_shape=jax.ShapeDtypeStruct((n_tok, d), F32),
        scratch_shapes=[pltpu.VMEM((tokens // sub_rows, sub_rows, d), F32)],
        compiler_params=pltpu.CompilerParams(
            dimension_semantics=("parallel",), vmem_limit_bytes=VMEM_LIMIT),
        name="ffn_mix" if mixer_args is not None else "ffn",
    )(*args)


def _rope(x, cos, sin_signed, first_half):
    swapped = jnp.where(first_half, pltpu.roll(x, 96, 1), pltpu.roll(x, 32, 1))
    return x * cos + swapped * sin_signed


def _mix_in_kernel(x_ref, gain_ref, w_ref, qg_ref, kg_ref, cos_ref, sin_ref, lbl_ref, csel_ref,
                   q_ref, k_ref, v_ref, rq_ref, gf_ref, gb_ref, ri_ref, rg_ref, ctot_ref, *, layer):
    h = _rms(x_ref[...], gain_ref[...]).astype(BF16)
    cos = cos_ref[...]
    sin = sin_ref[...]
    lane = lax.broadcasted_iota(jnp.int32, cos.shape, 1)
    first_half = (lane % (ROPE_AXIS_DIM)) < (ROPE_AXIS_DIM // 2)

    def proj(c0, width):
        return _dot(h, w_ref[:, c0:c0 + width])

    aq = proj(0, ATTN_WIDTH)
    for hh in range(N_Q_HEADS):
        sl = slice(hh * HEAD_DIM, (hh + 1) * HEAD_DIM)
        qh = _rope(_rms(aq[:, sl], qg_ref[...]), cos, sin, first_half) * (HEAD_DIM ** -0.5 * LOG2_E)
        q_ref[:, sl] = qh.astype(BF16)
    c = ATTN_WIDTH
    ak = proj(c, KV_WIDTH)
    for hh in range(N_KV_HEADS):
        sl = slice(hh * HEAD_DIM, (hh + 1) * HEAD_DIM)
        k_ref[:, sl] = _rope(_rms(ak[:, sl], kg_ref[...]), cos, sin, first_half).astype(BF16)
    c += KV_WIDTH
    av = proj(c, KV_WIDTH).astype(BF16)
    for hh in range(N_KV_HEADS):
        v_ref[:, 2 * hh * HEAD_DIM:(2 * hh + 1) * HEAD_DIM] = av[:, hh * HEAD_DIM:(hh + 1) * HEAD_DIM]
        v_ref[:, (2 * hh + 1) * HEAD_DIM:(2 * hh + 2) * HEAD_DIM] = jnp.ones((av.shape[0], HEAD_DIM), BF16)
    c += KV_WIDTH
    rq_ref[...] = proj(c, REC_WIDTH)
    c += REC_WIDTH
    chunk_lane = lax.broadcasted_iota(jnp.int32, ctot_ref.shape, 1)
    ctot = jnp.zeros(ctot_ref.shape, F32)
    for d, g_ref in enumerate((gf_ref, gb_ref)):
        logits = lbl_ref[d]
        e = jnp.exp(logits - jnp.max(logits, axis=0, keepdims=True))
        lb = jnp.sum(e[:layer + 1], axis=0, keepdims=True) / jnp.sum(e, axis=0, keepdims=True)
        g = jnp.log(lb + (1.0 - lb) * jax.nn.sigmoid(proj(c, REC_WIDTH)))
        c += REC_WIDTH
        g_ref[...] = g
        chunk_sum = _dot(csel_ref[...], g.astype(BF16))
        ctot = jnp.where(chunk_lane == d, jnp.max(-chunk_sum, axis=-1, keepdims=True), ctot)
    ctot_ref[...] = ctot
    for out_ref in (ri_ref, rg_ref):
        out_ref[...] = proj(c, REC_WIDTH)
        c += REC_WIDTH


def _rope_tables(seq_len):
    rows = seq_len // GRID_W
    inv_freq = ROPE_THETA ** (-jnp.arange(0, ROPE_AXIS_DIM, 2, dtype=F32) / ROPE_AXIS_DIM)
    ang_r = jnp.arange(rows, dtype=F32)[:, None] * inv_freq[None, :]
    ang_c = jnp.arange(GRID_W, dtype=F32)[:, None] * inv_freq[None, :]
    half = ROPE_AXIS_DIM // 2

    def expand(row_part, col_part):
        r = jnp.broadcast_to(row_part[:, None, :], (rows, GRID_W, 2 * half))
        c = jnp.broadcast_to(col_part[None, :, :], (rows, GRID_W, 2 * half))
        return jnp.concatenate([r, c], axis=-1).reshape(seq_len, 4 * half)

    cos = expand(jnp.concatenate([jnp.cos(ang_r)] * 2, -1), jnp.concatenate([jnp.cos(ang_c)] * 2, -1))
    sin = expand(jnp.concatenate([-jnp.sin(ang_r), jnp.sin(ang_r)], -1),
                 jnp.concatenate([-jnp.sin(ang_c), jnp.sin(ang_c)], -1))
    return cos, sin


def _mix_in(x2d, gain, w_in, q_gain, k_gain, lb_logits, layer, seq_len):
    n_tok, d = x2d.shape
    d_in = w_in.shape[1]
    tm = MIX_IN_TOKENS
    cos, sin = _rope_tables(seq_len)
    chunks = tm // REC_CHUNK
    chunk_sel = jnp.asarray(np.repeat(np.eye(chunks, dtype=np.float32), REC_CHUNK, axis=1), BF16)
    tiles_per_seq = seq_len // tm
    tok = lambda i: (i, 0)
    const = lambda i: (0, 0)
    table = lambda i: (i % tiles_per_seq, 0)
    out_shape = [
        jax.ShapeDtypeStruct((n_tok, ATTN_WIDTH), BF16),
        jax.ShapeDtypeStruct((n_tok, KV_WIDTH), BF16),
        jax.ShapeDtypeStruct((n_tok, 2 * KV_WIDTH), BF16),
    ] + [jax.ShapeDtypeStruct((n_tok, REC_WIDTH), F32)] * 5 + [
        jax.ShapeDtypeStruct((n_tok // REC_CHUNK, V7X_LANES), F32)]
    out_specs = [
        pl.BlockSpec((tm, ATTN_WIDTH), tok),
        pl.BlockSpec((tm, KV_WIDTH), tok),
        pl.BlockSpec((tm, 2 * KV_WIDTH), tok),
    ] + [pl.BlockSpec((tm, REC_WIDTH), tok)] * 5 + [pl.BlockSpec((chunks, V7X_LANES), tok)]
    return pl.pallas_call(
        functools.partial(_mix_in_kernel, layer=layer),
        grid=(n_tok // tm,),
        in_specs=[
            pl.BlockSpec((tm, d), tok),
            pl.BlockSpec((1, d), const),
            pl.BlockSpec((d, d_in), const),
            pl.BlockSpec((1, HEAD_DIM), const),
            pl.BlockSpec((1, HEAD_DIM), const),
            pl.BlockSpec((tm, HEAD_DIM), table),
            pl.BlockSpec((tm, HEAD_DIM), table),
            pl.BlockSpec(lb_logits.shape, lambda i: (0, 0, 0)),
            pl.BlockSpec(chunk_sel.shape, const),
        ],
        out_specs=out_specs,
        out_shape=out_shape,
        compiler_params=pltpu.CompilerParams(
            dimension_semantics=("parallel",), vmem_limit_bytes=VMEM_LIMIT),
        name="mix_in",
    )(x2d, gain.reshape(1, d), w_in, q_gain.reshape(1, HEAD_DIM), k_gain.reshape(1, HEAD_DIM), cos, sin,
      lb_logits, chunk_sel)


def _attn_kernel(q_ref, k_ref, v_ref, o_ref, q2_scr, s_scr, p_scr, mt_scr, m_scr, acc_scr):
    tq = q_ref.shape[1]
    rows = KV_GROUPS * tq
    tk = s_scr.shape[2]
    n_kv = k_ref.shape[1] // tk

    for g in range(KV_GROUPS):
        q2_scr[g * tq:(g + 1) * tq, :] = q_ref[0, :, g * HEAD_DIM:(g + 1) * HEAD_DIM]
    m_scr[...] = jnp.full_like(m_scr, -jnp.inf)
    acc_scr[...] = jnp.zeros_like(acc_scr)

    def scores(j):
        s = _dot_nt(q2_scr[...], k_ref[0, j * tk:(j + 1) * tk, :])
        s_scr[j % 2] = s
        mt_scr[j % 2] = jnp.broadcast_to(jnp.max(s, axis=-1, keepdims=True), (rows, V7X_LANES))

    def accumulate(j):
        slot = j % 2
        m_prev = m_scr[...]
        m_new = jnp.maximum(m_prev, mt_scr[slot])
        m_scr[...] = m_new
        for r in range(0, rows, ATTN_ROW_CHUNK):
            m_r = m_new[r:r + ATTN_ROW_CHUNK]
            for c in range(0, tk, V7X_LANES):
                s_piece = s_scr[slot, r:r + ATTN_ROW_CHUNK, c:c + V7X_LANES]
                p_scr[slot, r:r + ATTN_ROW_CHUNK, c:c + V7X_LANES] = jnp.exp2(s_piece - m_r).astype(BF16)
        alpha = jnp.exp2(m_prev - m_new)
        pv = _dot(p_scr[slot], v_ref[0, j * tk:(j + 1) * tk, :])
        for half in range(2):
            sl = slice(half * HEAD_DIM, (half + 1) * HEAD_DIM)
            acc_scr[:, sl] = alpha * acc_scr[:, sl] + pv[:, sl]

    scores(0)
    for j in range(n_kv):
        if j + 1 < n_kv:
            scores(j + 1)
        accumulate(j)

    o = acc_scr[:, 0:HEAD_DIM] / acc_scr[:, HEAD_DIM:2 * HEAD_DIM]
    for g in range(KV_GROUPS):
        o_ref[0, :, g * HEAD_DIM:(g + 1) * HEAD_DIM] = o[g * tq:(g + 1) * tq]


def _attention(q, k, v_ext):
    bsz, seq_len, _ = q.shape
    tq, tk = ATTN_Q_TOKENS, ATTN_KV_TOKENS
    gw = KV_GROUPS * HEAD_DIM
    rows = KV_GROUPS * tq
    return pl.pallas_call(
        _attn_kernel,
        grid=(bsz, N_KV_HEADS, seq_len // tq),
        in_specs=[
            pl.BlockSpec((1, tq, gw), lambda b, h, qi: (b, qi, h)),
            pl.BlockSpec((1, seq_len, HEAD_DIM), lambda b, h, qi: (b, 0, h)),
            pl.BlockSpec((1, seq_len, 2 * HEAD_DIM), lambda b, h, qi: (b, 0, h)),
        ],
        out_specs=pl.BlockSpec((1, tq, gw), lambda b, h, qi: (b, qi, h)),
        out_shape=jax.ShapeDtypeStruct((bsz, seq_len, ATTN_WIDTH), F32),
        scratch_shapes=[
            pltpu.VMEM((rows, HEAD_DIM), BF16),
            pltpu.VMEM((2, rows, tk), F32),
            pltpu.VMEM((2, rows, tk), BF16),
            pltpu.VMEM((2, rows, V7X_LANES), F32),
            pltpu.VMEM((rows, V7X_LANES), F32),
            pltpu.VMEM((rows, 2 * HEAD_DIM), F32),
        ],
        compiler_params=pltpu.CompilerParams(
            dimension_semantics=("parallel", "parallel", "arbitrary"),
            vmem_limit_bytes=VMEM_LIMIT),
        name="attn",
    )(q, k, v_ext)


def _rec_constants():
    c = REC_CHUNK
    t = np.arange(c)[:, None]
    u = np.arange(c)[None, :]
    blocks = [(u <= t)]
    masks = []
    for lvl in range(REC_LEVELS):
        h = 1 << lvl
        base_t = t - t % (2 * h)
        mid = base_t + h
        upper = t >= mid
        blocks.append(np.where(upper, (u >= mid) & (u <= t), (u > t) & (u < mid)))
        masks.append((t // (2 * h) == u // (2 * h)) & (t % (2 * h) >= h) & (u % (2 * h) < h))
    blocks.append(u > t)
    masks.append(t == u)
    fwd = np.concatenate([b.astype(np.float32) for b in blocks], axis=0)
    fwd_masks = np.stack([m.astype(np.float32) for m in masks])
    n_rows = fwd.shape[0]
    bwd = fwd.reshape(-1, c, c)[:, ::-1, ::-1].reshape(n_rows, c)
    bwd_masks = fwd_masks[:, ::-1, ::-1]
    total = np.ones((8, c), np.float32)
    mats = np.stack([np.concatenate([fwd, total]), np.concatenate([bwd, total])])
    mats = np.concatenate([mats, mats, mats], axis=-1)
    return mats, np.stack([fwd_masks, bwd_masks])


def _split3(x):
    hi = x.astype(BF16)
    r1 = x - hi.astype(F32)
    mid = r1.astype(BF16)
    lo = (r1 - mid.astype(F32)).astype(BF16)
    return jnp.concatenate([hi, mid, lo], axis=0)


def _rec_robust(d, q_ref, v_ref, g_ref, o_ref, mat_ref, mask_ref, state_scr):
    c = REC_CHUNK
    g = g_ref[0]
    kk = 1.0 - jnp.exp(g)
    sums = _dot(mat_ref[d], _split3(g))
    q = q_ref[0]
    v = v_ref[0].astype(BF16)
    for hh in range(N_REC_HEADS):
        sl = slice(hh * REC_HEAD_DIM, (hh + 1) * REC_HEAD_DIM)
        qh = q[:, sl]
        kh = kk[:, sl]
        vh = v[:, sl]
        scores = mask_ref[d, REC_LEVELS] * _dot_nt(qh.astype(BF16), kh.astype(BF16))
        for lvl in range(REC_LEVELS):
            el = jnp.exp(sums[(lvl + 1) * c:(lvl + 2) * c, sl])
            scores += mask_ref[d, lvl] * _dot_nt((qh * el).astype(BF16), (kh * el).astype(BF16))
        cum = sums[0:c, sl]
        rem = sums[(REC_LEVELS + 1) * c:(REC_LEVELS + 2) * c, sl]
        tot = sums[(REC_LEVELS + 2) * c:(REC_LEVELS + 2) * c + 1, sl]
        state_t = state_scr[d * N_REC_HEADS + hh]
        inter = _dot_nt((qh * jnp.exp(cum)).astype(BF16), state_t.astype(BF16))
        o_ref[0, :, sl] = inter + _dot(scores.astype(BF16), vh)
        k_tail = (kh * jnp.exp(rem)).astype(BF16)
        state_scr[d * N_REC_HEADS + hh] = state_t * jnp.exp(tot) + _dot_tn(vh, k_tail)


def _rec_fast(dirs, cmat_ref, tri_ref, state_scr):
    c = REC_CHUNK
    heads = [(d, hh) for d in range(len(dirs)) for hh in range(N_REC_HEADS)]
    lanes = lambda hh: slice(hh * REC_HEAD_DIM, (hh + 1) * REC_HEAD_DIM)
    prep = []
    for d, (q_ref, v_ref, g_ref, _) in enumerate(dirs):
        g = g_ref[0]
        kk = 1.0 - jnp.exp(g)
        cs = _dot(cmat_ref[d], _split3(g))
        cum = cs[0:c]
        tot = cs[c:c + 1]
        half = 0.5 * tot
        prep.append(dict(
            q_s=(q_ref[0] * jnp.exp(cum - half)).astype(BF16),
            k_s=(kk * jnp.exp(half - cum)).astype(BF16),
            k_tail=(kk * jnp.exp(tot - cum)).astype(BF16),
            e_half=jnp.exp(half), e_tot=jnp.exp(tot),
            v=v_ref[0].astype(BF16), causal=tri_ref[d] > 0.0))
    old_state = {(d, hh): state_scr[d * N_REC_HEADS + hh] for d, hh in heads}
    res = {}
    for d, hh in heads:
        p, sl = prep[d], lanes(hh)
        rhs = jnp.concatenate([(old_state[d, hh] * p["e_half"][:, sl]).astype(BF16), p["k_s"][:, sl]], axis=0)
        res[d, hh] = _dot_nt(p["q_s"][:, sl], rhs)
    for d, hh in heads:
        p, sl = prep[d], lanes(hh)
        update = _dot_tn(p["v"][:, sl], p["k_tail"][:, sl])
        state_scr[d * N_REC_HEADS + hh] = old_state[d, hh] * p["e_tot"][:, sl] + update
    for d, hh in heads:
        p, sl = prep[d], lanes(hh)
        scores = jnp.where(p["causal"], res[d, hh][:, REC_HEAD_DIM:], 0.0)
        dirs[d][3][0, :, sl] = res[d, hh][:, :REC_HEAD_DIM] + _dot(scores.astype(BF16), p["v"][:, sl])


def _rec_kernel(flag_ref, qf_ref, qb_ref, vf_ref, vb_ref, gf_ref, gb_ref, mat_ref, mask_ref, cmat_ref, tri_ref,
                of_ref, ob_ref, state_scr):
    b = pl.program_id(0)
    i = pl.program_id(1)
    nc = pl.num_programs(1)

    @pl.when(i == 0)
    def _():
        state_scr[...] = jnp.zeros_like(state_scr)

    fast = jnp.logical_and(flag_ref[(b * nc + i) * 2] == 1, flag_ref[(b * nc + nc - 1 - i) * 2 + 1] == 1)
    dirs = ((qf_ref, vf_ref, gf_ref, of_ref), (qb_ref, vb_ref, gb_ref, ob_ref))

    @pl.when(fast)
    def _():
        _rec_fast(dirs, cmat_ref, tri_ref, state_scr)

    @pl.when(jnp.logical_not(fast))
    def _():
        for d, (q_ref, v_ref, g_ref, o_ref) in enumerate(dirs):
            _rec_robust(d, q_ref, v_ref, g_ref, o_ref, mat_ref, mask_ref, state_scr)


def _hgrn2(rq, g_f, g_b, ri, fast_flags):
    bsz, seq_len, width = rq.shape
    c = REC_CHUNK
    nc = seq_len // c
    mats, masks = _rec_constants()
    n_lvl_rows = (REC_LEVELS + 2) * c
    cmat = np.concatenate([mats[:, 0:c], mats[:, n_lvl_rows:]], axis=1)
    tri = masks.sum(axis=1)
    mats, cmat = jnp.asarray(mats, BF16), jnp.asarray(cmat, BF16)
    masks, tri = jnp.asarray(masks, F32), jnp.asarray(tri, F32)
    fwd = lambda b, i, flags: (b, i, 0)
    bwd = lambda b, i, flags: (b, nc - 1 - i, 0)
    blk = (1, c, width)
    full3 = lambda b, i, flags: (0, 0, 0)
    full4 = lambda b, i, flags: (0, 0, 0, 0)
    grid_spec = pltpu.PrefetchScalarGridSpec(
        num_scalar_prefetch=1,
        grid=(bsz, nc),
        in_specs=[
            pl.BlockSpec(blk, fwd), pl.BlockSpec(blk, bwd),
            pl.BlockSpec(blk, fwd), pl.BlockSpec(blk, bwd),
            pl.BlockSpec(blk, fwd), pl.BlockSpec(blk, bwd),
            pl.BlockSpec(mats.shape, full3),
            pl.BlockSpec(masks.shape, full4),
            pl.BlockSpec(cmat.shape, full3),
            pl.BlockSpec(tri.shape, full3),
        ],
        out_specs=[pl.BlockSpec(blk, fwd), pl.BlockSpec(blk, bwd)],
        scratch_shapes=[pltpu.VMEM((2 * N_REC_HEADS, REC_HEAD_DIM, REC_HEAD_DIM), F32)],
    )
    return pl.pallas_call(
        _rec_kernel,
        grid_spec=grid_spec,
        out_shape=[jax.ShapeDtypeStruct((bsz, seq_len, width), F32)] * 2,
        compiler_params=pltpu.CompilerParams(
            dimension_semantics=("parallel", "arbitrary"), vmem_limit_bytes=VMEM_LIMIT),
        name="hgrn2",
    )(fast_flags, rq, rq, ri, ri, g_f, g_b, mats, masks, cmat, tri)


def kernel(x, ffn1_norm, ffn1_w_in, ffn1_w_out, mix_norm, w_in_mix, attn_q_norm, attn_k_norm,
           attn_out_norm, rec_lb_logits, rec_out_norm, w_out_mix, ffn2_norm, ffn2_w_in,
           ffn2_w_out, final_norm):
    bsz, seq_len, d = x.shape
    depth = ffn1_norm.shape[0]
    n_tok = bsz * seq_len
    h = x.reshape(n_tok, d)
    to3 = lambda a: a.reshape(bsz, seq_len, a.shape[-1])
    to2 = lambda a: a.reshape(n_tok, a.shape[-1])
    for l in range(depth):
        h = _ffn(h, ffn1_norm[l], ffn1_w_in[l].astype(BF16), ffn1_w_out[l].astype(BF16),
                 tokens=FFN_TOKENS, sub_rows=FFN_SUB_ROWS)
        q, k, v_ext, rq, g_f, g_b, ri, rg, chunk_decay = _mix_in(
            h, mix_norm[l], w_in_mix[l].astype(BF16), attn_q_norm[l], attn_k_norm[l],
            rec_lb_logits, l, seq_len)
        attn = _attention(to3(q), to3(k), to3(v_ext))
        fast_flags = (chunk_decay[:, :2] <= REC_FAST_RANGE).astype(jnp.int32).reshape(-1)
        o_f, o_b = _hgrn2(to3(rq), to3(g_f), to3(g_b), to3(ri), fast_flags)
        mixer_args = (to2(attn), to2(o_f), to2(o_b), rg, attn_out_norm[l], rec_out_norm[l],
                      w_out_mix[l].astype(BF16))
        h = _ffn(h, ffn2_norm[l], ffn2_w_in[l].astype(BF16), ffn2_w_out[l].astype(BF16),
                 tokens=FFN_MIX_TOKENS, sub_rows=FFN_MIX_SUB_ROWS, mixer_args=mixer_args,
                 final_gain=final_norm[l])
    return h.reshape(bsz, seq_len, d)
```

```python
import functools

import jax
import jax.numpy as jnp
import numpy as np
from jax import lax
from jax.experimental import pallas as pl
from jax.experimental.pallas import tpu as pltpu

F32 = jnp.float32
BF16 = jnp.bfloat16

EPS = 1e-6
GRID_W = 64
HEAD_DIM = 128
N_Q_HEADS = 4
N_KV_HEADS = 2
KV_GROUPS = N_Q_HEADS // N_KV_HEADS
ATTN_WIDTH = N_Q_HEADS * HEAD_DIM
KV_WIDTH = N_KV_HEADS * HEAD_DIM
ROPE_THETA = 10000.0
ROPE_AXIS_DIM = HEAD_DIM // 2
REC_HEAD_DIM = 128
N_REC_HEADS = 4
REC_WIDTH = N_REC_HEADS * REC_HEAD_DIM

V7X_LANES = 128

FFN_TOKENS = 1024
FFN_SUB_ROWS = 512
FFN_MIX_TOKENS = 1024
FFN_MIX_SUB_ROWS = 512
FFN_FF_CHUNK = 512
MIX_IN_TOKENS = 512
MIX_IN_SUB_ROWS = 256
ATTN_Q_TOKENS = 512
ATTN_KV_TOKENS = 1024
ATTN_ROW_CHUNK = 128
LOG2_E = 1.4426950408889634
REC_CHUNK = 64
REC_STEP_CHUNKS = 4
REC_LEVELS = 6
REC_FAST_RANGE = 100.0

VMEM_LIMIT = 56 * 1024 * 1024


def _rms(x, gain):
    return x * lax.rsqrt(jnp.mean(x * x, axis=-1, keepdims=True) + EPS) * gain


def _dot(a, b):
    return jnp.dot(a, b, preferred_element_type=F32)


def _dot_nt(a, b):
    return lax.dot_general(a, b, (((1,), (1,)), ((), ())), preferred_element_type=F32)


def _dot_tn(a, b):
    return lax.dot_general(a, b, (((0,), (0,)), ((), ())), preferred_element_type=F32)


def _mixer_out_rows(rows, attn_ref, of_ref, ob_ref, rg_ref, ag_ref, rgain_ref, wmix_ref):
    parts = [_rms(attn_ref[rows, :].astype(F32), ag_ref[...]).astype(BF16)]
    for hh in range(N_REC_HEADS):
        sl = slice(hh * REC_HEAD_DIM, (hh + 1) * REC_HEAD_DIM)
        gate = rg_ref[rows, sl].astype(F32)
        o = of_ref[rows, sl].astype(F32) + ob_ref[rows, sl].astype(F32)
        parts.append((_rms(o, rgain_ref[...]) * (gate * jax.nn.sigmoid(gate))).astype(BF16))
    return _dot(jnp.concatenate(parts, axis=-1), wmix_ref[...])


def _ffn_kernel(*refs, mixer, final_norm, sub_rows, ff_chunks):
    refs = list(refs)
    x_ref = refs.pop(0)
    mix_refs = [refs.pop(0) for _ in range(7)] if mixer else None
    gain_ref, win_ref, wout_ref = refs.pop(0), refs.pop(0), refs.pop(0)
    fgain_ref = refs.pop(0) if final_norm else None
    out_ref, acc_scr = refs
    d_ff = wout_ref.shape[0]

    for sub in range(x_ref.shape[0] // sub_rows):
        rows = slice(sub * sub_rows, (sub + 1) * sub_rows)
        x = x_ref[rows, :]
        res_ref = x_ref
        if mixer:
            x = x + _mixer_out_rows(rows, *mix_refs)
            out_ref[rows, :] = x
            res_ref = out_ref
        h = _rms(x, gain_ref[...]).astype(BF16)
        c0 = 0
        for width in ff_chunks:
            gate = _dot(h, win_ref[:, c0:c0 + width])
            up = _dot(h, win_ref[:, d_ff + c0:d_ff + c0 + width])
            act = ((gate * jax.nn.sigmoid(gate)) * up).astype(BF16)
            contrib = _dot(act, wout_ref[c0:c0 + width, :])
            if c0 == 0:
                acc_scr[sub] = contrib
            else:
                acc_scr[sub] += contrib
            c0 += width
        y = res_ref[rows, :] + 0.5 * acc_scr[sub]
        if final_norm:
            y = _rms(y, fgain_ref[...])
        out_ref[rows, :] = y


def _ffn_chunks(d_ff):
    n_full, rest = divmod(d_ff, FFN_FF_CHUNK)
    assert rest % V7X_LANES == 0
    return (FFN_FF_CHUNK,) * n_full + ((rest,) if rest else ())


def _ffn(x2d, gain, w_in, w_out, *, tokens, sub_rows, mixer_args=None, final_gain=None):
    n_tok, d = x2d.shape
    d_ff = w_out.shape[0]
    tok = lambda i: (i, 0)
    const = lambda i: (0, 0)
    resident = lambda shape: pl.BlockSpec(shape, const, pipeline_mode=pl.Buffered(1))
    in_specs = [pl.BlockSpec((tokens, d), tok)]
    args = [x2d]
    if mixer_args is not None:
        attn, o_f, o_b, rg, attn_gain, rec_gain, w_mix = mixer_args
        in_specs += [pl.BlockSpec((tokens, ATTN_WIDTH), tok)] + [pl.BlockSpec((tokens, REC_WIDTH), tok)] * 3
        in_specs += [pl.BlockSpec((1, ATTN_WIDTH), const), pl.BlockSpec((1, REC_HEAD_DIM), const),
                     resident(w_mix.shape)]
        args += [attn, o_f, o_b, rg, attn_gain.reshape(1, ATTN_WIDTH), rec_gain.reshape(1, REC_HEAD_DIM), w_mix]
    in_specs += [pl.BlockSpec((1, d), const), resident(w_in.shape), resident(w_out.shape)]
    args += [gain.reshape(1, d), w_in, w_out]
    if final_gain is not None:
        in_specs.append(pl.BlockSpec((1, d), const))
        args.append(final_gain.reshape(1, d))
    body = functools.partial(
        _ffn_kernel, mixer=mixer_args is not None, final_norm=final_gain is not None,
        sub_rows=sub_rows, ff_chunks=_ffn_chunks(d_ff))
    return pl.pallas_call(
        body,
        grid=(n_tok // tokens,),
        in_specs=in_specs,
        out_specs=pl.BlockSpec((tokens, d), tok),
        out_shape=jax.ShapeDtypeStruct((n_tok, d), F32),
        scratch_shapes=[pltpu.VMEM((tokens // sub_rows, sub_rows, d), F32)],
        compiler_params=pltpu.CompilerParams(
            dimension_semantics=("parallel",), vmem_limit_bytes=VMEM_LIMIT),
        name="ffn_mix" if mixer_args is not None else "ffn",
    )(*args)


def _rope(x, cos, sin_signed, first_half):
    swapped = jnp.where(first_half, pltpu.roll(x, 96, 1), pltpu.roll(x, 32, 1))
    return x * cos + swapped * sin_signed


def _mix_in_kernel(x_ref, gain_ref, w_ref, qg_ref, kg_ref, cos_ref, sin_ref, lbl_ref, csel_ref,
                   q_ref, k_ref, v_ref, rq_ref, gf_ref, gb_ref, ri_ref, rg_ref, ctot_ref, *, layer, sub_rows):
    col = {}
    c = 0
    for name, width in (("aq", ATTN_WIDTH), ("ak", KV_WIDTH), ("av", KV_WIDTH), ("rq", REC_WIDTH),
                        ("zf", REC_WIDTH), ("zb", REC_WIDTH), ("ri", REC_WIDTH), ("rg", REC_WIDTH)):
        col[name] = (c, width)
        c += width
    lane = lax.broadcasted_iota(jnp.int32, (sub_rows, HEAD_DIM), 1)
    first_half = (lane % (ROPE_AXIS_DIM)) < (ROPE_AXIS_DIM // 2)
    chunk_lane = lax.broadcasted_iota(jnp.int32, ctot_ref.shape, 1)
    ctot = jnp.zeros(ctot_ref.shape, F32)
    gates = []

    for sub in range(x_ref.shape[0] // sub_rows):
        rows = slice(sub * sub_rows, (sub + 1) * sub_rows)
        h = _rms(x_ref[rows, :], gain_ref[...]).astype(BF16)
        cos = cos_ref[rows, :]
        sin = sin_ref[rows, :]

        def proj(name):
            c0, width = col[name]
            return _dot(h, w_ref[:, c0:c0 + width])

        for d, (name, g_ref) in enumerate((("zf", gf_ref), ("zb", gb_ref))):
            logits = lbl_ref[d]
            e = jnp.exp(logits - jnp.max(logits, axis=0, keepdims=True))
            lb = jnp.sum(e[:layer + 1], axis=0, keepdims=True) / jnp.sum(e, axis=0, keepdims=True)
            g = jnp.log(lb + (1.0 - lb) * jax.nn.sigmoid(proj(name)))
            g_ref[rows, :] = g
            gates.append((d, rows, g.astype(BF16)))

        aq = proj("aq")
        for hh in range(N_Q_HEADS):
            sl = slice(hh * HEAD_DIM, (hh + 1) * HEAD_DIM)
            qh = _rope(_rms(aq[:, sl], qg_ref[...]), cos, sin, first_half) * (HEAD_DIM ** -0.5 * LOG2_E)
            q_ref[rows, sl] = qh.astype(BF16)
        ak = proj("ak")
        for hh in range(N_KV_HEADS):
            sl = slice(hh * HEAD_DIM, (hh + 1) * HEAD_DIM)
            k_ref[rows, sl] = _rope(_rms(ak[:, sl], kg_ref[...]), cos, sin, first_half).astype(BF16)
        av = proj("av").astype(BF16)
        for hh in range(N_KV_HEADS):
            v_ref[rows, 2 * hh * HEAD_DIM:(2 * hh + 1) * HEAD_DIM] = av[:, hh * HEAD_DIM:(hh + 1) * HEAD_DIM]
            v_ref[rows, (2 * hh + 1) * HEAD_DIM:(2 * hh + 2) * HEAD_DIM] = jnp.ones((sub_rows, HEAD_DIM), BF16)
        rq_ref[rows, :] = proj("rq")
        ri_ref[rows, :] = proj("ri")
        rg_ref[rows, :] = proj("rg").astype(rg_ref.dtype)
    for d, rows, g_bf16 in gates:
        chunk_sum = _dot(csel_ref[:, rows], g_bf16)
        ctot = jnp.maximum(ctot, jnp.where(chunk_lane == d, jnp.max(-chunk_sum, axis=-1, keepdims=True), 0.0))
    ctot_ref[...] = ctot


def _rope_tables(seq_len):
    rows = seq_len // GRID_W
    inv_freq = ROPE_THETA ** (-jnp.arange(0, ROPE_AXIS_DIM, 2, dtype=F32) / ROPE_AXIS_DIM)
    ang_r = jnp.arange(rows, dtype=F32)[:, None] * inv_freq[None, :]
    ang_c = jnp.arange(GRID_W, dtype=F32)[:, None] * inv_freq[None, :]
    half = ROPE_AXIS_DIM // 2

    def expand(row_part, col_part):
        r = jnp.broadcast_to(row_part[:, None, :], (rows, GRID_W, 2 * half))
        c = jnp.broadcast_to(col_part[None, :, :], (rows, GRID_W, 2 * half))
        return jnp.concatenate([r, c], axis=-1).reshape(seq_len, 4 * half)

    cos = expand(jnp.concatenate([jnp.cos(ang_r)] * 2, -1), jnp.concatenate([jnp.cos(ang_c)] * 2, -1))
    sin = expand(jnp.concatenate([-jnp.sin(ang_r), jnp.sin(ang_r)], -1),
                 jnp.concatenate([-jnp.sin(ang_c), jnp.sin(ang_c)], -1))
    return cos, sin


def _mix_in(x2d, gain, w_in, q_gain, k_gain, lb_logits, layer, seq_len):
    n_tok, d = x2d.shape
    d_in = w_in.shape[1]
    tm = MIX_IN_TOKENS
    cos, sin = _rope_tables(seq_len)
    chunks = tm // REC_CHUNK
    chunk_sel = jnp.asarray(np.repeat(np.eye(chunks, dtype=np.float32), REC_CHUNK, axis=1), BF16)
    tiles_per_seq = seq_len // tm
    tok = lambda i: (i, 0)
    const = lambda i: (0, 0)
    table = lambda i: (i % tiles_per_seq, 0)
    out_shape = [
        jax.ShapeDtypeStruct((n_tok, ATTN_WIDTH), BF16),
        jax.ShapeDtypeStruct((n_tok, KV_WIDTH), BF16),
        jax.ShapeDtypeStruct((n_tok, 2 * KV_WIDTH), BF16),
    ] + [jax.ShapeDtypeStruct((n_tok, REC_WIDTH), F32)] * 4 + [
        jax.ShapeDtypeStruct((n_tok, REC_WIDTH), BF16),
        jax.ShapeDtypeStruct((n_tok // REC_CHUNK, V7X_LANES), F32)]
    out_specs = [
        pl.BlockSpec((tm, ATTN_WIDTH), tok),
        pl.BlockSpec((tm, KV_WIDTH), tok),
        pl.BlockSpec((tm, 2 * KV_WIDTH), tok),
    ] + [pl.BlockSpec((tm, REC_WIDTH), tok)] * 5 + [pl.BlockSpec((chunks, V7X_LANES), tok)]
    return pl.pallas_call(
        functools.partial(_mix_in_kernel, layer=layer, sub_rows=MIX_IN_SUB_ROWS),
        grid=(n_tok // tm,),
        in_specs=[
            pl.BlockSpec((tm, d), tok),
            pl.BlockSpec((1, d), const),
            pl.BlockSpec((d, d_in), const),
            pl.BlockSpec((1, HEAD_DIM), const),
            pl.BlockSpec((1, HEAD_DIM), const),
            pl.BlockSpec((tm, HEAD_DIM), table),
            pl.BlockSpec((tm, HEAD_DIM), table),
            pl.BlockSpec(lb_logits.shape, lambda i: (0, 0, 0)),
            pl.BlockSpec(chunk_sel.shape, const),
        ],
        out_specs=out_specs,
        out_shape=out_shape,
        compiler_params=pltpu.CompilerParams(
            dimension_semantics=("parallel",), vmem_limit_bytes=VMEM_LIMIT),
        name="mix_in",
    )(x2d, gain.reshape(1, d), w_in, q_gain.reshape(1, HEAD_DIM), k_gain.reshape(1, HEAD_DIM), cos, sin,
      lb_logits, chunk_sel)


def _attn_kernel(q_ref, k_ref, v_ref, o_ref, q2_scr, s_scr, p_scr, mt_scr, m_scr, acc_scr):
    tq = q_ref.shape[1]
    rows = KV_GROUPS * tq
    tk = s_scr.shape[2]
    n_kv = k_ref.shape[1] // tk

    for g in range(KV_GROUPS):
        q2_scr[g * tq:(g + 1) * tq, :] = q_ref[0, :, g * HEAD_DIM:(g + 1) * HEAD_DIM]
    m_scr[...] = jnp.full_like(m_scr, -jnp.inf)
    acc_scr[...] = jnp.zeros_like(acc_scr)

    def scores(j):
        s = _dot_nt(q2_scr[...], k_ref[0, j * tk:(j + 1) * tk, :])
        s_scr[j % 2] = s
        mt_scr[j % 2] = jnp.broadcast_to(jnp.max(s, axis=-1, keepdims=True), (rows, V7X_LANES))

    def accumulate(j):
        slot = j % 2
        m_prev = m_scr[...]
        m_new = jnp.maximum(m_prev, mt_scr[slot])
        m_scr[...] = m_new
        for r in range(0, rows, ATTN_ROW_CHUNK):
            m_r = m_new[r:r + ATTN_ROW_CHUNK]
            for c in range(0, tk, V7X_LANES):
                s_piece = s_scr[slot, r:r + ATTN_ROW_CHUNK, c:c + V7X_LANES]
                p_scr[slot, r:r + ATTN_ROW_CHUNK, c:c + V7X_LANES] = jnp.exp2(s_piece - m_r).astype(BF16)
        alpha = jnp.exp2(m_prev - m_new)
        pv = _dot(p_scr[slot], v_ref[0, j * tk:(j + 1) * tk, :])
        for half in range(2):
            sl = slice(half * HEAD_DIM, (half + 1) * HEAD_DIM)
            acc_scr[:, sl] = alpha * acc_scr[:, sl] + pv[:, sl]

    scores(0)
    for j in range(n_kv):
        if j + 1 < n_kv:
            scores(j + 1)
        accumulate(j)

    o = (acc_scr[:, 0:HEAD_DIM] / acc_scr[:, HEAD_DIM:2 * HEAD_DIM]).astype(o_ref.dtype)
    for g in range(KV_GROUPS):
        o_ref[0, :, g * HEAD_DIM:(g + 1) * HEAD_DIM] = o[g * tq:(g + 1) * tq]


def _attention(q, k, v_ext):
    bsz, seq_len, _ = q.shape
    tq, tk = ATTN_Q_TOKENS, ATTN_KV_TOKENS
    gw = KV_GROUPS * HEAD_DIM
    rows = KV_GROUPS * tq
    return pl.pallas_call(
        _attn_kernel,
        grid=(bsz, N_KV_HEADS, seq_len // tq),
        in_specs=[
            pl.BlockSpec((1, tq, gw), lambda b, h, qi: (b, qi, h)),
            pl.BlockSpec((1, seq_len, HEAD_DIM), lambda b, h, qi: (b, 0, h)),
            pl.BlockSpec((1, seq_len, 2 * HEAD_DIM), lambda b, h, qi: (b, 0, h)),
        ],
        out_specs=pl.BlockSpec((1, tq, gw), lambda b, h, qi: (b, qi, h)),
        out_shape=jax.ShapeDtypeStruct((bsz, seq_len, ATTN_WIDTH), BF16),
        scratch_shapes=[
            pltpu.VMEM((rows, HEAD_DIM), BF16),
            pltpu.VMEM((2, rows, tk), F32),
            pltpu.VMEM((2, rows, tk), BF16),
            pltpu.VMEM((2, rows, V7X_LANES), F32),
            pltpu.VMEM((rows, V7X_LANES), F32),
            pltpu.VMEM((rows, 2 * HEAD_DIM), F32),
        ],
        compiler_params=pltpu.CompilerParams(
            dimension_semantics=("parallel", "parallel", "arbitrary"),
            vmem_limit_bytes=VMEM_LIMIT),
        name="attn",
    )(q, k, v_ext)


def _rec_constants():
    c = REC_CHUNK
    t = np.arange(c)[:, None]
    u = np.arange(c)[None, :]
    blocks = [(u <= t)]
    masks = []
    for lvl in range(REC_LEVELS):
        h = 1 << lvl
        base_t = t - t % (2 * h)
        mid = base_t + h
        upper = t >= mid
        blocks.append(np.where(upper, (u >= mid) & (u <= t), (u > t) & (u < mid)))
        masks.append((t // (2 * h) == u // (2 * h)) & (t % (2 * h) >= h) & (u % (2 * h) < h))
    blocks.append(u > t)
    masks.append(t == u)
    fwd = np.concatenate([b.astype(np.float32) for b in blocks], axis=0)
    fwd_masks = np.stack([m.astype(np.float32) for m in masks])
    n_rows = fwd.shape[0]
    bwd = fwd.reshape(-1, c, c)[:, ::-1, ::-1].reshape(n_rows, c)
    bwd_masks = fwd_masks[:, ::-1, ::-1]
    total = np.ones((8, c), np.float32)
    mats = np.stack([np.concatenate([fwd, total]), np.concatenate([bwd, total])])
    mats = np.concatenate([mats, mats, mats], axis=-1)
    return mats, np.stack([fwd_masks, bwd_masks])


def _split3(x):
    hi = x.astype(BF16)
    r1 = x - hi.astype(F32)
    mid = r1.astype(BF16)
    lo = (r1 - mid.astype(F32)).astype(BF16)
    return jnp.concatenate([hi, mid, lo], axis=0)


def _rec_robust(d, rows, q_ref, v_ref, g_ref, o_ref, mat_ref, mask_ref, state_scr):
    c = REC_CHUNK
    g = g_ref[0, rows, :]
    kk = 1.0 - jnp.exp(g)
    sums = _dot(mat_ref[d], _split3(g))
    q = q_ref[0, rows, :]
    v = v_ref[0, rows, :].astype(BF16)
    for hh in range(N_REC_HEADS):
        sl = slice(hh * REC_HEAD_DIM, (hh + 1) * REC_HEAD_DIM)
        qh = q[:, sl]
        kh = kk[:, sl]
        vh = v[:, sl]
        scores = mask_ref[d, REC_LEVELS] * _dot_nt(qh.astype(BF16), kh.astype(BF16))
        for lvl in range(REC_LEVELS):
            el = jnp.exp(sums[(lvl + 1) * c:(lvl + 2) * c, sl])
            scores += mask_ref[d, lvl] * _dot_nt((qh * el).astype(BF16), (kh * el).astype(BF16))
        cum = sums[0:c, sl]
        rem = sums[(REC_LEVELS + 1) * c:(REC_LEVELS + 2) * c, sl]
        tot = sums[(REC_LEVELS + 2) * c:(REC_LEVELS + 2) * c + 1, sl]
        state_t = state_scr[d * N_REC_HEADS + hh]
        inter = _dot_nt((qh * jnp.exp(cum)).astype(BF16), state_t.astype(BF16))
        o_ref[0, rows, sl] = (inter + _dot(scores.astype(BF16), vh)).astype(o_ref.dtype)
        k_tail = (kh * jnp.exp(rem)).astype(BF16)
        state_scr[d * N_REC_HEADS + hh] = state_t * jnp.exp(tot) + _dot_tn(vh, k_tail)


def _rec_fast_prep(d, rows, q_ref, v_ref, g_ref, cmat_ref, tri_ref):
    c = REC_CHUNK
    g = g_ref[0, rows, :]
    kk = 1.0 - jnp.exp(g)
    cs = _dot(cmat_ref[d], _split3(g))
    cum = cs[0:c]
    tot = cs[c:c + 1]
    half = 0.5 * tot
    return dict(
        q_s=(q_ref[0, rows, :] * jnp.exp(cum - half)).astype(BF16),
        k_s=(kk * jnp.exp(half - cum)).astype(BF16),
        k_tail=(kk * jnp.exp(tot - cum)).astype(BF16),
        e_half=jnp.exp(half), e_tot=jnp.exp(tot),
        v=v_ref[0, rows, :].astype(BF16), causal=tri_ref[d] > 0.0)


def _rec_fast(prep, rows_of, o_refs, state_scr):
    heads = [(d, hh) for d in range(len(o_refs)) for hh in range(N_REC_HEADS)]
    lanes = lambda hh: slice(hh * REC_HEAD_DIM, (hh + 1) * REC_HEAD_DIM)
    updates = [{(d, hh): _dot_tn(pj[d]["v"][:, lanes(hh)], pj[d]["k_tail"][:, lanes(hh)]) for d, hh in heads}
               for pj in prep]
    state = {(d, hh): state_scr[d * N_REC_HEADS + hh] for d, hh in heads}
    res = []
    for pj, uj in zip(prep, updates):
        res_j = {}
        for d, hh in heads:
            p, sl = pj[d], lanes(hh)
            rhs = jnp.concatenate([(state[d, hh] * p["e_half"][:, sl]).astype(BF16), p["k_s"][:, sl]], axis=0)
            res_j[d, hh] = _dot_nt(p["q_s"][:, sl], rhs)
            state[d, hh] = state[d, hh] * p["e_tot"][:, sl] + uj[d, hh]
        res.append(res_j)
    for d, hh in heads:
        state_scr[d * N_REC_HEADS + hh] = state[d, hh]
    for pj, rows_j, res_j in zip(prep, rows_of, res):
        for d, hh in heads:
            p, sl = pj[d], lanes(hh)
            scores = jnp.where(p["causal"], res_j[d, hh][:, REC_HEAD_DIM:], 0.0)
            o = res_j[d, hh][:, :REC_HEAD_DIM] + _dot(scores.astype(BF16), p["v"][:, sl])
            o_refs[d][0, rows_j[d], sl] = o.astype(o_refs[d].dtype)


def _rec_kernel(flag_ref, qf_ref, qb_ref, vf_ref, vb_ref, gf_ref, gb_ref, mat_ref, mask_ref, cmat_ref, tri_ref,
                of_ref, ob_ref, state_scr):
    b = pl.program_id(0)
    i = pl.program_id(1)
    n_steps = pl.num_programs(1)
    per_step = REC_STEP_CHUNKS

    @pl.when(i == 0)
    def _():
        state_scr[...] = jnp.zeros_like(state_scr)

    fwd_chunk0 = (b * n_steps + i) * per_step
    bwd_chunk0 = (b * n_steps + n_steps - 1 - i) * per_step
    fast = flag_ref[fwd_chunk0 * 2] == 1
    for j in range(per_step):
        if j:
            fast = jnp.logical_and(fast, flag_ref[(fwd_chunk0 + j) * 2] == 1)
        fast = jnp.logical_and(fast, flag_ref[(bwd_chunk0 + j) * 2 + 1] == 1)
    dirs = ((qf_ref, vf_ref, gf_ref, of_ref), (qb_ref, vb_ref, gb_ref, ob_ref))
    chunk_rows = lambda j: slice(j * REC_CHUNK, (j + 1) * REC_CHUNK)
    rows_of = [(chunk_rows(j), chunk_rows(per_step - 1 - j)) for j in range(per_step)]

    @pl.when(fast)
    def _():
        prep = [[_rec_fast_prep(d, rows_of[j][d], q_ref, v_ref, g_ref, cmat_ref, tri_ref)
                 for d, (q_ref, v_ref, g_ref, _) in enumerate(dirs)] for j in range(per_step)]
        _rec_fast(prep, rows_of, (of_ref, ob_ref), state_scr)

    @pl.when(jnp.logical_not(fast))
    def _():
        for j in range(per_step):
            for d, (q_ref, v_ref, g_ref, o_ref) in enumerate(dirs):
                _rec_robust(d, rows_of[j][d], q_ref, v_ref, g_ref, o_ref, mat_ref, mask_ref, state_scr)


def _hgrn2(rq, g_f, g_b, ri, fast_flags):
    bsz, seq_len, width = rq.shape
    c = REC_CHUNK
    n_steps = seq_len // (c * REC_STEP_CHUNKS)
    mats, masks = _rec_constants()
    n_lvl_rows = (REC_LEVELS + 2) * c
    cmat = np.concatenate([mats[:, 0:c], mats[:, n_lvl_rows:]], axis=1)
    tri = masks.sum(axis=1)
    mats, cmat = jnp.asarray(mats, BF16), jnp.asarray(cmat, BF16)
    masks, tri = jnp.asarray(masks, F32), jnp.asarray(tri, F32)
    fwd = lambda b, i, flags: (b, i, 0)
    bwd = lambda b, i, flags: (b, n_steps - 1 - i, 0)
    blk = (1, c * REC_STEP_CHUNKS, width)
    full3 = lambda b, i, flags: (0, 0, 0)
    full4 = lambda b, i, flags: (0, 0, 0, 0)
    grid_spec = pltpu.PrefetchScalarGridSpec(
        num_scalar_prefetch=1,
        grid=(bsz, n_steps),
        in_specs=[
            pl.BlockSpec(blk, fwd), pl.BlockSpec(blk, bwd),
            pl.BlockSpec(blk, fwd), pl.BlockSpec(blk, bwd),
            pl.BlockSpec(blk, fwd), pl.BlockSpec(blk, bwd),
            pl.BlockSpec(mats.shape, full3),
            pl.BlockSpec(masks.shape, full4),
            pl.BlockSpec(cmat.shape, full3),
            pl.BlockSpec(tri.shape, full3),
        ],
        out_specs=[pl.BlockSpec(blk, fwd), pl.BlockSpec(blk, bwd)],
        scratch_shapes=[pltpu.VMEM((2 * N_REC_HEADS, REC_HEAD_DIM, REC_HEAD_DIM), F32)],
    )
    return pl.pallas_call(
        _rec_kernel,
        grid_spec=grid_spec,
        out_shape=[jax.ShapeDtypeStruct((bsz, seq_len, width), BF16)] * 2,
        compiler_params=pltpu.CompilerParams(
            dimension_semantics=("parallel", "arbitrary"), vmem_limit_bytes=VMEM_LIMIT),
        name="hgrn2",
    )(fast_flags, rq, rq, ri, ri, g_f, g_b, mats, masks, cmat, tri)


def kernel(x, ffn1_norm, ffn1_w_in, ffn1_w_out, mix_norm, w_in_mix, attn_q_norm, attn_k_norm,
           attn_out_norm, rec_lb_logits, rec_out_norm, w_out_mix, ffn2_norm, ffn2_w_in,
           ffn2_w_out, final_norm):
    bsz, seq_len, d = x.shape
    depth = ffn1_norm.shape[0]
    n_tok = bsz * seq_len
    h = x.reshape(n_tok, d)
    to3 = lambda a: a.reshape(bsz, seq_len, a.shape[-1])
    to2 = lambda a: a.reshape(n_tok, a.shape[-1])
    for l in range(depth):
        h = _ffn(h, ffn1_norm[l], ffn1_w_in[l].astype(BF16), ffn1_w_out[l].astype(BF16),
                 tokens=FFN_TOKENS, sub_rows=FFN_SUB_ROWS)
        q, k, v_ext, rq, g_f, g_b, ri, rg, chunk_decay = _mix_in(
            h, mix_norm[l], w_in_mix[l].astype(BF16), attn_q_norm[l], attn_k_norm[l],
            rec_lb_logits, l, seq_len)
        attn = _attention(to3(q), to3(k), to3(v_ext))
        fast_flags = (chunk_decay[:, :2] <= REC_FAST_RANGE).astype(jnp.int32).reshape(-1)
        o_f, o_b = _hgrn2(to3(rq), to3(g_f), to3(g_b), to3(ri), fast_flags)
        mixer_args = (to2(attn), to2(o_f), to2(o_b), rg, attn_out_norm[l], rec_out_norm[l],
                      w_out_mix[l].astype(BF16))
        h = _ffn(h, ffn2_norm[l], ffn2_w_in[l].astype(BF16), ffn2_w_out[l].astype(BF16),
                 tokens=FFN_MIX_TOKENS, sub_rows=FFN_MIX_SUB_ROWS, mixer_args=mixer_args,
                 final_gain=final_norm[l])
    return h.reshape(bsz, seq_len, d)
```

```python
import functools

import jax
import jax.numpy as jnp
import numpy as np
from jax import lax
from jax.experimental import pallas as pl
from jax.experimental.pallas import tpu as pltpu

F32 = jnp.float32
BF16 = jnp.bfloat16

EPS = 1e-6
GRID_W = 64
HEAD_DIM = 128
N_Q_HEADS = 4
N_KV_HEADS = 2
KV_GROUPS = N_Q_HEADS // N_KV_HEADS
ATTN_WIDTH = N_Q_HEADS * HEAD_DIM
KV_WIDTH = N_KV_HEADS * HEAD_DIM
ROPE_THETA = 10000.0
ROPE_AXIS_DIM = HEAD_DIM // 2
REC_HEAD_DIM = 128
N_REC_HEADS = 4
REC_WIDTH = N_REC_HEADS * REC_HEAD_DIM

V7X_LANES = 128

FFN_TOKENS = 1024
FFN_SUB_ROWS = 512
FFN_MIX_TOKENS = 1024
FFN_MIX_SUB_ROWS = 512
FFN_FF_CHUNK = 512
MIX_IN_TOKENS = 512
MIX_IN_SUB_ROWS = 256
ATTN_Q_TOKENS = 512
ATTN_KV_TOKENS = 1024
ATTN_ROW_CHUNK = 128
LOG2_E = 1.4426950408889634
REC_CHUNK = 64
REC_STEP_CHUNKS = 4
REC_LEVELS = 6
REC_FAST_RANGE = 100.0

VMEM_LIMIT = 56 * 1024 * 1024


def _rms(x, gain):
    return x * lax.rsqrt(jnp.mean(x * x, axis=-1, keepdims=True) + EPS) * gain


def _dot(a, b):
    return jnp.dot(a, b, preferred_element_type=F32)


def _dot_nt(a, b):
    return lax.dot_general(a, b, (((1,), (1,)), ((), ())), preferred_element_type=F32)


def _dot_tn(a, b):
    return lax.dot_general(a, b, (((0,), (0,)), ((), ())), preferred_element_type=F32)


def _mixer_out_rows(rows, attn_ref, of_ref, ob_ref, rg_ref, ag_ref, rgain_ref, wmix_ref):
    parts = [_rms(attn_ref[rows, :].astype(F32), ag_ref[...]).astype(BF16)]
    for hh in range(N_REC_HEADS):
        sl = slice(hh * REC_HEAD_DIM, (hh + 1) * REC_HEAD_DIM)
        gate = rg_ref[rows, sl].astype(F32)
        o = of_ref[rows, sl].astype(F32) + ob_ref[rows, sl].astype(F32)
        parts.append((_rms(o, rgain_ref[...]) * (gate * jax.nn.sigmoid(gate))).astype(BF16))
    return _dot(jnp.concatenate(parts, axis=-1), wmix_ref[...])


def _ffn_kernel(*refs, mixer, final_norm, sub_rows, ff_chunks):
    refs = list(refs)
    x_ref = refs.pop(0)
    mix_refs = [refs.pop(0) for _ in range(7)] if mixer else None
    gain_ref, win_ref, wout_ref = refs.pop(0), refs.pop(0), refs.pop(0)
    fgain_ref = refs.pop(0) if final_norm else None
    out_ref, acc_scr = refs
    d_ff = wout_ref.shape[0]

    for sub in range(x_ref.shape[0] // sub_rows):
        rows = slice(sub * sub_rows, (sub + 1) * sub_rows)
        x = x_ref[rows, :]
        res_ref = x_ref
        if mixer:
            x = x + _mixer_out_rows(rows, *mix_refs)
            out_ref[rows, :] = x
            res_ref = out_ref
        h = _rms(x, gain_ref[...]).astype(BF16)
        c0 = 0
        for width in ff_chunks:
            gate = _dot(h, win_ref[:, c0:c0 + width])
            up = _dot(h, win_ref[:, d_ff + c0:d_ff + c0 + width])
            act = ((gate * jax.nn.sigmoid(gate)) * up).astype(BF16)
            contrib = _dot(act, wout_ref[c0:c0 + width, :])
            if c0 == 0:
                acc_scr[sub] = contrib
            else:
                acc_scr[sub] += contrib
            c0 += width
        y = res_ref[rows, :] + 0.5 * acc_scr[sub]
        if final_norm:
            y = _rms(y, fgain_ref[...])
        out_ref[rows, :] = y


def _ffn_chunks(d_ff):
    n_full, rest = divmod(d_ff, FFN_FF_CHUNK)
    assert rest % V7X_LANES == 0
    return (FFN_FF_CHUNK,) * n_full + ((rest,) if rest else ())


def _ffn(x2d, gain, w_in, w_out, *, tokens, sub_rows, mixer_args=None, final_gain=None):
    n_tok, d = x2d.shape
    d_ff = w_out.shape[0]
    tok = lambda i: (i, 0)
    const = lambda i: (0, 0)
    resident = lambda shape: pl.BlockSpec(shape, const, pipeline_mode=pl.Buffered(1))
    in_specs = [pl.BlockSpec((tokens, d), tok)]
    args = [x2d]
    if mixer_args is not None:
        attn, o_f, o_b, rg, attn_gain, rec_gain, w_mix = mixer_args
        in_specs += [pl.BlockSpec((tokens, ATTN_WIDTH), tok)] + [pl.BlockSpec((tokens, REC_WIDTH), tok)] * 3
        in_specs += [pl.BlockSpec((1, ATTN_WIDTH), const), pl.BlockSpec((1, REC_HEAD_DIM), const),
                     resident(w_mix.shape)]
        args += [attn, o_f, o_b, rg, attn_gain.reshape(1, ATTN_WIDTH), rec_gain.reshape(1, REC_HEAD_DIM), w_mix]
    in_specs += [pl.BlockSpec((1, d), const), resident(w_in.shape), resident(w_out.shape)]
    args += [gain.reshape(1, d), w_in, w_out]
    if final_gain is not None:
        in_specs.append(pl.BlockSpec((1, d), const))
        args.append(final_gain.reshape(1, d))
    body = functools.partial(
        _ffn_kernel, mixer=mixer_args is not None, final_norm=final_gain is not None,
        sub_rows=sub_rows, ff_chunks=_ffn_chunks(d_ff))
    return pl.pallas_call(
        body,
        grid=(n_tok // tokens,),
        in_specs=in_specs,
        out_specs=pl.BlockSpec((tokens, d), tok),
        out_shape=jax.ShapeDtypeStruct((n_tok, d), F32),
        scratch_shapes=[pltpu.VMEM((tokens // sub_rows, sub_rows, d), F32)],
        compiler_params=pltpu.CompilerParams(
            dimension_semantics=("parallel",), vmem_limit_bytes=VMEM_LIMIT),
        name="ffn_mix" if mixer_args is not None else "ffn",
    )(*args)


def _rope(x, cos, sin_signed, first_half):
    swapped = jnp.where(first_half, pltpu.roll(x, 96, 1), pltpu.roll(x, 32, 1))
    return x * cos + swapped * sin_signed


def _mix_in_kernel(x_ref, gain_ref, w_ref, qg_ref, kg_ref, cos_ref, sin_ref, lbl_ref, csel_ref,
                   q_ref, k_ref, v_ref, rq_ref, gf_ref, gb_ref, ri_ref, rg_ref, ctot_ref, *, layer, sub_rows):
    col = {}
    c = 0
    for name, width in (("aq", ATTN_WIDTH), ("ak", KV_WIDTH), ("av", KV_WIDTH), ("rq", REC_WIDTH),
                        ("zf", REC_WIDTH), ("zb", REC_WIDTH), ("ri", REC_WIDTH), ("rg", REC_WIDTH)):
        col[name] = (c, width)
        c += width
    lane = lax.broadcasted_iota(jnp.int32, (sub_rows, HEAD_DIM), 1)
    first_half = (lane % (ROPE_AXIS_DIM)) < (ROPE_AXIS_DIM // 2)
    chunk_lane = lax.broadcasted_iota(jnp.int32, ctot_ref.shape, 1)
    ctot = jnp.zeros(ctot_ref.shape, F32)
    gates = []

    for sub in range(x_ref.shape[0] // sub_rows):
        rows = slice(sub * sub_rows, (sub + 1) * sub_rows)
        h = _rms(x_ref[rows, :], gain_ref[...]).astype(BF16)
        cos = cos_ref[rows, :]
        sin = sin_ref[rows, :]

        def proj(name):
            c0, width = col[name]
            return _dot(h, w_ref[:, c0:c0 + width])

        for d, (name, g_ref) in enumerate((("zf", gf_ref), ("zb", gb_ref))):
            logits = lbl_ref[d]
            e = jnp.exp(logits - jnp.max(logits, axis=0, keepdims=True))
            lb = jnp.sum(e[:layer + 1], axis=0, keepdims=True) / jnp.sum(e, axis=0, keepdims=True)
            g = jnp.log2(lb + (1.0 - lb) * jax.nn.sigmoid(proj(name)))
            g_ref[rows, :] = g
            gates.append((d, rows, g.astype(BF16)))

        aq = proj("aq")
        for hh in range(N_Q_HEADS):
            sl = slice(hh * HEAD_DIM, (hh + 1) * HEAD_DIM)
            qh = _rope(_rms(aq[:, sl], qg_ref[...]), cos, sin, first_half) * (HEAD_DIM ** -0.5 * LOG2_E)
            q_ref[rows, sl] = qh.astype(BF16)
        ak = proj("ak")
        for hh in range(N_KV_HEADS):
            sl = slice(hh * HEAD_DIM, (hh + 1) * HEAD_DIM)
            k_ref[rows, sl] = _rope(_rms(ak[:, sl], kg_ref[...]), cos, sin, first_half).astype(BF16)
        av = proj("av").astype(BF16)
        for hh in range(N_KV_HEADS):
            v_ref[rows, 2 * hh * HEAD_DIM:(2 * hh + 1) * HEAD_DIM] = av[:, hh * HEAD_DIM:(hh + 1) * HEAD_DIM]
            v_ref[rows, (2 * hh + 1) * HEAD_DIM:(2 * hh + 2) * HEAD_DIM] = jnp.ones((sub_rows, HEAD_DIM), BF16)
        rq_ref[rows, :] = proj("rq")
        ri_ref[rows, :] = proj("ri").astype(ri_ref.dtype)
        rg_ref[rows, :] = proj("rg").astype(rg_ref.dtype)
    for d, rows, g_bf16 in gates:
        chunk_sum = _dot(csel_ref[:, rows], g_bf16)
        ctot = jnp.maximum(ctot, jnp.where(chunk_lane == d, jnp.max(-chunk_sum, axis=-1, keepdims=True), 0.0))
    ctot_ref[...] = ctot


def _rope_tables(seq_len):
    rows = seq_len // GRID_W
    inv_freq = ROPE_THETA ** (-jnp.arange(0, ROPE_AXIS_DIM, 2, dtype=F32) / ROPE_AXIS_DIM)
    ang_r = jnp.arange(rows, dtype=F32)[:, None] * inv_freq[None, :]
    ang_c = jnp.arange(GRID_W, dtype=F32)[:, None] * inv_freq[None, :]
    half = ROPE_AXIS_DIM // 2

    def expand(row_part, col_part):
        r = jnp.broadcast_to(row_part[:, None, :], (rows, GRID_W, 2 * half))
        c = jnp.broadcast_to(col_part[None, :, :], (rows, GRID_W, 2 * half))
        return jnp.concatenate([r, c], axis=-1).reshape(seq_len, 4 * half)

    cos = expand(jnp.concatenate([jnp.cos(ang_r)] * 2, -1), jnp.concatenate([jnp.cos(ang_c)] * 2, -1))
    sin = expand(jnp.concatenate([-jnp.sin(ang_r), jnp.sin(ang_r)], -1),
                 jnp.concatenate([-jnp.sin(ang_c), jnp.sin(ang_c)], -1))
    return cos, sin


def _mix_in(x2d, gain, w_in, q_gain, k_gain, lb_logits, layer, seq_len):
    n_tok, d = x2d.shape
    d_in = w_in.shape[1]
    tm = MIX_IN_TOKENS
    cos, sin = _rope_tables(seq_len)
    chunks = tm // REC_CHUNK
    chunk_sel = jnp.asarray(np.repeat(np.eye(chunks, dtype=np.float32), REC_CHUNK, axis=1), BF16)
    tiles_per_seq = seq_len // tm
    tok = lambda i: (i, 0)
    const = lambda i: (0, 0)
    table = lambda i: (i % tiles_per_seq, 0)
    out_shape = [
        jax.ShapeDtypeStruct((n_tok, ATTN_WIDTH), BF16),
        jax.ShapeDtypeStruct((n_tok, KV_WIDTH), BF16),
        jax.ShapeDtypeStruct((n_tok, 2 * KV_WIDTH), BF16),
    ] + [jax.ShapeDtypeStruct((n_tok, REC_WIDTH), F32)] * 3 + [jax.ShapeDtypeStruct((n_tok, REC_WIDTH), BF16)] * 2 + [
        jax.ShapeDtypeStruct((n_tok // REC_CHUNK, V7X_LANES), F32)]
    out_specs = [
        pl.BlockSpec((tm, ATTN_WIDTH), tok),
        pl.BlockSpec((tm, KV_WIDTH), tok),
        pl.BlockSpec((tm, 2 * KV_WIDTH), tok),
    ] + [pl.BlockSpec((tm, REC_WIDTH), tok)] * 5 + [pl.BlockSpec((chunks, V7X_LANES), tok)]
    return pl.pallas_call(
        functools.partial(_mix_in_kernel, layer=layer, sub_rows=MIX_IN_SUB_ROWS),
        grid=(n_tok // tm,),
        in_specs=[
            pl.BlockSpec((tm, d), tok),
            pl.BlockSpec((1, d), const),
            pl.BlockSpec((d, d_in), const),
            pl.BlockSpec((1, HEAD_DIM), const),
            pl.BlockSpec((1, HEAD_DIM), const),
            pl.BlockSpec((tm, HEAD_DIM), table),
            pl.BlockSpec((tm, HEAD_DIM), table),
            pl.BlockSpec(lb_logits.shape, lambda i: (0, 0, 0)),
            pl.BlockSpec(chunk_sel.shape, const),
        ],
        out_specs=out_specs,
        out_shape=out_shape,
        compiler_params=pltpu.CompilerParams(
            dimension_semantics=("parallel",), vmem_limit_bytes=VMEM_LIMIT),
        name="mix_in",
    )(x2d, gain.reshape(1, d), w_in, q_gain.reshape(1, HEAD_DIM), k_gain.reshape(1, HEAD_DIM), cos, sin,
      lb_logits, chunk_sel)


def _attn_kernel(q_ref, k_ref, v_ref, o_ref, q2_scr, s_scr, p_scr, mt_scr, m_scr, acc_scr):
    tq = q_ref.shape[1]
    rows = KV_GROUPS * tq
    tk = s_scr.shape[2]
    n_kv = k_ref.shape[1] // tk

    for g in range(KV_GROUPS):
        q2_scr[g * tq:(g + 1) * tq, :] = q_ref[0, :, g * HEAD_DIM:(g + 1) * HEAD_DIM]
    m_scr[...] = jnp.full_like(m_scr, -jnp.inf)
    acc_scr[...] = jnp.zeros_like(acc_scr)

    def scores(j):
        s = _dot_nt(q2_scr[...], k_ref[0, j * tk:(j + 1) * tk, :])
        s_scr[j % 2] = s
        mt_scr[j % 2] = jnp.broadcast_to(jnp.max(s, axis=-1, keepdims=True), (rows, V7X_LANES))

    def accumulate(j):
        slot = j % 2
        m_prev = m_scr[...]
        m_new = jnp.maximum(m_prev, mt_scr[slot])
        m_scr[...] = m_new
        for r in range(0, rows, ATTN_ROW_CHUNK):
            m_r = m_new[r:r + ATTN_ROW_CHUNK]
            for c in range(0, tk, V7X_LANES):
                s_piece = s_scr[slot, r:r + ATTN_ROW_CHUNK, c:c + V7X_LANES]
                p_scr[slot, r:r + ATTN_ROW_CHUNK, c:c + V7X_LANES] = jnp.exp2(s_piece - m_r).astype(BF16)
        alpha = jnp.exp2(m_prev - m_new)
        pv = _dot(p_scr[slot], v_ref[0, j * tk:(j + 1) * tk, :])
        for half in range(2):
            sl = slice(half * HEAD_DIM, (half + 1) * HEAD_DIM)
            acc_scr[:, sl] = alpha * acc_scr[:, sl] + pv[:, sl]

    scores(0)
    for j in range(n_kv):
        if j + 1 < n_kv:
            scores(j + 1)
        accumulate(j)

    o = (acc_scr[:, 0:HEAD_DIM] / acc_scr[:, HEAD_DIM:2 * HEAD_DIM]).astype(o_ref.dtype)
    for g in range(KV_GROUPS):
        o_ref[0, :, g * HEAD_DIM:(g + 1) * HEAD_DIM] = o[g * tq:(g + 1) * tq]


def _attention(q, k, v_ext):
    bsz, seq_len, _ = q.shape
    tq, tk = ATTN_Q_TOKENS, ATTN_KV_TOKENS
    gw = KV_GROUPS * HEAD_DIM
    rows = KV_GROUPS * tq
    return pl.pallas_call(
        _attn_kernel,
        grid=(bsz, N_KV_HEADS, seq_len // tq),
        in_specs=[
            pl.BlockSpec((1, tq, gw), lambda b, h, qi: (b, qi, h)),
            pl.BlockSpec((1, seq_len, HEAD_DIM), lambda b, h, qi: (b, 0, h)),
            pl.BlockSpec((1, seq_len, 2 * HEAD_DIM), lambda b, h, qi: (b, 0, h)),
        ],
        out_specs=pl.BlockSpec((1, tq, gw), lambda b, h, qi: (b, qi, h)),
        out_shape=jax.ShapeDtypeStruct((bsz, seq_len, ATTN_WIDTH), BF16),
        scratch_shapes=[
            pltpu.VMEM((rows, HEAD_DIM), BF16),
            pltpu.VMEM((2, rows, tk), F32),
            pltpu.VMEM((2, rows, tk), BF16),
            pltpu.VMEM((2, rows, V7X_LANES), F32),
            pltpu.VMEM((rows, V7X_LANES), F32),
            pltpu.VMEM((rows, 2 * HEAD_DIM), F32),
        ],
        compiler_params=pltpu.CompilerParams(
            dimension_semantics=("parallel", "parallel", "arbitrary"),
            vmem_limit_bytes=VMEM_LIMIT),
        name="attn",
    )(q, k, v_ext)


def _rec_constants():
    c = REC_CHUNK
    t = np.arange(c)[:, None]
    u = np.arange(c)[None, :]
    blocks = [(u <= t)]
    masks = []
    for lvl in range(REC_LEVELS):
        h = 1 << lvl
        base_t = t - t % (2 * h)
        mid = base_t + h
        upper = t >= mid
        blocks.append(np.where(upper, (u >= mid) & (u <= t), (u > t) & (u < mid)))
        masks.append((t // (2 * h) == u // (2 * h)) & (t % (2 * h) >= h) & (u % (2 * h) < h))
    blocks.append(u > t)
    masks.append(t == u)
    fwd = np.concatenate([b.astype(np.float32) for b in blocks], axis=0)
    fwd_masks = np.stack([m.astype(np.float32) for m in masks])
    n_rows = fwd.shape[0]
    bwd = fwd.reshape(-1, c, c)[:, ::-1, ::-1].reshape(n_rows, c)
    bwd_masks = fwd_masks[:, ::-1, ::-1]
    total = np.ones((8, c), np.float32)
    mats = np.stack([np.concatenate([fwd, total]), np.concatenate([bwd, total])])
    mats = np.concatenate([mats, mats], axis=-1)
    return mats, np.stack([fwd_masks, bwd_masks])


def _split2(x):
    hi = x.astype(BF16)
    lo = (x - hi.astype(F32)).astype(BF16)
    return jnp.concatenate([hi, lo], axis=0)


def _rec_robust(d, rows, q_ref, v_ref, g_ref, o_ref, mat_ref, mask_ref, state_scr):
    c = REC_CHUNK
    g = g_ref[0, rows, :]
    kk = 1.0 - jnp.exp2(g)
    sums = _dot(mat_ref[d], _split2(g))
    q = q_ref[0, rows, :]
    v = v_ref[0, rows, :].astype(BF16)
    for hh in range(N_REC_HEADS):
        sl = slice(hh * REC_HEAD_DIM, (hh + 1) * REC_HEAD_DIM)
        qh = q[:, sl]
        kh = kk[:, sl]
        vh = v[:, sl]
        scores = mask_ref[d, REC_LEVELS] * _dot_nt(qh.astype(BF16), kh.astype(BF16))
        for lvl in range(REC_LEVELS):
            el = jnp.exp2(sums[(lvl + 1) * c:(lvl + 2) * c, sl])
            scores += mask_ref[d, lvl] * _dot_nt((qh * el).astype(BF16), (kh * el).astype(BF16))
        cum = sums[0:c, sl]
        rem = sums[(REC_LEVELS + 1) * c:(REC_LEVELS + 2) * c, sl]
        tot = sums[(REC_LEVELS + 2) * c:(REC_LEVELS + 2) * c + 1, sl]
        state_t = state_scr[d * N_REC_HEADS + hh]
        inter = _dot_nt((qh * jnp.exp2(cum)).astype(BF16), state_t.astype(BF16))
        o_ref[0, rows, sl] = (inter + _dot(scores.astype(BF16), vh)).astype(o_ref.dtype)
        k_tail = (kh * jnp.exp2(rem)).astype(BF16)
        state_scr[d * N_REC_HEADS + hh] = state_t * jnp.exp2(tot) + _dot_tn(vh, k_tail)


def _rec_fast_prep(d, rows, q_ref, v_ref, g_ref, cmat_ref, tri_ref):
    c = REC_CHUNK
    g = g_ref[0, rows, :]
    kk = 1.0 - jnp.exp2(g)
    cs = _dot(cmat_ref[d], _split2(g))
    cum = cs[0:c]
    tot = cs[c:c + 1]
    half = 0.5 * tot
    e_half = jnp.exp2(half)
    k_s = kk * jnp.exp2(half - cum)
    return dict(
        q_s=(q_ref[0, rows, :] * jnp.exp2(cum - half)).astype(BF16),
        k_s=k_s.astype(BF16),
        k_tail=(k_s * e_half).astype(BF16),
        e_half=e_half, e_tot=jnp.exp2(tot),
        v=v_ref[0, rows, :].astype(BF16), causal=tri_ref[d] > 0.0)


def _rec_fast(prep, rows_of, o_refs, state_scr):
    heads = [(d, hh) for d in range(len(o_refs)) for hh in range(N_REC_HEADS)]
    lanes = lambda hh: slice(hh * REC_HEAD_DIM, (hh + 1) * REC_HEAD_DIM)
    updates = [{(d, hh): _dot_tn(pj[d]["v"][:, lanes(hh)], pj[d]["k_tail"][:, lanes(hh)]) for d, hh in heads}
               for pj in prep]
    state = {(d, hh): state_scr[d * N_REC_HEADS + hh] for d, hh in heads}
    res = []
    for pj, uj in zip(prep, updates):
        res_j = {}
        for d, hh in heads:
            p, sl = pj[d], lanes(hh)
            rhs = jnp.concatenate([(state[d, hh] * p["e_half"][:, sl]).astype(BF16), p["k_s"][:, sl]], axis=0)
            res_j[d, hh] = _dot_nt(p["q_s"][:, sl], rhs)
            state[d, hh] = state[d, hh] * p["e_tot"][:, sl] + uj[d, hh]
        res.append(res_j)
    for d, hh in heads:
        state_scr[d * N_REC_HEADS + hh] = state[d, hh]
    for pj, rows_j, res_j in zip(prep, rows_of, res):
        for d, hh in heads:
            p, sl = pj[d], lanes(hh)
            scores = jnp.where(p["causal"], res_j[d, hh][:, REC_HEAD_DIM:], 0.0)
            o = res_j[d, hh][:, :REC_HEAD_DIM] + _dot(scores.astype(BF16), p["v"][:, sl])
            o_refs[d][0, rows_j[d], sl] = o.astype(o_refs[d].dtype)


def _rec_kernel(flag_ref, qf_ref, qb_ref, vf_ref, vb_ref, gf_ref, gb_ref, mat_ref, mask_ref, cmat_ref, tri_ref,
                of_ref, ob_ref, state_scr):
    b = pl.program_id(0)
    i = pl.program_id(1)
    n_steps = pl.num_programs(1)
    per_step = REC_STEP_CHUNKS

    @pl.when(i == 0)
    def _():
        state_scr[...] = jnp.zeros_like(state_scr)

    fwd_chunk0 = (b * n_steps + i) * per_step
    bwd_chunk0 = (b * n_steps + n_steps - 1 - i) * per_step
    fast = flag_ref[fwd_chunk0 * 2] == 1
    for j in range(per_step):
        if j:
            fast = jnp.logical_and(fast, flag_ref[(fwd_chunk0 + j) * 2] == 1)
        fast = jnp.logical_and(fast, flag_ref[(bwd_chunk0 + j) * 2 + 1] == 1)
    dirs = ((qf_ref, vf_ref, gf_ref, of_ref), (qb_ref, vb_ref, gb_ref, ob_ref))
    chunk_rows = lambda j: slice(j * REC_CHUNK, (j + 1) * REC_CHUNK)
    rows_of = [(chunk_rows(j), chunk_rows(per_step - 1 - j)) for j in range(per_step)]

    @pl.when(fast)
    def _():
        prep = [[_rec_fast_prep(d, rows_of[j][d], q_ref, v_ref, g_ref, cmat_ref, tri_ref)
                 for d, (q_ref, v_ref, g_ref, _) in enumerate(dirs)] for j in range(per_step)]
        _rec_fast(prep, rows_of, (of_ref, ob_ref), state_scr)

    @pl.when(jnp.logical_not(fast))
    def _():
        for j in range(per_step):
            for d, (q_ref, v_ref, g_ref, o_ref) in enumerate(dirs):
                _rec_robust(d, rows_of[j][d], q_ref, v_ref, g_ref, o_ref, mat_ref, mask_ref, state_scr)


def _hgrn2(rq, g_f, g_b, ri, fast_flags):
    bsz, seq_len, width = rq.shape
    c = REC_CHUNK
    n_steps = seq_len // (c * REC_STEP_CHUNKS)
    mats, masks = _rec_constants()
    n_lvl_rows = (REC_LEVELS + 2) * c
    cmat = np.concatenate([mats[:, 0:c], mats[:, n_lvl_rows:]], axis=1)
    tri = masks.sum(axis=1)
    mats, cmat = jnp.asarray(mats, BF16), jnp.asarray(cmat, BF16)
    masks, tri = jnp.asarray(masks, F32), jnp.asarray(tri, F32)
    fwd = lambda b, i, flags: (b, i, 0)
    bwd = lambda b, i, flags: (b, n_steps - 1 - i, 0)
    blk = (1, c * REC_STEP_CHUNKS, width)
    full3 = lambda b, i, flags: (0, 0, 0)
    full4 = lambda b, i, flags: (0, 0, 0, 0)
    grid_spec = pltpu.PrefetchScalarGridSpec(
        num_scalar_prefetch=1,
        grid=(bsz, n_steps),
        in_specs=[
            pl.BlockSpec(blk, fwd), pl.BlockSpec(blk, bwd),
            pl.BlockSpec(blk, fwd), pl.BlockSpec(blk, bwd),
            pl.BlockSpec(blk, fwd), pl.BlockSpec(blk, bwd),
            pl.BlockSpec(mats.shape, full3),
            pl.BlockSpec(masks.shape, full4),
            pl.BlockSpec(cmat.shape, full3),
            pl.BlockSpec(tri.shape, full3),
        ],
        out_specs=[pl.BlockSpec(blk, fwd), pl.BlockSpec(blk, bwd)],
        scratch_shapes=[pltpu.VMEM((2 * N_REC_HEADS, REC_HEAD_DIM, REC_HEAD_DIM), F32)],
    )
    return pl.pallas_call(
        _rec_kernel,
        grid_spec=grid_spec,
        out_shape=[jax.ShapeDtypeStruct((bsz, seq_len, width), BF16)] * 2,
        compiler_params=pltpu.CompilerParams(
            dimension_semantics=("parallel", "arbitrary"), vmem_limit_bytes=VMEM_LIMIT),
        name="hgrn2",
    )(fast_flags, rq, rq, ri, ri, g_f, g_b, mats, masks, cmat, tri)


def kernel(x, ffn1_norm, ffn1_w_in, ffn1_w_out, mix_norm, w_in_mix, attn_q_norm, attn_k_norm,
           attn_out_norm, rec_lb_logits, rec_out_norm, w_out_mix, ffn2_norm, ffn2_w_in,
           ffn2_w_out, final_norm):
    bsz, seq_len, d = x.shape
    depth = ffn1_norm.shape[0]
    n_tok = bsz * seq_len
    h = x.reshape(n_tok, d)
    to3 = lambda a: a.reshape(bsz, seq_len, a.shape[-1])
    to2 = lambda a: a.reshape(n_tok, a.shape[-1])
    for l in range(depth):
        h = _ffn(h, ffn1_norm[l], ffn1_w_in[l].astype(BF16), ffn1_w_out[l].astype(BF16),
                 tokens=FFN_TOKENS, sub_rows=FFN_SUB_ROWS)
        q, k, v_ext, rq, g_f, g_b, ri, rg, chunk_decay = _mix_in(
            h, mix_norm[l], w_in_mix[l].astype(BF16), attn_q_norm[l], attn_k_norm[l],
            rec_lb_logits, l, seq_len)
        attn = _attention(to3(q), to3(k), to3(v_ext))
        fast_flags = (chunk_decay[:, :2] <= REC_FAST_RANGE * LOG2_E).astype(jnp.int32).reshape(-1)
        o_f, o_b = _hgrn2(to3(rq), to3(g_f), to3(g_b), to3(ri), fast_flags)
        mixer_args = (to2(attn), to2(o_f), to2(o_b), rg, attn_out_norm[l], rec_out_norm[l],
                      w_out_mix[l].astype(BF16))
        h = _ffn(h, ffn2_norm[l], ffn2_w_in[l].astype(BF16), ffn2_w_out[l].astype(BF16),
                 tokens=FFN_MIX_TOKENS, sub_rows=FFN_MIX_SUB_ROWS, mixer_args=mixer_args,
                 final_gain=final_norm[l])
    return h.reshape(bsz, seq_len, d)
```

```python
import functools

import jax
import jax.numpy as jnp
import numpy as np
from jax import lax
from jax.experimental import pallas as pl
from jax.experimental.pallas import tpu as pltpu

F32 = jnp.float32
BF16 = jnp.bfloat16

EPS = 1e-6
GRID_W = 64
HEAD_DIM = 128
N_Q_HEADS = 4
N_KV_HEADS = 2
KV_GROUPS = N_Q_HEADS // N_KV_HEADS
ATTN_WIDTH = N_Q_HEADS * HEAD_DIM
KV_WIDTH = N_KV_HEADS * HEAD_DIM
ROPE_THETA = 10000.0
ROPE_AXIS_DIM = HEAD_DIM // 2
REC_HEAD_DIM = 128
N_REC_HEADS = 4
REC_WIDTH = N_REC_HEADS * REC_HEAD_DIM

V7X_LANES = 128

FFN_TOKENS = 1024
FFN_SUB_ROWS = 512
FFN_MIX_TOKENS = 1024
FFN_MIX_SUB_ROWS = 512
FFN_FF_CHUNK = 512
MIX_IN_TOKENS = 512
MIX_IN_SUB_ROWS = 256
ATTN_Q_TOKENS = 512
ATTN_KV_TOKENS = 1024
ATTN_ROW_CHUNK = 128
LOG2_E = 1.4426950408889634
REC_CHUNK = 64
REC_STEP_CHUNKS = 4
REC_LEVELS = 6
STAT_DECAY_LANE = 0
STAT_Q_LANE = 2
STAT_K_LANE = STAT_Q_LANE + N_Q_HEADS
ATTN_FAST_BOUND = 50.0
REC_FAST_RANGE = 100.0

VMEM_LIMIT = 56 * 1024 * 1024


def _rms(x, gain):
    return x * lax.rsqrt(jnp.mean(x * x, axis=-1, keepdims=True) + EPS) * gain


def _dot(a, b):
    return jnp.dot(a, b, preferred_element_type=F32)


def _dot_nt(a, b):
    return lax.dot_general(a, b, (((1,), (1,)), ((), ())), preferred_element_type=F32)


def _dot_tn(a, b):
    return lax.dot_general(a, b, (((0,), (0,)), ((), ())), preferred_element_type=F32)


def _mixer_out_rows(rows, attn_ref, of_ref, ob_ref, rg_ref, ag_ref, rgain_ref, wmix_ref):
    parts = [_rms(attn_ref[rows, :].astype(F32), ag_ref[...]).astype(BF16)]
    for hh in range(N_REC_HEADS):
        sl = slice(hh * REC_HEAD_DIM, (hh + 1) * REC_HEAD_DIM)
        gate = rg_ref[rows, sl].astype(F32)
        o = of_ref[rows, sl].astype(F32) + ob_ref[rows, sl].astype(F32)
        parts.append((_rms(o, rgain_ref[...]) * (gate * jax.nn.sigmoid(gate))).astype(BF16))
    return _dot(jnp.concatenate(parts, axis=-1), wmix_ref[...])


def _ffn_kernel(*refs, mixer, final_norm, sub_rows, ff_chunks):
    refs = list(refs)
    x_ref = refs.pop(0)
    mix_refs = [refs.pop(0) for _ in range(7)] if mixer else None
    gain_ref, win_ref, wout_ref = refs.pop(0), refs.pop(0), refs.pop(0)
    fgain_ref = refs.pop(0) if final_norm else None
    out_ref, acc_scr = refs
    d_ff = wout_ref.shape[0]

    for sub in range(x_ref.shape[0] // sub_rows):
        rows = slice(sub * sub_rows, (sub + 1) * sub_rows)
        x = x_ref[rows, :]
        res_ref = x_ref
        if mixer:
            x = x + _mixer_out_rows(rows, *mix_refs)
            out_ref[rows, :] = x
            res_ref = out_ref
        h = _rms(x, gain_ref[...]).astype(BF16)
        c0 = 0
        for width in ff_chunks:
            gate = _dot(h, win_ref[:, c0:c0 + width])
            up = _dot(h, win_ref[:, d_ff + c0:d_ff + c0 + width])
            act = ((gate * jax.nn.sigmoid(gate)) * up).astype(BF16)
            contrib = _dot(act, wout_ref[c0:c0 + width, :])
            if c0 == 0:
                acc_scr[sub] = contrib
            else:
                acc_scr[sub] += contrib
            c0 += width
        y = res_ref[rows, :] + 0.5 * acc_scr[sub]
        if final_norm:
            y = _rms(y, fgain_ref[...])
        out_ref[rows, :] = y


def _ffn_chunks(d_ff):
    n_full, rest = divmod(d_ff, FFN_FF_CHUNK)
    assert rest % V7X_LANES == 0
    return (FFN_FF_CHUNK,) * n_full + ((rest,) if rest else ())


def _ffn(x2d, gain, w_in, w_out, *, tokens, sub_rows, mixer_args=None, final_gain=None):
    n_tok, d = x2d.shape
    d_ff = w_out.shape[0]
    tok = lambda i: (i, 0)
    const = lambda i: (0, 0)
    resident = lambda shape: pl.BlockSpec(shape, const, pipeline_mode=pl.Buffered(1))
    in_specs = [pl.BlockSpec((tokens, d), tok)]
    args = [x2d]
    if mixer_args is not None:
        attn, o_f, o_b, rg, attn_gain, rec_gain, w_mix = mixer_args
        in_specs += [pl.BlockSpec((tokens, ATTN_WIDTH), tok)] + [pl.BlockSpec((tokens, REC_WIDTH), tok)] * 3
        in_specs += [pl.BlockSpec((1, ATTN_WIDTH), const), pl.BlockSpec((1, REC_HEAD_DIM), const),
                     resident(w_mix.shape)]
        args += [attn, o_f, o_b, rg, attn_gain.reshape(1, ATTN_WIDTH), rec_gain.reshape(1, REC_HEAD_DIM), w_mix]
    in_specs += [pl.BlockSpec((1, d), const), resident(w_in.shape), resident(w_out.shape)]
    args += [gain.reshape(1, d), w_in, w_out]
    if final_gain is not None:
        in_specs.append(pl.BlockSpec((1, d), const))
        args.append(final_gain.reshape(1, d))
    body = functools.partial(
        _ffn_kernel, mixer=mixer_args is not None, final_norm=final_gain is not None,
        sub_rows=sub_rows, ff_chunks=_ffn_chunks(d_ff))
    return pl.pallas_call(
        body,
        grid=(n_tok // tokens,),
        in_specs=in_specs,
        out_specs=pl.BlockSpec((tokens, d), tok),
        out_shape=jax.ShapeDtypeStruct((n_tok, d), F32),
        scratch_shapes=[pltpu.VMEM((tokens // sub_rows, sub_rows, d), F32)],
        compiler_params=pltpu.CompilerParams(
            dimension_semantics=("parallel",), vmem_limit_bytes=VMEM_LIMIT),
        name="ffn_mix" if mixer_args is not None else "ffn",
    )(*args)


def _rope(x, cos, sin_signed, first_half):
    swapped = jnp.where(first_half, pltpu.roll(x, 96, 1), pltpu.roll(x, 32, 1))
    return x * cos + swapped * sin_signed


def _mix_in_kernel(x_ref, gain_ref, w_ref, qg_ref, kg_ref, cos_ref, sin_ref, lbl_ref, csel_ref,
                   q_ref, k_ref, v_ref, rq_ref, gf_ref, gb_ref, ri_ref, rg_ref, ctot_ref, *, layer, sub_rows):
    col = {}
    c = 0
    for name, width in (("aq", ATTN_WIDTH), ("ak", KV_WIDTH), ("av", KV_WIDTH), ("rq", REC_WIDTH),
                        ("zf", REC_WIDTH), ("zb", REC_WIDTH), ("ri", REC_WIDTH), ("rg", REC_WIDTH)):
        col[name] = (c, width)
        c += width
    lane = lax.broadcasted_iota(jnp.int32, (sub_rows, HEAD_DIM), 1)
    first_half = (lane % (ROPE_AXIS_DIM)) < (ROPE_AXIS_DIM // 2)
    chunk_lane = lax.broadcasted_iota(jnp.int32, ctot_ref.shape, 1)
    ctot = jnp.zeros(ctot_ref.shape, F32)
    gates = []

    for sub in range(x_ref.shape[0] // sub_rows):
        rows = slice(sub * sub_rows, (sub + 1) * sub_rows)
        h = _rms(x_ref[rows, :], gain_ref[...]).astype(BF16)
        cos = cos_ref[rows, :]
        sin = sin_ref[rows, :]

        def proj(name):
            c0, width = col[name]
            return _dot(h, w_ref[:, c0:c0 + width])

        for d, (name, g_ref) in enumerate((("zf", gf_ref), ("zb", gb_ref))):
            logits = lbl_ref[d]
            e = jnp.exp(logits - jnp.max(logits, axis=0, keepdims=True))
            lb = jnp.sum(e[:layer + 1], axis=0, keepdims=True) / jnp.sum(e, axis=0, keepdims=True)
            g = jnp.log2(lb + (1.0 - lb) * jax.nn.sigmoid(proj(name)))
            g_ref[rows, :] = g
            gates.append((d, rows, g.astype(BF16)))

        def max_sq_norm(x_bf16):
            x = x_bf16.astype(F32)
            return jnp.max(jnp.sum(x * x, axis=-1, keepdims=True), axis=0, keepdims=True)

        aq = proj("aq")
        for hh in range(N_Q_HEADS):
            sl = slice(hh * HEAD_DIM, (hh + 1) * HEAD_DIM)
            qh = _rope(_rms(aq[:, sl], qg_ref[...]), cos, sin, first_half) * (HEAD_DIM ** -0.5 * LOG2_E)
            qh = qh.astype(BF16)
            q_ref[rows, sl] = qh
            ctot = jnp.maximum(ctot, jnp.where(chunk_lane == STAT_Q_LANE + hh, max_sq_norm(qh), 0.0))
        ak = proj("ak")
        for hh in range(N_KV_HEADS):
            sl = slice(hh * HEAD_DIM, (hh + 1) * HEAD_DIM)
            kh = _rope(_rms(ak[:, sl], kg_ref[...]), cos, sin, first_half).astype(BF16)
            k_ref[rows, sl] = kh
            ctot = jnp.maximum(ctot, jnp.where(chunk_lane == STAT_K_LANE + hh, max_sq_norm(kh), 0.0))
        av = proj("av").astype(BF16)
        for hh in range(N_KV_HEADS):
            v_ref[rows, 2 * hh * HEAD_DIM:(2 * hh + 1) * HEAD_DIM] = av[:, hh * HEAD_DIM:(hh + 1) * HEAD_DIM]
            v_ref[rows, (2 * hh + 1) * HEAD_DIM:(2 * hh + 2) * HEAD_DIM] = jnp.ones((sub_rows, HEAD_DIM), BF16)
        rq_ref[rows, :] = proj("rq")
        ri_ref[rows, :] = proj("ri").astype(ri_ref.dtype)
        rg_ref[rows, :] = proj("rg").astype(rg_ref.dtype)
    for d, rows, g_bf16 in gates:
        chunk_sum = _dot(csel_ref[:, rows], g_bf16)
        ctot = jnp.maximum(ctot, jnp.where(chunk_lane == STAT_DECAY_LANE + d, jnp.max(-chunk_sum, axis=-1, keepdims=True), 0.0))
    ctot_ref[...] = ctot


def _rope_tables(seq_len):
    rows = seq_len // GRID_W
    inv_freq = ROPE_THETA ** (-jnp.arange(0, ROPE_AXIS_DIM, 2, dtype=F32) / ROPE_AXIS_DIM)
    ang_r = jnp.arange(rows, dtype=F32)[:, None] * inv_freq[None, :]
    ang_c = jnp.arange(GRID_W, dtype=F32)[:, None] * inv_freq[None, :]
    half = ROPE_AXIS_DIM // 2

    def expand(row_part, col_part):
        r = jnp.broadcast_to(row_part[:, None, :], (rows, GRID_W, 2 * half))
        c = jnp.broadcast_to(col_part[None, :, :], (rows, GRID_W, 2 * half))
        return jnp.concatenate([r, c], axis=-1).reshape(seq_len, 4 * half)

    cos = expand(jnp.concatenate([jnp.cos(ang_r)] * 2, -1), jnp.concatenate([jnp.cos(ang_c)] * 2, -1))
    sin = expand(jnp.concatenate([-jnp.sin(ang_r), jnp.sin(ang_r)], -1),
                 jnp.concatenate([-jnp.sin(ang_c), jnp.sin(ang_c)], -1))
    return cos, sin


def _mix_in(x2d, gain, w_in, q_gain, k_gain, lb_logits, layer, seq_len):
    n_tok, d = x2d.shape
    d_in = w_in.shape[1]
    tm = MIX_IN_TOKENS
    cos, sin = _rope_tables(seq_len)
    chunks = tm // REC_CHUNK
    chunk_sel = jnp.asarray(np.repeat(np.eye(chunks, dtype=np.float32), REC_CHUNK, axis=1), BF16)
    tiles_per_seq = seq_len // tm
    tok = lambda i: (i, 0)
    const = lambda i: (0, 0)
    table = lambda i: (i % tiles_per_seq, 0)
    out_shape = [
        jax.ShapeDtypeStruct((n_tok, ATTN_WIDTH), BF16),
        jax.ShapeDtypeStruct((n_tok, KV_WIDTH), BF16),
        jax.ShapeDtypeStruct((n_tok, 2 * KV_WIDTH), BF16),
    ] + [jax.ShapeDtypeStruct((n_tok, REC_WIDTH), F32)] * 3 + [jax.ShapeDtypeStruct((n_tok, REC_WIDTH), BF16)] * 2 + [
        jax.ShapeDtypeStruct((n_tok // REC_CHUNK, V7X_LANES), F32)]
    out_specs = [
        pl.BlockSpec((tm, ATTN_WIDTH), tok),
        pl.BlockSpec((tm, KV_WIDTH), tok),
        pl.BlockSpec((tm, 2 * KV_WIDTH), tok),
    ] + [pl.BlockSpec((tm, REC_WIDTH), tok)] * 5 + [pl.BlockSpec((chunks, V7X_LANES), tok)]
    return pl.pallas_call(
        functools.partial(_mix_in_kernel, layer=layer, sub_rows=MIX_IN_SUB_ROWS),
        grid=(n_tok // tm,),
        in_specs=[
            pl.BlockSpec((tm, d), tok),
            pl.BlockSpec((1, d), const),
            pl.BlockSpec((d, d_in), const),
            pl.BlockSpec((1, HEAD_DIM), const),
            pl.BlockSpec((1, HEAD_DIM), const),
            pl.BlockSpec((tm, HEAD_DIM), table),
            pl.BlockSpec((tm, HEAD_DIM), table),
            pl.BlockSpec(lb_logits.shape, lambda i: (0, 0, 0)),
            pl.BlockSpec(chunk_sel.shape, const),
        ],
        out_specs=out_specs,
        out_shape=out_shape,
        compiler_params=pltpu.CompilerParams(
            dimension_semantics=("parallel",), vmem_limit_bytes=VMEM_LIMIT),
        name="mix_in",
    )(x2d, gain.reshape(1, d), w_in, q_gain.reshape(1, HEAD_DIM), k_gain.reshape(1, HEAD_DIM), cos, sin,
      lb_logits, chunk_sel)


def _attn_online(q2_scr, k_ref, v_ref, s_scr, p_scr, mt_scr, m_scr, acc_scr):
    rows, tk = s_scr.shape[1], s_scr.shape[2]
    n_kv = k_ref.shape[1] // tk
    m_scr[...] = jnp.full_like(m_scr, -jnp.inf)
    acc_scr[...] = jnp.zeros_like(acc_scr)

    def scores(j):
        s = _dot_nt(q2_scr[...], k_ref[0, j * tk:(j + 1) * tk, :])
        s_scr[j % 2] = s
        mt_scr[j % 2] = jnp.broadcast_to(jnp.max(s, axis=-1, keepdims=True), (rows, V7X_LANES))

    def accumulate(j):
        slot = j % 2
        m_prev = m_scr[...]
        m_new = jnp.maximum(m_prev, mt_scr[slot])
        m_scr[...] = m_new
        for r in range(0, rows, ATTN_ROW_CHUNK):
            m_r = m_new[r:r + ATTN_ROW_CHUNK]
            for c in range(0, tk, V7X_LANES):
                s_piece = s_scr[slot, r:r + ATTN_ROW_CHUNK, c:c + V7X_LANES]
                p_scr[slot, r:r + ATTN_ROW_CHUNK, c:c + V7X_LANES] = jnp.exp2(s_piece - m_r).astype(BF16)
        alpha = jnp.exp2(m_prev - m_new)
        pv = _dot(p_scr[slot], v_ref[0, j * tk:(j + 1) * tk, :])
        for half in range(2):
            sl = slice(half * HEAD_DIM, (half + 1) * HEAD_DIM)
            acc_scr[:, sl] = alpha * acc_scr[:, sl] + pv[:, sl]

    scores(0)
    for j in range(n_kv):
        if j + 1 < n_kv:
            scores(j + 1)
        accumulate(j)


def _attn_fixed_shift(q2_scr, k_ref, v_ref, kbound_ref, s_scr, p_scr, m_scr, acc_scr):
    rows, tk = s_scr.shape[1], s_scr.shape[2]
    n_kv = k_ref.shape[1] // tk
    q = q2_scr[...].astype(F32)
    q_norm = jnp.sqrt(jnp.sum(q * q, axis=-1, keepdims=True))
    m_scr[...] = q_norm * kbound_ref[0, 0:1, :]
    acc_scr[...] = jnp.zeros_like(acc_scr)

    def scores(j):
        s_scr[j % 2] = _dot_nt(q2_scr[...], k_ref[0, j * tk:(j + 1) * tk, :])

    scores(0)
    for j in range(n_kv):
        slot = j % 2
        if j + 1 < n_kv:
            scores(j + 1)
        for r in range(0, rows, ATTN_ROW_CHUNK):
            shift = m_scr[r:r + ATTN_ROW_CHUNK, :]
            for c in range(0, tk, V7X_LANES):
                s_piece = s_scr[slot, r:r + ATTN_ROW_CHUNK, c:c + V7X_LANES]
                p_scr[slot, r:r + ATTN_ROW_CHUNK, c:c + V7X_LANES] = jnp.exp2(s_piece - shift).astype(BF16)
        acc_scr[...] += _dot(p_scr[slot], v_ref[0, j * tk:(j + 1) * tk, :])


def _attn_kernel(flag_ref, q_ref, k_ref, v_ref, kbound_ref, o_ref, q2_scr, s_scr, p_scr, mt_scr, m_scr, acc_scr):
    tq = q_ref.shape[1]
    for g in range(KV_GROUPS):
        q2_scr[g * tq:(g + 1) * tq, :] = q_ref[0, :, g * HEAD_DIM:(g + 1) * HEAD_DIM]
    fixed_shift = flag_ref[pl.program_id(0) * N_KV_HEADS + pl.program_id(1)] == 1

    @pl.when(fixed_shift)
    def _():
        _attn_fixed_shift(q2_scr, k_ref, v_ref, kbound_ref, s_scr, p_scr, m_scr, acc_scr)

    @pl.when(jnp.logical_not(fixed_shift))
    def _():
        _attn_online(q2_scr, k_ref, v_ref, s_scr, p_scr, mt_scr, m_scr, acc_scr)

    o = (acc_scr[:, 0:HEAD_DIM] / acc_scr[:, HEAD_DIM:2 * HEAD_DIM]).astype(o_ref.dtype)
    for g in range(KV_GROUPS):
        o_ref[0, :, g * HEAD_DIM:(g + 1) * HEAD_DIM] = o[g * tq:(g + 1) * tq]


def _attention(q, k, v_ext, fixed_shift_flags, k_bound):
    bsz, seq_len, _ = q.shape
    tq, tk = ATTN_Q_TOKENS, ATTN_KV_TOKENS
    gw = KV_GROUPS * HEAD_DIM
    rows = KV_GROUPS * tq
    grid_spec = pltpu.PrefetchScalarGridSpec(
        num_scalar_prefetch=1,
        grid=(bsz, N_KV_HEADS, seq_len // tq),
        in_specs=[
            pl.BlockSpec((1, tq, gw), lambda b, h, qi, flags: (b, qi, h)),
            pl.BlockSpec((1, seq_len, HEAD_DIM), lambda b, h, qi, flags: (b, 0, h)),
            pl.BlockSpec((1, seq_len, 2 * HEAD_DIM), lambda b, h, qi, flags: (b, 0, h)),
            pl.BlockSpec((1,) + k_bound.shape[1:], lambda b, h, qi, flags: (b * N_KV_HEADS + h, 0, 0)),
        ],
        out_specs=pl.BlockSpec((1, tq, gw), lambda b, h, qi, flags: (b, qi, h)),
        scratch_shapes=[
            pltpu.VMEM((rows, HEAD_DIM), BF16),
            pltpu.VMEM((2, rows, tk), F32),
            pltpu.VMEM((2, rows, tk), BF16),
            pltpu.VMEM((2, rows, V7X_LANES), F32),
            pltpu.VMEM((rows, V7X_LANES), F32),
            pltpu.VMEM((rows, 2 * HEAD_DIM), F32),
        ],
    )
    return pl.pallas_call(
        _attn_kernel,
        grid_spec=grid_spec,
        out_shape=jax.ShapeDtypeStruct((bsz, seq_len, ATTN_WIDTH), BF16),
        compiler_params=pltpu.CompilerParams(
            dimension_semantics=("parallel", "parallel", "arbitrary"),
            vmem_limit_bytes=VMEM_LIMIT),
        name="attn",
    )(fixed_shift_flags, q, k, v_ext, k_bound)


def _attention_plan(stats, bsz):
    stats = stats.reshape(bsz, -1, V7X_LANES)
    q_norm = jnp.sqrt(jnp.max(stats[:, :, STAT_Q_LANE:STAT_Q_LANE + N_Q_HEADS], axis=1))
    k_norm = jnp.sqrt(jnp.max(stats[:, :, STAT_K_LANE:STAT_K_LANE + N_KV_HEADS], axis=1))
    k_norm = k_norm * 1.01
    score_bound = jnp.max(q_norm.reshape(bsz, N_KV_HEADS, KV_GROUPS), axis=-1) * k_norm
    flags = (score_bound <= ATTN_FAST_BOUND).astype(jnp.int32).reshape(-1)
    k_bound = jnp.broadcast_to(k_norm.reshape(-1, 1, 1), (bsz * N_KV_HEADS, 8, V7X_LANES))
    return flags, k_bound


def _rec_constants():
    c = REC_CHUNK
    t = np.arange(c)[:, None]
    u = np.arange(c)[None, :]
    blocks = [(u <= t)]
    masks = []
    for lvl in range(REC_LEVELS):
        h = 1 << lvl
        base_t = t - t % (2 * h)
        mid = base_t + h
        upper = t >= mid
        blocks.append(np.where(upper, (u >= mid) & (u <= t), (u > t) & (u < mid)))
        masks.append((t // (2 * h) == u // (2 * h)) & (t % (2 * h) >= h) & (u % (2 * h) < h))
    blocks.append(u > t)
    masks.append(t == u)
    fwd = np.concatenate([b.astype(np.float32) for b in blocks], axis=0)
    fwd_masks = np.stack([m.astype(np.float32) for m in masks])
    n_rows = fwd.shape[0]
    bwd = fwd.reshape(-1, c, c)[:, ::-1, ::-1].reshape(n_rows, c)
    bwd_masks = fwd_masks[:, ::-1, ::-1]
    total = np.ones((8, c), np.float32)
    mats = np.stack([np.concatenate([fwd, total]), np.concatenate([bwd, total])])
    mats = np.concatenate([mats, mats], axis=-1)
    return mats, np.stack([fwd_masks, bwd_masks])


def _split2(x):
    hi = x.astype(BF16)
    lo = (x - hi.astype(F32)).astype(BF16)
    return jnp.concatenate([hi, lo], axis=0)


def _rec_robust(d, rows, q_ref, v_ref, g_ref, o_ref, mat_ref, mask_ref, state_scr):
    c = REC_CHUNK
    g = g_ref[0, rows, :]
    kk = 1.0 - jnp.exp2(g)
    sums = _dot(mat_ref[d], _split2(g))
    q = q_ref[0, rows, :]
    v = v_ref[0, rows, :].astype(BF16)
    for hh in range(N_REC_HEADS):
        sl = slice(hh * REC_HEAD_DIM, (hh + 1) * REC_HEAD_DIM)
        qh = q[:, sl]
        kh = kk[:, sl]
        vh = v[:, sl]
        scores = mask_ref[d, REC_LEVELS] * _dot_nt(qh.astype(BF16), kh.astype(BF16))
        for lvl in range(REC_LEVELS):
            el = jnp.exp2(sums[(lvl + 1) * c:(lvl + 2) * c, sl])
            scores += mask_ref[d, lvl] * _dot_nt((qh * el).astype(BF16), (kh * el).astype(BF16))
        cum = sums[0:c, sl]
        rem = sums[(REC_LEVELS + 1) * c:(REC_LEVELS + 2) * c, sl]
        tot = sums[(REC_LEVELS + 2) * c:(REC_LEVELS + 2) * c + 1, sl]
        state_t = state_scr[d * N_REC_HEADS + hh]
        inter = _dot_nt((qh * jnp.exp2(cum)).astype(BF16), state_t.astype(BF16))
        o_ref[0, rows, sl] = (inter + _dot(scores.astype(BF16), vh)).astype(o_ref.dtype)
        k_tail = (kh * jnp.exp2(rem)).astype(BF16)
        state_scr[d * N_REC_HEADS + hh] = state_t * jnp.exp2(tot) + _dot_tn(vh, k_tail)


def _rec_fast_prep(d, rows, q_ref, v_ref, g_ref, cmat_ref, tri_ref):
    c = REC_CHUNK
    g = g_ref[0, rows, :]
    kk = 1.0 - jnp.exp2(g)
    cs = _dot(cmat_ref[d], _split2(g))
    cum = cs[0:c]
    tot = cs[c:c + 1]
    half = 0.5 * tot
    e_half = jnp.exp2(half)
    k_s = kk * jnp.exp2(half - cum)
    return dict(
        q_s=(q_ref[0, rows, :] * jnp.exp2(cum - half)).astype(BF16),
        k_s=k_s.astype(BF16),
        k_tail=(k_s * e_half).astype(BF16),
        e_half=e_half, e_tot=jnp.exp2(tot),
        v=v_ref[0, rows, :].astype(BF16), causal=tri_ref[d] > 0.0)


def _rec_fast(prep, rows_of, o_refs, state_scr):
    heads = [(d, hh) for d in range(len(o_refs)) for hh in range(N_REC_HEADS)]
    lanes = lambda hh: slice(hh * REC_HEAD_DIM, (hh + 1) * REC_HEAD_DIM)
    updates = [{(d, hh): _dot_tn(pj[d]["v"][:, lanes(hh)], pj[d]["k_tail"][:, lanes(hh)]) for d, hh in heads}
               for pj in prep]
    state = {(d, hh): state_scr[d * N_REC_HEADS + hh] for d, hh in heads}
    res = []
    for pj, uj in zip(prep, updates):
        res_j = {}
        for d, hh in heads:
            p, sl = pj[d], lanes(hh)
            rhs = jnp.concatenate([(state[d, hh] * p["e_half"][:, sl]).astype(BF16), p["k_s"][:, sl]], axis=0)
            res_j[d, hh] = _dot_nt(p["q_s"][:, sl], rhs)
            state[d, hh] = state[d, hh] * p["e_tot"][:, sl] + uj[d, hh]
        res.append(res_j)
    for d, hh in heads:
        state_scr[d * N_REC_HEADS + hh] = state[d, hh]
    for pj, rows_j, res_j in zip(prep, rows_of, res):
        for d, hh in heads:
            p, sl = pj[d], lanes(hh)
            scores = jnp.where(p["causal"], res_j[d, hh][:, REC_HEAD_DIM:], 0.0)
            o = res_j[d, hh][:, :REC_HEAD_DIM] + _dot(scores.astype(BF16), p["v"][:, sl])
            o_refs[d][0, rows_j[d], sl] = o.astype(o_refs[d].dtype)


def _rec_kernel(flag_ref, qf_ref, qb_ref, vf_ref, vb_ref, gf_ref, gb_ref, mat_ref, mask_ref, cmat_ref, tri_ref,
                of_ref, ob_ref, state_scr):
    b = pl.program_id(0)
    i = pl.program_id(1)
    n_steps = pl.num_programs(1)
    per_step = REC_STEP_CHUNKS

    @pl.when(i == 0)
    def _():
        state_scr[...] = jnp.zeros_like(state_scr)

    fwd_chunk0 = (b * n_steps + i) * per_step
    bwd_chunk0 = (b * n_steps + n_steps - 1 - i) * per_step
    fast = flag_ref[fwd_chunk0 * 2] == 1
    for j in range(per_step):
        if j:
            fast = jnp.logical_and(fast, flag_ref[(fwd_chunk0 + j) * 2] == 1)
        fast = jnp.logical_and(fast, flag_ref[(bwd_chunk0 + j) * 2 + 1] == 1)
    dirs = ((qf_ref, vf_ref, gf_ref, of_ref), (qb_ref, vb_ref, gb_ref, ob_ref))
    chunk_rows = lambda j: slice(j * REC_CHUNK, (j + 1) * REC_CHUNK)
    rows_of = [(chunk_rows(j), chunk_rows(per_step - 1 - j)) for j in range(per_step)]

    @pl.when(fast)
    def _():
        prep = [[_rec_fast_prep(d, rows_of[j][d], q_ref, v_ref, g_ref, cmat_ref, tri_ref)
                 for d, (q_ref, v_ref, g_ref, _) in enumerate(dirs)] for j in range(per_step)]
        _rec_fast(prep, rows_of, (of_ref, ob_ref), state_scr)

    @pl.when(jnp.logical_not(fast))
    def _():
        for j in range(per_step):
            for d, (q_ref, v_ref, g_ref, o_ref) in enumerate(dirs):
                _rec_robust(d, rows_of[j][d], q_ref, v_ref, g_ref, o_ref, mat_ref, mask_ref, state_scr)


def _hgrn2(rq, g_f, g_b, ri, fast_flags):
    bsz, seq_len, width = rq.shape
    c = REC_CHUNK
    n_steps = seq_len // (c * REC_STEP_CHUNKS)
    mats, masks = _rec_constants()
    n_lvl_rows = (REC_LEVELS + 2) * c
    cmat = np.concatenate([mats[:, 0:c], mats[:, n_lvl_rows:]], axis=1)
    tri = masks.sum(axis=1)
    mats, cmat = jnp.asarray(mats, BF16), jnp.asarray(cmat, BF16)
    masks, tri = jnp.asarray(masks, F32), jnp.asarray(tri, F32)
    fwd = lambda b, i, flags: (b, i, 0)
    bwd = lambda b, i, flags: (b, n_steps - 1 - i, 0)
    blk = (1, c * REC_STEP_CHUNKS, width)
    full3 = lambda b, i, flags: (0, 0, 0)
    full4 = lambda b, i, flags: (0, 0, 0, 0)
    grid_spec = pltpu.PrefetchScalarGridSpec(
        num_scalar_prefetch=1,
        grid=(bsz, n_steps),
        in_specs=[
            pl.BlockSpec(blk, fwd), pl.BlockSpec(blk, bwd),
            pl.BlockSpec(blk, fwd), pl.BlockSpec(blk, bwd),
            pl.BlockSpec(blk, fwd), pl.BlockSpec(blk, bwd),
            pl.BlockSpec(mats.shape, full3),
            pl.BlockSpec(masks.shape, full4),
            pl.BlockSpec(cmat.shape, full3),
            pl.BlockSpec(tri.shape, full3),
        ],
        out_specs=[pl.BlockSpec(blk, fwd), pl.BlockSpec(blk, bwd)],
        scratch_shapes=[pltpu.VMEM((2 * N_REC_HEADS, REC_HEAD_DIM, REC_HEAD_DIM), F32)],
    )
    return pl.pallas_call(
        _rec_kernel,
        grid_spec=grid_spec,
        out_shape=[jax.ShapeDtypeStruct((bsz, seq_len, width), BF16)] * 2,
        compiler_params=pltpu.CompilerParams(
            dimension_semantics=("parallel", "arbitrary"), vmem_limit_bytes=VMEM_LIMIT),
        name="hgrn2",
    )(fast_flags, rq, rq, ri, ri, g_f, g_b, mats, masks, cmat, tri)


def kernel(x, ffn1_norm, ffn1_w_in, ffn1_w_out, mix_norm, w_in_mix, attn_q_norm, attn_k_norm,
           attn_out_norm, rec_lb_logits, rec_out_norm, w_out_mix, ffn2_norm, ffn2_w_in,
           ffn2_w_out, final_norm):
    bsz, seq_len, d = x.shape
    depth = ffn1_norm.shape[0]
    n_tok = bsz * seq_len
    h = x.reshape(n_tok, d)
    to3 = lambda a: a.reshape(bsz, seq_len, a.shape[-1])
    to2 = lambda a: a.reshape(n_tok, a.shape[-1])
    for l in range(depth):
        h = _ffn(h, ffn1_norm[l], ffn1_w_in[l].astype(BF16), ffn1_w_out[l].astype(BF16),
                 tokens=FFN_TOKENS, sub_rows=FFN_SUB_ROWS)
        q, k, v_ext, rq, g_f, g_b, ri, rg, chunk_stats = _mix_in(
            h, mix_norm[l], w_in_mix[l].astype(BF16), attn_q_norm[l], attn_k_norm[l],
            rec_lb_logits, l, seq_len)
        attn = _attention(to3(q), to3(k), to3(v_ext), *_attention_plan(chunk_stats, bsz))
        decay = chunk_stats[:, STAT_DECAY_LANE:STAT_DECAY_LANE + 2]
        fast_flags = (decay <= REC_FAST_RANGE * LOG2_E).astype(jnp.int32).reshape(-1)
        o_f, o_b = _hgrn2(to3(rq), to3(g_f), to3(g_b), to3(ri), fast_flags)
        mixer_args = (to2(attn), to2(o_f), to2(o_b), rg, attn_out_norm[l], rec_out_norm[l],
                      w_out_mix[l].astype(BF16))
        h = _ffn(h, ffn2_norm[l], ffn2_w_in[l].astype(BF16), ffn2_w_out[l].astype(BF16),
                 tokens=FFN_MIX_TOKENS, sub_rows=FFN_MIX_SUB_ROWS, mixer_args=mixer_args,
                 final_gain=final_norm[l])
    return h.reshape(bsz, seq_len, d)
```

```python
import functools

import jax
import jax.numpy as jnp
import numpy as np
from jax import lax
from jax.experimental import pallas as pl
from jax.experimental.pallas import tpu as pltpu

F32 = jnp.float32
BF16 = jnp.bfloat16

EPS = 1e-6
GRID_W = 64
HEAD_DIM = 128
N_Q_HEADS = 4
N_KV_HEADS = 2
KV_GROUPS = N_Q_HEADS // N_KV_HEADS
ATTN_WIDTH = N_Q_HEADS * HEAD_DIM
KV_WIDTH = N_KV_HEADS * HEAD_DIM
ROPE_THETA = 10000.0
ROPE_AXIS_DIM = HEAD_DIM // 2
REC_HEAD_DIM = 128
N_REC_HEADS = 4
REC_WIDTH = N_REC_HEADS * REC_HEAD_DIM

V7X_LANES = 128

FFN_TOKENS = 1024
FFN_SUB_ROWS = 512
FFN_MIX_TOKENS = 1024
FFN_MIX_SUB_ROWS = 512
FFN_FF_CHUNK = 512
MIX_IN_TOKENS = 512
MIX_IN_SUB_ROWS = 256
ATTN_Q_TOKENS = 512
ATTN_KV_TOKENS = 2048
ATTN_ROW_CHUNK = 128
LOG2_E = 1.4426950408889634
REC_CHUNK = 64
REC_STEP_CHUNKS = 4
REC_LEVELS = 6
REC_FAST_RANGE = 100.0

VMEM_LIMIT = 56 * 1024 * 1024


def _rms(x, gain):
    return x * lax.rsqrt(jnp.mean(x * x, axis=-1, keepdims=True) + EPS) * gain


def _dot(a, b):
    return jnp.dot(a, b, preferred_element_type=F32)


def _dot_nt(a, b):
    return lax.dot_general(a, b, (((1,), (1,)), ((), ())), preferred_element_type=F32)


def _dot_tn(a, b):
    return lax.dot_general(a, b, (((0,), (0,)), ((), ())), preferred_element_type=F32)


def _mixer_out_rows(rows, attn_ref, of_ref, ob_ref, rg_ref, ag_ref, rgain_ref, wmix_ref):
    parts = [_rms(attn_ref[rows, :].astype(F32), ag_ref[...]).astype(BF16)]
    for hh in range(N_REC_HEADS):
        sl = slice(hh * REC_HEAD_DIM, (hh + 1) * REC_HEAD_DIM)
        gate = rg_ref[rows, sl].astype(F32)
        o = of_ref[rows, sl].astype(F32) + ob_ref[rows, sl].astype(F32)
        parts.append((_rms(o, rgain_ref[...]) * (gate * jax.nn.sigmoid(gate))).astype(BF16))
    return _dot(jnp.concatenate(parts, axis=-1), wmix_ref[...])


def _ffn_kernel(*refs, mixer, final_norm, sub_rows, ff_chunks):
    refs = list(refs)
    x_ref = refs.pop(0)
    mix_refs = [refs.pop(0) for _ in range(7)] if mixer else None
    gain_ref, win_ref, wout_ref = refs.pop(0), refs.pop(0), refs.pop(0)
    fgain_ref = refs.pop(0) if final_norm else None
    out_ref, acc_scr = refs
    d_ff = wout_ref.shape[0]

    for sub in range(x_ref.shape[0] // sub_rows):
        rows = slice(sub * sub_rows, (sub + 1) * sub_rows)
        x = x_ref[rows, :]
        res_ref = x_ref
        if mixer:
            x = x + _mixer_out_rows(rows, *mix_refs)
            out_ref[rows, :] = x
            res_ref = out_ref
        h = _rms(x, gain_ref[...]).astype(BF16)
        c0 = 0
        for width in ff_chunks:
            gate = _dot(h, win_ref[:, c0:c0 + width])
            up = _dot(h, win_ref[:, d_ff + c0:d_ff + c0 + width])
            act = ((gate * jax.nn.sigmoid(gate)) * up).astype(BF16)
            contrib = _dot(act, wout_ref[c0:c0 + width, :])
            if c0 == 0:
                acc_scr[sub] = contrib
            else:
                acc_scr[sub] += contrib
            c0 += width
        y = res_ref[rows, :] + 0.5 * acc_scr[sub]
        if final_norm:
            y = _rms(y, fgain_ref[...])
        out_ref[rows, :] = y


def _ffn_chunks(d_ff):
    n_full, rest = divmod(d_ff, FFN_FF_CHUNK)
    assert rest % V7X_LANES == 0
    return (FFN_FF_CHUNK,) * n_full + ((rest,) if rest else ())


def _ffn(x2d, gain, w_in, w_out, *, tokens, sub_rows, mixer_args=None, final_gain=None):
    n_tok, d = x2d.shape
    d_ff = w_out.shape[0]
    tok = lambda i: (i, 0)
    const = lambda i: (0, 0)
    resident = lambda shape: pl.BlockSpec(shape, const, pipeline_mode=pl.Buffered(1))
    in_specs = [pl.BlockSpec((tokens, d), tok)]
    args = [x2d]
    if mixer_args is not None:
        attn, o_f, o_b, rg, attn_gain, rec_gain, w_mix = mixer_args
        in_specs += [pl.BlockSpec((tokens, ATTN_WIDTH), tok)] + [pl.BlockSpec((tokens, REC_WIDTH), tok)] * 3
        in_specs += [pl.BlockSpec((1, ATTN_WIDTH), const), pl.BlockSpec((1, REC_HEAD_DIM), const),
                     resident(w_mix.shape)]
        args += [attn, o_f, o_b, rg, attn_gain.reshape(1, ATTN_WIDTH), rec_gain.reshape(1, REC_HEAD_DIM), w_mix]
    in_specs += [pl.BlockSpec((1, d), const), resident(w_in.shape), resident(w_out.shape)]
    args += [gain.reshape(1, d), w_in, w_out]
    if final_gain is not None:
        in_specs.append(pl.BlockSpec((1, d), const))
        args.append(final_gain.reshape(1, d))
    body = functools.partial(
        _ffn_kernel, mixer=mixer_args is not None, final_norm=final_gain is not None,
        sub_rows=sub_rows, ff_chunks=_ffn_chunks(d_ff))
    return pl.pallas_call(
        body,
        grid=(n_tok // tokens,),
        in_specs=in_specs,
        out_specs=pl.BlockSpec((tokens, d), tok),
        out_shape=jax.ShapeDtypeStruct((n_tok, d), F32),
        scratch_shapes=[pltpu.VMEM((tokens // sub_rows, sub_rows, d), F32)],
        compiler_params=pltpu.CompilerParams(
            dimension_semantics=("parallel",), vmem_limit_bytes=VMEM_LIMIT),
        name="ffn_mix" if mixer_args is not None else "ffn",
    )(*args)


def _rope(x, cos, sin_signed, first_half):
    swapped = jnp.where(first_half, pltpu.roll(x, 96, 1), pltpu.roll(x, 32, 1))
    return x * cos + swapped * sin_signed


def _mix_in_kernel(x_ref, gain_ref, w_ref, qg_ref, kg_ref, cos_ref, sin_ref, lbl_ref, csel_ref,
                   q_ref, k_ref, v_ref, rq_ref, gf_ref, gb_ref, ri_ref, rg_ref, ctot_ref, *, layer, sub_rows):
    col = {}
    c = 0
    for name, width in (("aq", ATTN_WIDTH), ("ak", KV_WIDTH), ("av", KV_WIDTH), ("rq", REC_WIDTH),
                        ("zf", REC_WIDTH), ("zb", REC_WIDTH), ("ri", REC_WIDTH), ("rg", REC_WIDTH)):
        col[name] = (c, width)
        c += width
    lane = lax.broadcasted_iota(jnp.int32, (sub_rows, HEAD_DIM), 1)
    first_half = (lane % (ROPE_AXIS_DIM)) < (ROPE_AXIS_DIM // 2)
    chunk_lane = lax.broadcasted_iota(jnp.int32, ctot_ref.shape, 1)
    ctot = jnp.zeros(ctot_ref.shape, F32)
    gates = []

    for sub in range(x_ref.shape[0] // sub_rows):
        rows = slice(sub * sub_rows, (sub + 1) * sub_rows)
        h = _rms(x_ref[rows, :], gain_ref[...]).astype(BF16)
        cos = cos_ref[rows, :]
        sin = sin_ref[rows, :]

        def proj(name):
            c0, width = col[name]
            return _dot(h, w_ref[:, c0:c0 + width])

        for d, (name, g_ref) in enumerate((("zf", gf_ref), ("zb", gb_ref))):
            logits = lbl_ref[d]
            e = jnp.exp(logits - jnp.max(logits, axis=0, keepdims=True))
            lb = jnp.sum(e[:layer + 1], axis=0, keepdims=True) / jnp.sum(e, axis=0, keepdims=True)
            g = jnp.log2(lb + (1.0 - lb) * jax.nn.sigmoid(proj(name)))
            g_ref[rows, :] = g
            gates.append((d, rows, g.astype(BF16)))

        aq = proj("aq")
        for hh in range(N_Q_HEADS):
            sl = slice(hh * HEAD_DIM, (hh + 1) * HEAD_DIM)
            qh = _rope(_rms(aq[:, sl], qg_ref[...]), cos, sin, first_half) * (HEAD_DIM ** -0.5 * LOG2_E)
            q_ref[rows, sl] = qh.astype(BF16)
        ak = proj("ak")
        for hh in range(N_KV_HEADS):
            sl = slice(hh * HEAD_DIM, (hh + 1) * HEAD_DIM)
            k_ref[rows, sl] = _rope(_rms(ak[:, sl], kg_ref[...]), cos, sin, first_half).astype(BF16)
        av = proj("av").astype(BF16)
        for hh in range(N_KV_HEADS):
            v_ref[rows, 2 * hh * HEAD_DIM:(2 * hh + 1) * HEAD_DIM] = av[:, hh * HEAD_DIM:(hh + 1) * HEAD_DIM]
            v_ref[rows, (2 * hh + 1) * HEAD_DIM:(2 * hh + 2) * HEAD_DIM] = jnp.ones((sub_rows, HEAD_DIM), BF16)
        rq_ref[rows, :] = proj("rq")
        ri_ref[rows, :] = proj("ri").astype(ri_ref.dtype)
        rg_ref[rows, :] = proj("rg").astype(rg_ref.dtype)
    for d, rows, g_bf16 in gates:
        chunk_sum = _dot(csel_ref[:, rows], g_bf16)
        ctot = jnp.maximum(ctot, jnp.where(chunk_lane == d, jnp.max(-chunk_sum, axis=-1, keepdims=True), 0.0))
    ctot_ref[...] = ctot


def _rope_tables(seq_len):
    rows = seq_len // GRID_W
    inv_freq = ROPE_THETA ** (-jnp.arange(0, ROPE_AXIS_DIM, 2, dtype=F32) / ROPE_AXIS_DIM)
    ang_r = jnp.arange(rows, dtype=F32)[:, None] * inv_freq[None, :]
    ang_c = jnp.arange(GRID_W, dtype=F32)[:, None] * inv_freq[None, :]
    half = ROPE_AXIS_DIM // 2

    def expand(row_part, col_part):
        r = jnp.broadcast_to(row_part[:, None, :], (rows, GRID_W, 2 * half))
        c = jnp.broadcast_to(col_part[None, :, :], (rows, GRID_W, 2 * half))
        return jnp.concatenate([r, c], axis=-1).reshape(seq_len, 4 * half)

    cos = expand(jnp.concatenate([jnp.cos(ang_r)] * 2, -1), jnp.concatenate([jnp.cos(ang_c)] * 2, -1))
    sin = expand(jnp.concatenate([-jnp.sin(ang_r), jnp.sin(ang_r)], -1),
                 jnp.concatenate([-jnp.sin(ang_c), jnp.sin(ang_c)], -1))
    return cos, sin


def _mix_in(x2d, gain, w_in, q_gain, k_gain, lb_logits, layer, seq_len):
    n_tok, d = x2d.shape
    d_in = w_in.shape[1]
    tm = MIX_IN_TOKENS
    cos, sin = _rope_tables(seq_len)
    chunks = tm // REC_CHUNK
    chunk_sel = jnp.asarray(np.repeat(np.eye(chunks, dtype=np.float32), REC_CHUNK, axis=1), BF16)
    tiles_per_seq = seq_len // tm
    tok = lambda i: (i, 0)
    const = lambda i: (0, 0)
    table = lambda i: (i % tiles_per_seq, 0)
    out_shape = [
        jax.ShapeDtypeStruct((n_tok, ATTN_WIDTH), BF16),
        jax.ShapeDtypeStruct((n_tok, KV_WIDTH), BF16),
        jax.ShapeDtypeStruct((n_tok, 2 * KV_WIDTH), BF16),
    ] + [jax.ShapeDtypeStruct((n_tok, REC_WIDTH), F32)] * 3 + [jax.ShapeDtypeStruct((n_tok, REC_WIDTH), BF16)] * 2 + [
        jax.ShapeDtypeStruct((n_tok // REC_CHUNK, V7X_LANES), F32)]
    out_specs = [
        pl.BlockSpec((tm, ATTN_WIDTH), tok),
        pl.BlockSpec((tm, KV_WIDTH), tok),
        pl.BlockSpec((tm, 2 * KV_WIDTH), tok),
    ] + [pl.BlockSpec((tm, REC_WIDTH), tok)] * 5 + [pl.BlockSpec((chunks, V7X_LANES), tok)]
    return pl.pallas_call(
        functools.partial(_mix_in_kernel, layer=layer, sub_rows=MIX_IN_SUB_ROWS),
        grid=(n_tok // tm,),
        in_specs=[
            pl.BlockSpec((tm, d), tok),
            pl.BlockSpec((1, d), const),
            pl.BlockSpec((d, d_in), const),
            pl.BlockSpec((1, HEAD_DIM), const),
            pl.BlockSpec((1, HEAD_DIM), const),
            pl.BlockSpec((tm, HEAD_DIM), table),
            pl.BlockSpec((tm, HEAD_DIM), table),
            pl.BlockSpec(lb_logits.shape, lambda i: (0, 0, 0)),
            pl.BlockSpec(chunk_sel.shape, const),
        ],
        out_specs=out_specs,
        out_shape=out_shape,
        compiler_params=pltpu.CompilerParams(
            dimension_semantics=("parallel",), vmem_limit_bytes=VMEM_LIMIT),
        name="mix_in",
    )(x2d, gain.reshape(1, d), w_in, q_gain.reshape(1, HEAD_DIM), k_gain.reshape(1, HEAD_DIM), cos, sin,
      lb_logits, chunk_sel)


def _attn_kernel(q_ref, k_ref, v_ref, o_ref, q2_scr, s_scr, p_scr, mt_scr, m_scr, acc_scr):
    tq = q_ref.shape[1]
    rows = KV_GROUPS * tq
    tk = s_scr.shape[2]
    n_kv = k_ref.shape[1] // tk

    for g in range(KV_GROUPS):
        q2_scr[g * tq:(g + 1) * tq, :] = q_ref[0, :, g * HEAD_DIM:(g + 1) * HEAD_DIM]
    m_scr[...] = jnp.full_like(m_scr, -jnp.inf)
    acc_scr[...] = jnp.zeros_like(acc_scr)

    def scores(j):
        s = _dot_nt(q2_scr[...], k_ref[0, j * tk:(j + 1) * tk, :])
        s_scr[j % 2] = s
        mt_scr[j % 2] = jnp.broadcast_to(jnp.max(s, axis=-1, keepdims=True), (rows, V7X_LANES))

    def accumulate(j):
        slot = j % 2
        m_prev = m_scr[...]
        m_new = jnp.maximum(m_prev, mt_scr[slot])
        m_scr[...] = m_new
        for r in range(0, rows, ATTN_ROW_CHUNK):
            m_r = m_new[r:r + ATTN_ROW_CHUNK]
            for c in range(0, tk, V7X_LANES):
                s_piece = s_scr[slot, r:r + ATTN_ROW_CHUNK, c:c + V7X_LANES]
                p_scr[slot, r:r + ATTN_ROW_CHUNK, c:c + V7X_LANES] = jnp.exp2(s_piece - m_r).astype(BF16)
        alpha = jnp.exp2(m_prev - m_new)
        pv = _dot(p_scr[slot], v_ref[0, j * tk:(j + 1) * tk, :])
        for half in range(2):
            sl = slice(half * HEAD_DIM, (half + 1) * HEAD_DIM)
            acc_scr[:, sl] = alpha * acc_scr[:, sl] + pv[:, sl]

    scores(0)
    for j in range(n_kv):
        if j + 1 < n_kv:
            scores(j + 1)
        accumulate(j)

    o = (acc_scr[:, 0:HEAD_DIM] / acc_scr[:, HEAD_DIM:2 * HEAD_DIM]).astype(o_ref.dtype)
    for g in range(KV_GROUPS):
        o_ref[0, :, g * HEAD_DIM:(g + 1) * HEAD_DIM] = o[g * tq:(g + 1) * tq]


def _attention(q, k, v_ext):
    bsz, seq_len, _ = q.shape
    tq, tk = ATTN_Q_TOKENS, ATTN_KV_TOKENS
    gw = KV_GROUPS * HEAD_DIM
    rows = KV_GROUPS * tq
    return pl.pallas_call(
        _attn_kernel,
        grid=(bsz, N_KV_HEADS, seq_len // tq),
        in_specs=[
            pl.BlockSpec((1, tq, gw), lambda b, h, qi: (b, qi, h)),
            pl.BlockSpec((1, seq_len, HEAD_DIM), lambda b, h, qi: (b, 0, h)),
            pl.BlockSpec((1, seq_len, 2 * HEAD_DIM), lambda b, h, qi: (b, 0, h)),
        ],
        out_specs=pl.BlockSpec((1, tq, gw), lambda b, h, qi: (b, qi, h)),
        out_shape=jax.ShapeDtypeStruct((bsz, seq_len, ATTN_WIDTH), BF16),
        scratch_shapes=[
            pltpu.VMEM((rows, HEAD_DIM), BF16),
            pltpu.VMEM((2, rows, tk), F32),
            pltpu.VMEM((2, rows, tk), BF16),
            pltpu.VMEM((2, rows, V7X_LANES), F32),
            pltpu.VMEM((rows, V7X_LANES), F32),
            pltpu.VMEM((rows, 2 * HEAD_DIM), F32),
        ],
        compiler_params=pltpu.CompilerParams(
            dimension_semantics=("parallel", "parallel", "arbitrary"),
            vmem_limit_bytes=VMEM_LIMIT),
        name="attn",
    )(q, k, v_ext)


def _rec_constants():
    c = REC_CHUNK
    t = np.arange(c)[:, None]
    u = np.arange(c)[None, :]
    blocks = [(u <= t)]
    masks = []
    for lvl in range(REC_LEVELS):
        h = 1 << lvl
        base_t = t - t % (2 * h)
        mid = base_t + h
        upper = t >= mid
        blocks.append(np.where(upper, (u >= mid) & (u <= t), (u > t) & (u < mid)))
        masks.append((t // (2 * h) == u // (2 * h)) & (t % (2 * h) >= h) & (u % (2 * h) < h))
    blocks.append(u > t)
    masks.append(t == u)
    fwd = np.concatenate([b.astype(np.float32) for b in blocks], axis=0)
    fwd_masks = np.stack([m.astype(np.float32) for m in masks])
    n_rows = fwd.shape[0]
    bwd = fwd.reshape(-1, c, c)[:, ::-1, ::-1].reshape(n_rows, c)
    bwd_masks = fwd_masks[:, ::-1, ::-1]
    total = np.ones((8, c), np.float32)
    mats = np.stack([np.concatenate([fwd, total]), np.concatenate([bwd, total])])
    mats = np.concatenate([mats, mats], axis=-1)
    return mats, np.stack([fwd_masks, bwd_masks])


def _split2(x):
    hi = x.astype(BF16)
    lo = (x - hi.astype(F32)).astype(BF16)
    return jnp.concatenate([hi, lo], axis=0)


def _rec_robust(d, rows, q_ref, v_ref, g_ref, o_ref, mat_ref, mask_ref, state_scr):
    c = REC_CHUNK
    g = g_ref[0, rows, :]
    kk = 1.0 - jnp.exp2(g)
    sums = _dot(mat_ref[d], _split2(g))
    q = q_ref[0, rows, :]
    v = v_ref[0, rows, :].astype(BF16)
    for hh in range(N_REC_HEADS):
        sl = slice(hh * REC_HEAD_DIM, (hh + 1) * REC_HEAD_DIM)
        qh = q[:, sl]
        kh = kk[:, sl]
        vh = v[:, sl]
        scores = mask_ref[d, REC_LEVELS] * _dot_nt(qh.astype(BF16), kh.astype(BF16))
        for lvl in range(REC_LEVELS):
            el = jnp.exp2(sums[(lvl + 1) * c:(lvl + 2) * c, sl])
            scores += mask_ref[d, lvl] * _dot_nt((qh * el).astype(BF16), (kh * el).astype(BF16))
        cum = sums[0:c, sl]
        rem = sums[(REC_LEVELS + 1) * c:(REC_LEVELS + 2) * c, sl]
        tot = sums[(REC_LEVELS + 2) * c:(REC_LEVELS + 2) * c + 1, sl]
        state_t = state_scr[d * N_REC_HEADS + hh]
        inter = _dot_nt((qh * jnp.exp2(cum)).astype(BF16), state_t.astype(BF16))
        o_ref[0, rows, sl] = (inter + _dot(scores.astype(BF16), vh)).astype(o_ref.dtype)
        k_tail = (kh * jnp.exp2(rem)).astype(BF16)
        state_scr[d * N_REC_HEADS + hh] = state_t * jnp.exp2(tot) + _dot_tn(vh, k_tail)


def _rec_fast_prep(d, rows, q_ref, v_ref, g_ref, cmat_ref, tri_ref):
    c = REC_CHUNK
    g = g_ref[0, rows, :]
    kk = 1.0 - jnp.exp2(g)
    cs = _dot(cmat_ref[d], _split2(g))
    cum = cs[0:c]
    tot = cs[c:c + 1]
    half = 0.5 * tot
    e_half = jnp.exp2(half)
    k_s = kk * jnp.exp2(half - cum)
    return dict(
        q_s=(q_ref[0, rows, :] * jnp.exp2(cum - half)).astype(BF16),
        k_s=k_s.astype(BF16),
        k_tail=(k_s * e_half).astype(BF16),
        e_half=e_half, e_tot=jnp.exp2(tot),
        v=v_ref[0, rows, :].astype(BF16), causal=tri_ref[d] > 0.0)


def _rec_fast(prep, rows_of, o_refs, state_scr):
    heads = [(d, hh) for d in range(len(o_refs)) for hh in range(N_REC_HEADS)]
    lanes = lambda hh: slice(hh * REC_HEAD_DIM, (hh + 1) * REC_HEAD_DIM)
    updates = [{(d, hh): _dot_tn(pj[d]["v"][:, lanes(hh)], pj[d]["k_tail"][:, lanes(hh)]) for d, hh in heads}
               for pj in prep]
    state = {(d, hh): state_scr[d * N_REC_HEADS + hh] for d, hh in heads}
    res = []
    for pj, uj in zip(prep, updates):
        res_j = {}
        for d, hh in heads:
            p, sl = pj[d], lanes(hh)
            rhs = jnp.concatenate([(state[d, hh] * p["e_half"][:, sl]).astype(BF16), p["k_s"][:, sl]], axis=0)
            res_j[d, hh] = _dot_nt(p["q_s"][:, sl], rhs)
            state[d, hh] = state[d, hh] * p["e_tot"][:, sl] + uj[d, hh]
        res.append(res_j)
    for d, hh in heads:
        state_scr[d * N_REC_HEADS + hh] = state[d, hh]
    for pj, rows_j, res_j in zip(prep, rows_of, res):
        for d, hh in heads:
            p, sl = pj[d], lanes(hh)
            scores = jnp.where(p["causal"], res_j[d, hh][:, REC_HEAD_DIM:], 0.0)
            o = res_j[d, hh][:, :REC_HEAD_DIM] + _dot(scores.astype(BF16), p["v"][:, sl])
            o_refs[d][0, rows_j[d], sl] = o.astype(o_refs[d].dtype)


def _rec_kernel(flag_ref, qf_ref, qb_ref, vf_ref, vb_ref, gf_ref, gb_ref, mat_ref, mask_ref, cmat_ref, tri_ref,
                of_ref, ob_ref, state_scr):
    b = pl.program_id(0)
    i = pl.program_id(1)
    n_steps = pl.num_programs(1)
    per_step = REC_STEP_CHUNKS

    @pl.when(i == 0)
    def _():
        state_scr[...] = jnp.zeros_like(state_scr)

    fwd_chunk0 = (b * n_steps + i) * per_step
    bwd_chunk0 = (b * n_steps + n_steps - 1 - i) * per_step
    fast = flag_ref[fwd_chunk0 * 2] == 1
    for j in range(per_step):
        if j:
            fast = jnp.logical_and(fast, flag_ref[(fwd_chunk0 + j) * 2] == 1)
        fast = jnp.logical_and(fast, flag_ref[(bwd_chunk0 + j) * 2 + 1] == 1)
    dirs = ((qf_ref, vf_ref, gf_ref, of_ref), (qb_ref, vb_ref, gb_ref, ob_ref))
    chunk_rows = lambda j: slice(j * REC_CHUNK, (j + 1) * REC_CHUNK)
    rows_of = [(chunk_rows(j), chunk_rows(per_step - 1 - j)) for j in range(per_step)]

    @pl.when(fast)
    def _():
        prep = [[_rec_fast_prep(d, rows_of[j][d], q_ref, v_ref, g_ref, cmat_ref, tri_ref)
                 for d, (q_ref, v_ref, g_ref, _) in enumerate(dirs)] for j in range(per_step)]
        _rec_fast(prep, rows_of, (of_ref, ob_ref), state_scr)

    @pl.when(jnp.logical_not(fast))
    def _():
        for j in range(per_step):
            for d, (q_ref, v_ref, g_ref, o_ref) in enumerate(dirs):
                _rec_robust(d, rows_of[j][d], q_ref, v_ref, g_ref, o_ref, mat_ref, mask_ref, state_scr)


def _hgrn2(rq, g_f, g_b, ri, fast_flags):
    bsz, seq_len, width = rq.shape
    c = REC_CHUNK
    n_steps = seq_len // (c * REC_STEP_CHUNKS)
    mats, masks = _rec_constants()
    n_lvl_rows = (REC_LEVELS + 2) * c
    cmat = np.concatenate([mats[:, 0:c], mats[:, n_lvl_rows:]], axis=1)
    tri = masks.sum(axis=1)
    mats, cmat = jnp.asarray(mats, BF16), jnp.asarray(cmat, BF16)
    masks, tri = jnp.asarray(masks, F32), jnp.asarray(tri, F32)
    fwd = lambda b, i, flags: (b, i, 0)
    bwd = lambda b, i, flags: (b, n_steps - 1 - i, 0)
    blk = (1, c * REC_STEP_CHUNKS, width)
    full3 = lambda b, i, flags: (0, 0, 0)
    full4 = lambda b, i, flags: (0, 0, 0, 0)
    grid_spec = pltpu.PrefetchScalarGridSpec(
        num_scalar_prefetch=1,
        grid=(bsz, n_steps),
        in_specs=[
            pl.BlockSpec(blk, fwd), pl.BlockSpec(blk, bwd),
            pl.BlockSpec(blk, fwd), pl.BlockSpec(blk, bwd),
            pl.BlockSpec(blk, fwd), pl.BlockSpec(blk, bwd),
            pl.BlockSpec(mats.shape, full3),
            pl.BlockSpec(masks.shape, full4),
            pl.BlockSpec(cmat.shape, full3),
            pl.BlockSpec(tri.shape, full3),
        ],
        out_specs=[pl.BlockSpec(blk, fwd), pl.BlockSpec(blk, bwd)],
        scratch_shapes=[pltpu.VMEM((2 * N_REC_HEADS, REC_HEAD_DIM, REC_HEAD_DIM), F32)],
    )
    return pl.pallas_call(
        _rec_kernel,
        grid_spec=grid_spec,
        out_shape=[jax.ShapeDtypeStruct((bsz, seq_len, width), BF16)] * 2,
        compiler_params=pltpu.CompilerParams(
            dimension_semantics=("parallel", "arbitrary"), vmem_limit_bytes=VMEM_LIMIT),
        name="hgrn2",
    )(fast_flags, rq, rq, ri, ri, g_f, g_b, mats, masks, cmat, tri)


def kernel(x, ffn1_norm, ffn1_w_in, ffn1_w_out, mix_norm, w_in_mix, attn_q_norm, attn_k_norm,
           attn_out_norm, rec_lb_logits, rec_out_norm, w_out_mix, ffn2_norm, ffn2_w_in,
           ffn2_w_out, final_norm):
    bsz, seq_len, d = x.shape
    depth = ffn1_norm.shape[0]
    n_tok = bsz * seq_len
    h = x.reshape(n_tok, d)
    to3 = lambda a: a.reshape(bsz, seq_len, a.shape[-1])
    to2 = lambda a: a.reshape(n_tok, a.shape[-1])
    for l in range(depth):
        h = _ffn(h, ffn1_norm[l], ffn1_w_in[l].astype(BF16), ffn1_w_out[l].astype(BF16),
                 tokens=FFN_TOKENS, sub_rows=FFN_SUB_ROWS)
        q, k, v_ext, rq, g_f, g_b, ri, rg, chunk_decay = _mix_in(
            h, mix_norm[l], w_in_mix[l].astype(BF16), attn_q_norm[l], attn_k_norm[l],
            rec_lb_logits, l, seq_len)
        attn = _attention(to3(q), to3(k), to3(v_ext))
        fast_flags = (chunk_decay[:, :2] <= REC_FAST_RANGE * LOG2_E).astype(jnp.int32).reshape(-1)
        o_f, o_b = _hgrn2(to3(rq), to3(g_f), to3(g_b), to3(ri), fast_flags)
        mixer_args = (to2(attn), to2(o_f), to2(o_b), rg, attn_out_norm[l], rec_out_norm[l],
                      w_out_mix[l].astype(BF16))
        h = _ffn(h, ffn2_norm[l], ffn2_w_in[l].astype(BF16), ffn2_w_out[l].astype(BF16),
                 tokens=FFN_MIX_TOKENS, sub_rows=FFN_MIX_SUB_ROWS, mixer_args=mixer_args,
                 final_gain=final_norm[l])
    return h.reshape(bsz, seq_len, d)
```

```python
import functools

import jax
import jax.numpy as jnp
import numpy as np
from jax import lax
from jax.experimental import pallas as pl
from jax.experimental.pallas import tpu as pltpu

F32 = jnp.float32
BF16 = jnp.bfloat16

EPS = 1e-6
GRID_W = 64
HEAD_DIM = 128
N_Q_HEADS = 4
N_KV_HEADS = 2
KV_GROUPS = N_Q_HEADS // N_KV_HEADS
ATTN_WIDTH = N_Q_HEADS * HEAD_DIM
KV_WIDTH = N_KV_HEADS * HEAD_DIM
ROPE_THETA = 10000.0
ROPE_AXIS_DIM = HEAD_DIM // 2
REC_HEAD_DIM = 128
N_REC_HEADS = 4
REC_WIDTH = N_REC_HEADS * REC_HEAD_DIM

V7X_LANES = 128

FFN_TOKENS = 1024
FFN_SUB_ROWS = 512
FFN_MIX_TOKENS = 1024
FFN_MIX_SUB_ROWS = 512
FFN_FF_CHUNK = 512
MIX_IN_TOKENS = 512
MIX_IN_SUB_ROWS = 256
ATTN_Q_TOKENS = 512
ATTN_KV_TOKENS = 2048
ATTN_ROW_CHUNK = 128
LOG2_E = 1.4426950408889634
REC_CHUNK = 128
REC_STEP_CHUNKS = 2
REC_LEVELS = 7
REC_STAT_ROWS = 8
REC_FAST_RANGE = 150.0

VMEM_LIMIT = 56 * 1024 * 1024


def _rms(x, gain):
    return x * lax.rsqrt(jnp.mean(x * x, axis=-1, keepdims=True) + EPS) * gain


def _dot(a, b):
    return jnp.dot(a, b, preferred_element_type=F32)


def _dot_nt(a, b):
    return lax.dot_general(a, b, (((1,), (1,)), ((), ())), preferred_element_type=F32)


def _dot_tn(a, b):
    return lax.dot_general(a, b, (((0,), (0,)), ((), ())), preferred_element_type=F32)


def _mixer_out_rows(rows, attn_ref, of_ref, ob_ref, rg_ref, ag_ref, rgain_ref, wmix_ref):
    parts = [_rms(attn_ref[rows, :].astype(F32), ag_ref[...]).astype(BF16)]
    for hh in range(N_REC_HEADS):
        sl = slice(hh * REC_HEAD_DIM, (hh + 1) * REC_HEAD_DIM)
        gate = rg_ref[rows, sl].astype(F32)
        o = of_ref[rows, sl].astype(F32) + ob_ref[rows, sl].astype(F32)
        parts.append((_rms(o, rgain_ref[...]) * (gate * jax.nn.sigmoid(gate))).astype(BF16))
    return _dot(jnp.concatenate(parts, axis=-1), wmix_ref[...])


def _ffn_kernel(*refs, mixer, final_norm, sub_rows, ff_chunks):
    refs = list(refs)
    x_ref = refs.pop(0)
    mix_refs = [refs.pop(0) for _ in range(7)] if mixer else None
    gain_ref, win_ref, wout_ref = refs.pop(0), refs.pop(0), refs.pop(0)
    fgain_ref = refs.pop(0) if final_norm else None
    out_ref, acc_scr = refs
    d_ff = wout_ref.shape[0]

    for sub in range(x_ref.shape[0] // sub_rows):
        rows = slice(sub * sub_rows, (sub + 1) * sub_rows)
        x = x_ref[rows, :]
        res_ref = x_ref
        if mixer:
            x = x + _mixer_out_rows(rows, *mix_refs)
            out_ref[rows, :] = x
            res_ref = out_ref
        h = _rms(x, gain_ref[...]).astype(BF16)
        c0 = 0
        for width in ff_chunks:
            gate = _dot(h, win_ref[:, c0:c0 + width])
            up = _dot(h, win_ref[:, d_ff + c0:d_ff + c0 + width])
            act = ((gate * jax.nn.sigmoid(gate)) * up).astype(BF16)
            contrib = _dot(act, wout_ref[c0:c0 + width, :])
            if c0 == 0:
                acc_scr[sub] = contrib
            else:
                acc_scr[sub] += contrib
            c0 += width
        y = res_ref[rows, :] + 0.5 * acc_scr[sub]
        if final_norm:
            y = _rms(y, fgain_ref[...])
        out_ref[rows, :] = y


def _ffn_chunks(d_ff):
    n_full, rest = divmod(d_ff, FFN_FF_CHUNK)
    assert rest % V7X_LANES == 0
    return (FFN_FF_CHUNK,) * n_full + ((rest,) if rest else ())


def _ffn(x2d, gain, w_in, w_out, *, tokens, sub_rows, mixer_args=None, final_gain=None):
    n_tok, d = x2d.shape
    d_ff = w_out.shape[0]
    tok = lambda i: (i, 0)
    const = lambda i: (0, 0)
    resident = lambda shape: pl.BlockSpec(shape, const, pipeline_mode=pl.Buffered(1))
    in_specs = [pl.BlockSpec((tokens, d), tok)]
    args = [x2d]
    if mixer_args is not None:
        attn, o_f, o_b, rg, attn_gain, rec_gain, w_mix = mixer_args
        in_specs += [pl.BlockSpec((tokens, ATTN_WIDTH), tok)] + [pl.BlockSpec((tokens, REC_WIDTH), tok)] * 3
        in_specs += [pl.BlockSpec((1, ATTN_WIDTH), const), pl.BlockSpec((1, REC_HEAD_DIM), const),
                     resident(w_mix.shape)]
        args += [attn, o_f, o_b, rg, attn_gain.reshape(1, ATTN_WIDTH), rec_gain.reshape(1, REC_HEAD_DIM), w_mix]
    in_specs += [pl.BlockSpec((1, d), const), resident(w_in.shape), resident(w_out.shape)]
    args += [gain.reshape(1, d), w_in, w_out]
    if final_gain is not None:
        in_specs.append(pl.BlockSpec((1, d), const))
        args.append(final_gain.reshape(1, d))
    body = functools.partial(
        _ffn_kernel, mixer=mixer_args is not None, final_norm=final_gain is not None,
        sub_rows=sub_rows, ff_chunks=_ffn_chunks(d_ff))
    return pl.pallas_call(
        body,
        grid=(n_tok // tokens,),
        in_specs=in_specs,
        out_specs=pl.BlockSpec((tokens, d), tok),
        out_shape=jax.ShapeDtypeStruct((n_tok, d), F32),
        scratch_shapes=[pltpu.VMEM((tokens // sub_rows, sub_rows, d), F32)],
        compiler_params=pltpu.CompilerParams(
            dimension_semantics=("parallel",), vmem_limit_bytes=VMEM_LIMIT),
        name="ffn_mix" if mixer_args is not None else "ffn",
    )(*args)


def _rope(x, cos, sin_signed, first_half):
    swapped = jnp.where(first_half, pltpu.roll(x, 96, 1), pltpu.roll(x, 32, 1))
    return x * cos + swapped * sin_signed


def _mix_in_kernel(x_ref, gain_ref, w_ref, qg_ref, kg_ref, cos_ref, sin_ref, lbl_ref, csel_ref,
                   q_ref, k_ref, v_ref, rq_ref, gf_ref, gb_ref, ri_ref, rg_ref, ctot_ref, *, layer, sub_rows):
    col = {}
    c = 0
    for name, width in (("aq", ATTN_WIDTH), ("ak", KV_WIDTH), ("av", KV_WIDTH), ("rq", REC_WIDTH),
                        ("zf", REC_WIDTH), ("zb", REC_WIDTH), ("ri", REC_WIDTH), ("rg", REC_WIDTH)):
        col[name] = (c, width)
        c += width
    lane = lax.broadcasted_iota(jnp.int32, (sub_rows, HEAD_DIM), 1)
    first_half = (lane % (ROPE_AXIS_DIM)) < (ROPE_AXIS_DIM // 2)
    chunk_lane = lax.broadcasted_iota(jnp.int32, ctot_ref.shape, 1)
    ctot = jnp.zeros(ctot_ref.shape, F32)
    gates = []

    for sub in range(x_ref.shape[0] // sub_rows):
        rows = slice(sub * sub_rows, (sub + 1) * sub_rows)
        h = _rms(x_ref[rows, :], gain_ref[...]).astype(BF16)
        cos = cos_ref[rows, :]
        sin = sin_ref[rows, :]

        def proj(name):
            c0, width = col[name]
            return _dot(h, w_ref[:, c0:c0 + width])

        for d, (name, g_ref) in enumerate((("zf", gf_ref), ("zb", gb_ref))):
            logits = lbl_ref[d]
            e = jnp.exp(logits - jnp.max(logits, axis=0, keepdims=True))
            lb = jnp.sum(e[:layer + 1], axis=0, keepdims=True) / jnp.sum(e, axis=0, keepdims=True)
            g = jnp.log2(lb + (1.0 - lb) * jax.nn.sigmoid(proj(name)))
            g_ref[rows, :] = g
            gates.append((d, rows, g.astype(BF16)))

        aq = proj("aq")
        for hh in range(N_Q_HEADS):
            sl = slice(hh * HEAD_DIM, (hh + 1) * HEAD_DIM)
            qh = _rope(_rms(aq[:, sl], qg_ref[...]), cos, sin, first_half) * (HEAD_DIM ** -0.5 * LOG2_E)
            q_ref[rows, sl] = qh.astype(BF16)
        ak = proj("ak")
        for hh in range(N_KV_HEADS):
            sl = slice(hh * HEAD_DIM, (hh + 1) * HEAD_DIM)
            k_ref[rows, sl] = _rope(_rms(ak[:, sl], kg_ref[...]), cos, sin, first_half).astype(BF16)
        av = proj("av").astype(BF16)
        for hh in range(N_KV_HEADS):
            v_ref[rows, 2 * hh * HEAD_DIM:(2 * hh + 1) * HEAD_DIM] = av[:, hh * HEAD_DIM:(hh + 1) * HEAD_DIM]
            v_ref[rows, (2 * hh + 1) * HEAD_DIM:(2 * hh + 2) * HEAD_DIM] = jnp.ones((sub_rows, HEAD_DIM), BF16)
        rq_ref[rows, :] = proj("rq")
        ri_ref[rows, :] = proj("ri").astype(ri_ref.dtype)
        rg_ref[rows, :] = proj("rg").astype(rg_ref.dtype)
    for d, rows, g_bf16 in gates:
        chunk_sum = _dot(csel_ref[:, rows], g_bf16)
        ctot = jnp.maximum(ctot, jnp.where(chunk_lane == d, jnp.max(-chunk_sum, axis=-1, keepdims=True), 0.0))
    ctot_ref[...] = ctot


def _rope_tables(seq_len):
    rows = seq_len // GRID_W
    inv_freq = ROPE_THETA ** (-jnp.arange(0, ROPE_AXIS_DIM, 2, dtype=F32) / ROPE_AXIS_DIM)
    ang_r = jnp.arange(rows, dtype=F32)[:, None] * inv_freq[None, :]
    ang_c = jnp.arange(GRID_W, dtype=F32)[:, None] * inv_freq[None, :]
    half = ROPE_AXIS_DIM // 2

    def expand(row_part, col_part):
        r = jnp.broadcast_to(row_part[:, None, :], (rows, GRID_W, 2 * half))
        c = jnp.broadcast_to(col_part[None, :, :], (rows, GRID_W, 2 * half))
        return jnp.concatenate([r, c], axis=-1).reshape(seq_len, 4 * half)

    cos = expand(jnp.concatenate([jnp.cos(ang_r)] * 2, -1), jnp.concatenate([jnp.cos(ang_c)] * 2, -1))
    sin = expand(jnp.concatenate([-jnp.sin(ang_r), jnp.sin(ang_r)], -1),
                 jnp.concatenate([-jnp.sin(ang_c), jnp.sin(ang_c)], -1))
    return cos, sin


def _mix_in(x2d, gain, w_in, q_gain, k_gain, lb_logits, layer, seq_len):
    n_tok, d = x2d.shape
    d_in = w_in.shape[1]
    tm = MIX_IN_TOKENS
    cos, sin = _rope_tables(seq_len)
    chunks = tm // REC_CHUNK
    assert chunks <= REC_STAT_ROWS
    chunk_sel = np.zeros((REC_STAT_ROWS, tm), np.float32)
    chunk_sel[:chunks] = np.repeat(np.eye(chunks, dtype=np.float32), REC_CHUNK, axis=1)
    chunk_sel = jnp.asarray(chunk_sel, BF16)
    tiles_per_seq = seq_len // tm
    tok = lambda i: (i, 0)
    const = lambda i: (0, 0)
    table = lambda i: (i % tiles_per_seq, 0)
    out_shape = [
        jax.ShapeDtypeStruct((n_tok, ATTN_WIDTH), BF16),
        jax.ShapeDtypeStruct((n_tok, KV_WIDTH), BF16),
        jax.ShapeDtypeStruct((n_tok, 2 * KV_WIDTH), BF16),
    ] + [jax.ShapeDtypeStruct((n_tok, REC_WIDTH), F32)] * 3 + [jax.ShapeDtypeStruct((n_tok, REC_WIDTH), BF16)] * 2 + [
        jax.ShapeDtypeStruct((n_tok // tm * REC_STAT_ROWS, V7X_LANES), F32)]
    out_specs = [
        pl.BlockSpec((tm, ATTN_WIDTH), tok),
        pl.BlockSpec((tm, KV_WIDTH), tok),
        pl.BlockSpec((tm, 2 * KV_WIDTH), tok),
    ] + [pl.BlockSpec((tm, REC_WIDTH), tok)] * 5 + [pl.BlockSpec((REC_STAT_ROWS, V7X_LANES), tok)]
    return pl.pallas_call(
        functools.partial(_mix_in_kernel, layer=layer, sub_rows=MIX_IN_SUB_ROWS),
        grid=(n_tok // tm,),
        in_specs=[
            pl.BlockSpec((tm, d), tok),
            pl.BlockSpec((1, d), const),
            pl.BlockSpec((d, d_in), const),
            pl.BlockSpec((1, HEAD_DIM), const),
            pl.BlockSpec((1, HEAD_DIM), const),
            pl.BlockSpec((tm, HEAD_DIM), table),
            pl.BlockSpec((tm, HEAD_DIM), table),
            pl.BlockSpec(lb_logits.shape, lambda i: (0, 0, 0)),
            pl.BlockSpec(chunk_sel.shape, const),
        ],
        out_specs=out_specs,
        out_shape=out_shape,
        compiler_params=pltpu.CompilerParams(
            dimension_semantics=("parallel",), vmem_limit_bytes=VMEM_LIMIT),
        name="mix_in",
    )(x2d, gain.reshape(1, d), w_in, q_gain.reshape(1, HEAD_DIM), k_gain.reshape(1, HEAD_DIM), cos, sin,
      lb_logits, chunk_sel)


def _attn_kernel(q_ref, k_ref, v_ref, o_ref, q2_scr, s_scr, p_scr, mt_scr, m_scr, acc_scr):
    tq = q_ref.shape[1]
    rows = KV_GROUPS * tq
    tk = s_scr.shape[2]
    n_kv = k_ref.shape[1] // tk

    for g in range(KV_GROUPS):
        q2_scr[g * tq:(g + 1) * tq, :] = q_ref[0, :, g * HEAD_DIM:(g + 1) * HEAD_DIM]
    m_scr[...] = jnp.full_like(m_scr, -jnp.inf)
    acc_scr[...] = jnp.zeros_like(acc_scr)

    def scores(j):
        s = _dot_nt(q2_scr[...], k_ref[0, j * tk:(j + 1) * tk, :])
        s_scr[j % 2] = s
        mt_scr[j % 2] = jnp.broadcast_to(jnp.max(s, axis=-1, keepdims=True), (rows, V7X_LANES))

    def accumulate(j):
        slot = j % 2
        m_prev = m_scr[...]
        m_new = jnp.maximum(m_prev, mt_scr[slot])
        m_scr[...] = m_new
        for r in range(0, rows, ATTN_ROW_CHUNK):
            m_r = m_new[r:r + ATTN_ROW_CHUNK]
            for c in range(0, tk, V7X_LANES):
                s_piece = s_scr[slot, r:r + ATTN_ROW_CHUNK, c:c + V7X_LANES]
                p_scr[slot, r:r + ATTN_ROW_CHUNK, c:c + V7X_LANES] = jnp.exp2(s_piece - m_r).astype(BF16)
        alpha = jnp.exp2(m_prev - m_new)
        pv = _dot(p_scr[slot], v_ref[0, j * tk:(j + 1) * tk, :])
        for half in range(2):
            sl = slice(half * HEAD_DIM, (half + 1) * HEAD_DIM)
            acc_scr[:, sl] = alpha * acc_scr[:, sl] + pv[:, sl]

    scores(0)
    for j in range(n_kv):
        if j + 1 < n_kv:
            scores(j + 1)
        accumulate(j)

    o = (acc_scr[:, 0:HEAD_DIM] / acc_scr[:, HEAD_DIM:2 * HEAD_DIM]).astype(o_ref.dtype)
    for g in range(KV_GROUPS):
        o_ref[0, :, g * HEAD_DIM:(g + 1) * HEAD_DIM] = o[g * tq:(g + 1) * tq]


def _attention(q, k, v_ext):
    bsz, seq_len, _ = q.shape
    tq, tk = ATTN_Q_TOKENS, ATTN_KV_TOKENS
    gw = KV_GROUPS * HEAD_DIM
    rows = KV_GROUPS * tq
    return pl.pallas_call(
        _attn_kernel,
        grid=(bsz, N_KV_HEADS, seq_len // tq),
        in_specs=[
            pl.BlockSpec((1, tq, gw), lambda b, h, qi: (b, qi, h)),
            pl.BlockSpec((1, seq_len, HEAD_DIM), lambda b, h, qi: (b, 0, h)),
            pl.BlockSpec((1, seq_len, 2 * HEAD_DIM), lambda b, h, qi: (b, 0, h)),
        ],
        out_specs=pl.BlockSpec((1, tq, gw), lambda b, h, qi: (b, qi, h)),
        out_shape=jax.ShapeDtypeStruct((bsz, seq_len, ATTN_WIDTH), BF16),
        scratch_shapes=[
            pltpu.VMEM((rows, HEAD_DIM), BF16),
            pltpu.VMEM((2, rows, tk), F32),
            pltpu.VMEM((2, rows, tk), BF16),
            pltpu.VMEM((2, rows, V7X_LANES), F32),
            pltpu.VMEM((rows, V7X_LANES), F32),
            pltpu.VMEM((rows, 2 * HEAD_DIM), F32),
        ],
        compiler_params=pltpu.CompilerParams(
            dimension_semantics=("parallel", "parallel", "arbitrary"),
            vmem_limit_bytes=VMEM_LIMIT),
        name="attn",
    )(q, k, v_ext)


def _rec_constants():
    c = REC_CHUNK
    t = np.arange(c)[:, None]
    u = np.arange(c)[None, :]
    blocks = [(u <= t)]
    masks = []
    for lvl in range(REC_LEVELS):
        h = 1 << lvl
        base_t = t - t % (2 * h)
        mid = base_t + h
        upper = t >= mid
        blocks.append(np.where(upper, (u >= mid) & (u <= t), (u > t) & (u < mid)))
        masks.append((t // (2 * h) == u // (2 * h)) & (t % (2 * h) >= h) & (u % (2 * h) < h))
    blocks.append(u > t)
    masks.append(t == u)
    fwd = np.concatenate([b.astype(np.float32) for b in blocks], axis=0)
    fwd_masks = np.stack([m.astype(np.float32) for m in masks])
    n_rows = fwd.shape[0]
    bwd = fwd.reshape(-1, c, c)[:, ::-1, ::-1].reshape(n_rows, c)
    bwd_masks = fwd_masks[:, ::-1, ::-1]
    total = np.ones((8, c), np.float32)
    mats = np.stack([np.concatenate([fwd, total]), np.concatenate([bwd, total])])
    mats = np.concatenate([mats, mats], axis=-1)
    return mats, np.stack([fwd_masks, bwd_masks])


def _split2(x):
    hi = x.astype(BF16)
    lo = (x - hi.astype(F32)).astype(BF16)
    return jnp.concatenate([hi, lo], axis=0)


def _rec_robust(d, rows, q_ref, v_ref, g_ref, o_ref, mat_ref, mask_ref, state_scr):
    c = REC_CHUNK
    g = g_ref[0, rows, :]
    kk = 1.0 - jnp.exp2(g)
    sums = _dot(mat_ref[d], _split2(g))
    q = q_ref[0, rows, :]
    v = v_ref[0, rows, :].astype(BF16)
    for hh in range(N_REC_HEADS):
        sl = slice(hh * REC_HEAD_DIM, (hh + 1) * REC_HEAD_DIM)
        qh = q[:, sl]
        kh = kk[:, sl]
        vh = v[:, sl]
        scores = mask_ref[d, REC_LEVELS] * _dot_nt(qh.astype(BF16), kh.astype(BF16))
        for lvl in range(REC_LEVELS):
            el = jnp.exp2(sums[(lvl + 1) * c:(lvl + 2) * c, sl])
            scores += mask_ref[d, lvl] * _dot_nt((qh * el).astype(BF16), (kh * el).astype(BF16))
        cum = sums[0:c, sl]
        rem = sums[(REC_LEVELS + 1) * c:(REC_LEVELS + 2) * c, sl]
        tot = sums[(REC_LEVELS + 2) * c:(REC_LEVELS + 2) * c + 1, sl]
        state_t = state_scr[d * N_REC_HEADS + hh]
        inter = _dot_nt((qh * jnp.exp2(cum)).astype(BF16), state_t.astype(BF16))
        o_ref[0, rows, sl] = (inter + _dot(scores.astype(BF16), vh)).astype(o_ref.dtype)
        k_tail = (kh * jnp.exp2(rem)).astype(BF16)
        state_scr[d * N_REC_HEADS + hh] = state_t * jnp.exp2(tot) + _dot_tn(vh, k_tail)


def _rec_fast_prep(d, rows, q_ref, v_ref, g_ref, cmat_ref, tri_ref):
    c = REC_CHUNK
    g = g_ref[0, rows, :]
    kk = 1.0 - jnp.exp2(g)
    cs = _dot(cmat_ref[d], _split2(g))
    cum = cs[0:c]
    tot = cs[c:c + 1]
    half = 0.5 * tot
    e_half = jnp.exp2(half)
    k_s = kk * jnp.exp2(half - cum)
    return dict(
        q_s=(q_ref[0, rows, :] * jnp.exp2(cum - half)).astype(BF16),
        k_s=k_s.astype(BF16),
        k_tail=(k_s * e_half).astype(BF16),
        e_half=e_half, e_tot=jnp.exp2(tot),
        v=v_ref[0, rows, :].astype(BF16), causal=tri_ref[d] > 0.0)


def _rec_fast(prep, rows_of, o_refs, state_scr):
    heads = [(d, hh) for d in range(len(o_refs)) for hh in range(N_REC_HEADS)]
    lanes = lambda hh: slice(hh * REC_HEAD_DIM, (hh + 1) * REC_HEAD_DIM)
    updates = [{(d, hh): _dot_tn(pj[d]["v"][:, lanes(hh)], pj[d]["k_tail"][:, lanes(hh)]) for d, hh in heads}
               for pj in prep]
    state = {(d, hh): state_scr[d * N_REC_HEADS + hh] for d, hh in heads}
    res = []
    for pj, uj in zip(prep, updates):
        res_j = {}
        for d, hh in heads:
            p, sl = pj[d], lanes(hh)
            rhs = jnp.concatenate([(state[d, hh] * p["e_half"][:, sl]).astype(BF16), p["k_s"][:, sl]], axis=0)
            res_j[d, hh] = _dot_nt(p["q_s"][:, sl], rhs)
            state[d, hh] = state[d, hh] * p["e_tot"][:, sl] + uj[d, hh]
        res.append(res_j)
    for d, hh in heads:
        state_scr[d * N_REC_HEADS + hh] = state[d, hh]
    for pj, rows_j, res_j in zip(prep, rows_of, res):
        for d, hh in heads:
            p, sl = pj[d], lanes(hh)
            scores = jnp.where(p["causal"], res_j[d, hh][:, REC_HEAD_DIM:], 0.0)
            o = res_j[d, hh][:, :REC_HEAD_DIM] + _dot(scores.astype(BF16), p["v"][:, sl])
            o_refs[d][0, rows_j[d], sl] = o.astype(o_refs[d].dtype)


def _rec_kernel(flag_ref, qf_ref, qb_ref, vf_ref, vb_ref, gf_ref, gb_ref, mat_ref, mask_ref, cmat_ref, tri_ref,
                of_ref, ob_ref, state_scr):
    b = pl.program_id(0)
    i = pl.program_id(1)
    n_steps = pl.num_programs(1)
    per_step = REC_STEP_CHUNKS

    @pl.when(i == 0)
    def _():
        state_scr[...] = jnp.zeros_like(state_scr)

    fwd_chunk0 = (b * n_steps + i) * per_step
    bwd_chunk0 = (b * n_steps + n_steps - 1 - i) * per_step
    fast = flag_ref[fwd_chunk0 * 2] == 1
    for j in range(per_step):
        if j:
            fast = jnp.logical_and(fast, flag_ref[(fwd_chunk0 + j) * 2] == 1)
        fast = jnp.logical_and(fast, flag_ref[(bwd_chunk0 + j) * 2 + 1] == 1)
    dirs = ((qf_ref, vf_ref, gf_ref, of_ref), (qb_ref, vb_ref, gb_ref, ob_ref))
    chunk_rows = lambda j: slice(j * REC_CHUNK, (j + 1) * REC_CHUNK)
    rows_of = [(chunk_rows(j), chunk_rows(per_step - 1 - j)) for j in range(per_step)]

    @pl.when(fast)
    def _():
        prep = [[_rec_fast_prep(d, rows_of[j][d], q_ref, v_ref, g_ref, cmat_ref, tri_ref)
                 for d, (q_ref, v_ref, g_ref, _) in enumerate(dirs)] for j in range(per_step)]
        _rec_fast(prep, rows_of, (of_ref, ob_ref), state_scr)

    @pl.when(jnp.logical_not(fast))
    def _():
        for j in range(per_step):
            for d, (q_ref, v_ref, g_ref, o_ref) in enumerate(dirs):
                _rec_robust(d, rows_of[j][d], q_ref, v_ref, g_ref, o_ref, mat_ref, mask_ref, state_scr)


def _hgrn2(rq, g_f, g_b, ri, fast_flags):
    bsz, seq_len, width = rq.shape
    c = REC_CHUNK
    n_steps = seq_len // (c * REC_STEP_CHUNKS)
    mats, masks = _rec_constants()
    n_lvl_rows = (REC_LEVELS + 2) * c
    cmat = np.concatenate([mats[:, 0:c], mats[:, n_lvl_rows:]], axis=1)
    tri = masks.sum(axis=1)
    mats, cmat = jnp.asarray(mats, BF16), jnp.asarray(cmat, BF16)
    masks, tri = jnp.asarray(masks, F32), jnp.asarray(tri, F32)
    fwd = lambda b, i, flags: (b, i, 0)
    bwd = lambda b, i, flags: (b, n_steps - 1 - i, 0)
    blk = (1, c * REC_STEP_CHUNKS, width)
    full3 = lambda b, i, flags: (0, 0, 0)
    full4 = lambda b, i, flags: (0, 0, 0, 0)
    grid_spec = pltpu.PrefetchScalarGridSpec(
        num_scalar_prefetch=1,
        grid=(bsz, n_steps),
        in_specs=[
            pl.BlockSpec(blk, fwd), pl.BlockSpec(blk, bwd),
            pl.BlockSpec(blk, fwd), pl.BlockSpec(blk, bwd),
            pl.BlockSpec(blk, fwd), pl.BlockSpec(blk, bwd),
            pl.BlockSpec(mats.shape, full3),
            pl.BlockSpec(masks.shape, full4),
            pl.BlockSpec(cmat.shape, full3),
            pl.BlockSpec(tri.shape, full3),
        ],
        out_specs=[pl.BlockSpec(blk, fwd), pl.BlockSpec(blk, bwd)],
        scratch_shapes=[pltpu.VMEM((2 * N_REC_HEADS, REC_HEAD_DIM, REC_HEAD_DIM), F32)],
    )
    return pl.pallas_call(
        _rec_kernel,
        grid_spec=grid_spec,
        out_shape=[jax.ShapeDtypeStruct((bsz, seq_len, width), BF16)] * 2,
        compiler_params=pltpu.CompilerParams(
            dimension_semantics=("parallel", "arbitrary"), vmem_limit_bytes=VMEM_LIMIT),
        name="hgrn2",
    )(fast_flags, rq, rq, ri, ri, g_f, g_b, mats, masks, cmat, tri)


def kernel(x, ffn1_norm, ffn1_w_in, ffn1_w_out, mix_norm, w_in_mix, attn_q_norm, attn_k_norm,
           attn_out_norm, rec_lb_logits, rec_out_norm, w_out_mix, ffn2_norm, ffn2_w_in,
           ffn2_w_out, final_norm):
    bsz, seq_len, d = x.shape
    depth = ffn1_norm.shape[0]
    n_tok = bsz * seq_len
    h = x.reshape(n_tok, d)
    to3 = lambda a: a.reshape(bsz, seq_len, a.shape[-1])
    to2 = lambda a: a.reshape(n_tok, a.shape[-1])
    for l in range(depth):
        h = _ffn(h, ffn1_norm[l], ffn1_w_in[l].astype(BF16), ffn1_w_out[l].astype(BF16),
                 tokens=FFN_TOKENS, sub_rows=FFN_SUB_ROWS)
        q, k, v_ext, rq, g_f, g_b, ri, rg, chunk_decay = _mix_in(
            h, mix_norm[l], w_in_mix[l].astype(BF16), attn_q_norm[l], attn_k_norm[l],
            rec_lb_logits, l, seq_len)
        attn = _attention(to3(q), to3(k), to3(v_ext))
        decay = chunk_decay.reshape(-1, REC_STAT_ROWS, V7X_LANES)[:, :MIX_IN_TOKENS // REC_CHUNK, :2]
        fast_flags = (decay <= REC_FAST_RANGE * LOG2_E).astype(jnp.int32).reshape(-1)
        o_f, o_b = _hgrn2(to3(rq), to3(g_f), to3(g_b), to3(ri), fast_flags)
        mixer_args = (to2(attn), to2(o_f), to2(o_b), rg, attn_out_norm[l], rec_out_norm[l],
                      w_out_mix[l].astype(BF16))
        h = _ffn(h, ffn2_norm[l], ffn2_w_in[l].astype(BF16), ffn2_w_out[l].astype(BF16),
                 tokens=FFN_MIX_TOKENS, sub_rows=FFN_MIX_SUB_ROWS, mixer_args=mixer_args,
                 final_gain=final_norm[l])
    return h.reshape(bsz, seq_len, d)
```

```python
import functools

import jax
import jax.numpy as jnp
import numpy as np
from jax import lax
from jax.experimental import pallas as pl
from jax.experimental.pallas import tpu as pltpu

F32 = jnp.float32
BF16 = jnp.bfloat16

EPS = 1e-6
GRID_W = 64
HEAD_DIM = 128
N_Q_HEADS = 4
N_KV_HEADS = 2
KV_GROUPS = N_Q_HEADS // N_KV_HEADS
ATTN_WIDTH = N_Q_HEADS * HEAD_DIM
KV_WIDTH = N_KV_HEADS * HEAD_DIM
ROPE_THETA = 10000.0
ROPE_AXIS_DIM = HEAD_DIM // 2
REC_HEAD_DIM = 128
N_REC_HEADS = 4
REC_WIDTH = N_REC_HEADS * REC_HEAD_DIM

V7X_LANES = 128

FFN_TOKENS = 1024
FFN_SUB_ROWS = 512
FFN_MIX_TOKENS = 1024
FFN_MIX_SUB_ROWS = 512
FFN_FF_CHUNK = 512
MIX_IN_TOKENS = 1024
MIX_IN_SUB_ROWS = 512
ATTN_Q_TOKENS = 512
ATTN_KV_TOKENS = 2048
ATTN_ROW_CHUNK = 128
LOG2_E = 1.4426950408889634
REC_CHUNK = 128
REC_STEP_CHUNKS = 4
REC_LEVELS = 7
REC_STAT_ROWS = 8
REC_FAST_RANGE = 150.0

VMEM_LIMIT = 56 * 1024 * 1024


def _rms(x, gain):
    return x * lax.rsqrt(jnp.mean(x * x, axis=-1, keepdims=True) + EPS) * gain


def _dot(a, b):
    return jnp.dot(a, b, preferred_element_type=F32)


def _dot_nt(a, b):
    return lax.dot_general(a, b, (((1,), (1,)), ((), ())), preferred_element_type=F32)


def _dot_tn(a, b):
    return lax.dot_general(a, b, (((0,), (0,)), ((), ())), preferred_element_type=F32)


def _mixer_out_rows(rows, attn_ref, of_ref, ob_ref, rg_ref, ag_ref, rgain_ref, wmix_ref):
    parts = [_rms(attn_ref[rows, :].astype(F32), ag_ref[...]).astype(BF16)]
    for hh in range(N_REC_HEADS):
        sl = slice(hh * REC_HEAD_DIM, (hh + 1) * REC_HEAD_DIM)
        gate = rg_ref[rows, sl].astype(F32)
        o = of_ref[rows, sl].astype(F32) + ob_ref[rows, sl].astype(F32)
        parts.append((_rms(o, rgain_ref[...]) * (gate * jax.nn.sigmoid(gate))).astype(BF16))
    return _dot(jnp.concatenate(parts, axis=-1), wmix_ref[...])


def _ffn_kernel(*refs, mixer, final_norm, sub_rows, ff_chunks):
    refs = list(refs)
    x_ref = refs.pop(0)
    mix_refs = [refs.pop(0) for _ in range(7)] if mixer else None
    gain_ref, win_ref, wout_ref = refs.pop(0), refs.pop(0), refs.pop(0)
    fgain_ref = refs.pop(0) if final_norm else None
    out_ref, h_scr, acc_scr = refs
    d_ff = wout_ref.shape[0]
    n_sub = x_ref.shape[0] // sub_rows
    res_ref = out_ref if mixer else x_ref
    sub_slice = lambda sub: slice(sub * sub_rows, (sub + 1) * sub_rows)

    def prologue(sub):
        rows = sub_slice(sub)
        x = x_ref[rows, :]
        if mixer:
            x = x + _mixer_out_rows(rows, *mix_refs)
            out_ref[rows, :] = x
        h_scr[sub] = _rms(x, gain_ref[...]).astype(BF16)

    def epilogue(sub):
        rows = sub_slice(sub)
        y = res_ref[rows, :] + 0.5 * acc_scr[sub]
        if final_norm:
            y = _rms(y, fgain_ref[...])
        out_ref[rows, :] = y

    prologue(0)
    for sub in range(n_sub):
        h = h_scr[sub]
        c0 = 0
        for width in ff_chunks:
            gate = _dot(h, win_ref[:, c0:c0 + width])
            up = _dot(h, win_ref[:, d_ff + c0:d_ff + c0 + width])
            act = ((gate * jax.nn.sigmoid(gate)) * up).astype(BF16)
            contrib = _dot(act, wout_ref[c0:c0 + width, :])
            if c0 == 0:
                acc_scr[sub] = contrib
                if sub + 1 < n_sub:
                    prologue(sub + 1)
                if sub >= 1:
                    epilogue(sub - 1)
            else:
                acc_scr[sub] += contrib
            c0 += width
    epilogue(n_sub - 1)


def _ffn_chunks(d_ff):
    n_full, rest = divmod(d_ff, FFN_FF_CHUNK)
    assert rest % V7X_LANES == 0
    return (FFN_FF_CHUNK,) * n_full + ((rest,) if rest else ())


def _ffn(x2d, gain, w_in, w_out, *, tokens, sub_rows, mixer_args=None, final_gain=None):
    n_tok, d = x2d.shape
    d_ff = w_out.shape[0]
    tok = lambda i: (i, 0)
    const = lambda i: (0, 0)
    resident = lambda shape: pl.BlockSpec(shape, const, pipeline_mode=pl.Buffered(1))
    in_specs = [pl.BlockSpec((tokens, d), tok)]
    args = [x2d]
    if mixer_args is not None:
        attn, o_f, o_b, rg, attn_gain, rec_gain, w_mix = mixer_args
        in_specs += [pl.BlockSpec((tokens, ATTN_WIDTH), tok)] + [pl.BlockSpec((tokens, REC_WIDTH), tok)] * 3
        in_specs += [pl.BlockSpec((1, ATTN_WIDTH), const), pl.BlockSpec((1, REC_HEAD_DIM), const),
                     resident(w_mix.shape)]
        args += [attn, o_f, o_b, rg, attn_gain.reshape(1, ATTN_WIDTH), rec_gain.reshape(1, REC_HEAD_DIM), w_mix]
    in_specs += [pl.BlockSpec((1, d), const), resident(w_in.shape), resident(w_out.shape)]
    args += [gain.reshape(1, d), w_in, w_out]
    if final_gain is not None:
        in_specs.append(pl.BlockSpec((1, d), const))
        args.append(final_gain.reshape(1, d))
    body = functools.partial(
        _ffn_kernel, mixer=mixer_args is not None, final_norm=final_gain is not None,
        sub_rows=sub_rows, ff_chunks=_ffn_chunks(d_ff))
    return pl.pallas_call(
        body,
        grid=(n_tok // tokens,),
        in_specs=in_specs,
        out_specs=pl.BlockSpec((tokens, d), tok),
        out_shape=jax.ShapeDtypeStruct((n_tok, d), F32),
        scratch_shapes=[pltpu.VMEM((tokens // sub_rows, sub_rows, d), BF16),
                        pltpu.VMEM((tokens // sub_rows, sub_rows, d), F32)],
        compiler_params=pltpu.CompilerParams(
            dimension_semantics=("parallel",), vmem_limit_bytes=VMEM_LIMIT),
        name="ffn_mix" if mixer_args is not None else "ffn",
    )(*args)


def _rope(x, cos, sin_signed, first_half):
    swapped = jnp.where(first_half, pltpu.roll(x, 96, 1), pltpu.roll(x, 32, 1))
    return x * cos + swapped * sin_signed


def _mix_in_kernel(x_ref, gain_ref, w_ref, qg_ref, kg_ref, cos_ref, sin_ref, lbl_ref, csel_ref,
                   q_ref, k_ref, v_ref, rq_ref, gf_ref, gb_ref, ri_ref, rg_ref, ctot_ref, *, layer, sub_rows):
    col = {}
    c = 0
    for name, width in (("aq", ATTN_WIDTH), ("ak", KV_WIDTH), ("av", KV_WIDTH), ("rq", REC_WIDTH),
                        ("zf", REC_WIDTH), ("zb", REC_WIDTH), ("ri", REC_WIDTH), ("rg", REC_WIDTH)):
        col[name] = (c, width)
        c += width
    lane = lax.broadcasted_iota(jnp.int32, (sub_rows, HEAD_DIM), 1)
    first_half = (lane % (ROPE_AXIS_DIM)) < (ROPE_AXIS_DIM // 2)
    chunk_lane = lax.broadcasted_iota(jnp.int32, ctot_ref.shape, 1)
    ctot = jnp.zeros(ctot_ref.shape, F32)
    gates = []

    for sub in range(x_ref.shape[0] // sub_rows):
        rows = slice(sub * sub_rows, (sub + 1) * sub_rows)
        h = _rms(x_ref[rows, :], gain_ref[...]).astype(BF16)
        cos = cos_ref[rows, :]
        sin = sin_ref[rows, :]

        def proj(name):
            c0, width = col[name]
            return _dot(h, w_ref[:, c0:c0 + width])

        for d, (name, g_ref) in enumerate((("zf", gf_ref), ("zb", gb_ref))):
            logits = lbl_ref[d]
            e = jnp.exp(logits - jnp.max(logits, axis=0, keepdims=True))
            lb = jnp.sum(e[:layer + 1], axis=0, keepdims=True) / jnp.sum(e, axis=0, keepdims=True)
            g = jnp.log2(lb + (1.0 - lb) * jax.nn.sigmoid(proj(name)))
            g_ref[rows, :] = g
            gates.append((d, rows, g.astype(BF16)))

        aq = proj("aq")
        for hh in range(N_Q_HEADS):
            sl = slice(hh * HEAD_DIM, (hh + 1) * HEAD_DIM)
            qh = _rope(_rms(aq[:, sl], qg_ref[...]), cos, sin, first_half) * (HEAD_DIM ** -0.5 * LOG2_E)
            q_ref[rows, sl] = qh.astype(BF16)
        ak = proj("ak")
        for hh in range(N_KV_HEADS):
            sl = slice(hh * HEAD_DIM, (hh + 1) * HEAD_DIM)
            k_ref[rows, sl] = _rope(_rms(ak[:, sl], kg_ref[...]), cos, sin, first_half).astype(BF16)
        av = proj("av").astype(BF16)
        for hh in range(N_KV_HEADS):
            v_ref[rows, 2 * hh * HEAD_DIM:(2 * hh + 1) * HEAD_DIM] = av[:, hh * HEAD_DIM:(hh + 1) * HEAD_DIM]
            v_ref[rows, (2 * hh + 1) * HEAD_DIM:(2 * hh + 2) * HEAD_DIM] = jnp.ones((sub_rows, HEAD_DIM), BF16)
        rq_ref[rows, :] = proj("rq")
        ri_ref[rows, :] = proj("ri").astype(ri_ref.dtype)
        rg_ref[rows, :] = proj("rg").astype(rg_ref.dtype)
    for d, rows, g_bf16 in gates:
        chunk_sum = _dot(csel_ref[:, rows], g_bf16)
        ctot = jnp.maximum(ctot, jnp.where(chunk_lane == d, jnp.max(-chunk_sum, axis=-1, keepdims=True), 0.0))
    ctot_ref[...] = ctot


def _rope_tables(seq_len):
    rows = seq_len // GRID_W
    inv_freq = ROPE_THETA ** (-jnp.arange(0, ROPE_AXIS_DIM, 2, dtype=F32) / ROPE_AXIS_DIM)
    ang_r = jnp.arange(rows, dtype=F32)[:, None] * inv_freq[None, :]
    ang_c = jnp.arange(GRID_W, dtype=F32)[:, None] * inv_freq[None, :]
    half = ROPE_AXIS_DIM // 2

    def expand(row_part, col_part):
        r = jnp.broadcast_to(row_part[:, None, :], (rows, GRID_W, 2 * half))
        c = jnp.broadcast_to(col_part[None, :, :], (rows, GRID_W, 2 * half))
        return jnp.concatenate([r, c], axis=-1).reshape(seq_len, 4 * half)

    cos = expand(jnp.concatenate([jnp.cos(ang_r)] * 2, -1), jnp.concatenate([jnp.cos(ang_c)] * 2, -1))
    sin = expand(jnp.concatenate([-jnp.sin(ang_r), jnp.sin(ang_r)], -1),
                 jnp.concatenate([-jnp.sin(ang_c), jnp.sin(ang_c)], -1))
    return cos, sin


def _mix_in(x2d, gain, w_in, q_gain, k_gain, lb_logits, layer, seq_len):
    n_tok, d = x2d.shape
    d_in = w_in.shape[1]
    tm = MIX_IN_TOKENS
    cos, sin = _rope_tables(seq_len)
    chunks = tm // REC_CHUNK
    assert chunks <= REC_STAT_ROWS
    chunk_sel = np.zeros((REC_STAT_ROWS, tm), np.float32)
    chunk_sel[:chunks] = np.repeat(np.eye(chunks, dtype=np.float32), REC_CHUNK, axis=1)
    chunk_sel = jnp.asarray(chunk_sel, BF16)
    tiles_per_seq = seq_len // tm
    tok = lambda i: (i, 0)
    const = lambda i: (0, 0)
    table = lambda i: (i % tiles_per_seq, 0)
    out_shape = [
        jax.ShapeDtypeStruct((n_tok, ATTN_WIDTH), BF16),
        jax.ShapeDtypeStruct((n_tok, KV_WIDTH), BF16),
        jax.ShapeDtypeStruct((n_tok, 2 * KV_WIDTH), BF16),
    ] + [jax.ShapeDtypeStruct((n_tok, REC_WIDTH), F32)] * 3 + [jax.ShapeDtypeStruct((n_tok, REC_WIDTH), BF16)] * 2 + [
        jax.ShapeDtypeStruct((n_tok // tm * REC_STAT_ROWS, V7X_LANES), F32)]
    out_specs = [
        pl.BlockSpec((tm, ATTN_WIDTH), tok),
        pl.BlockSpec((tm, KV_WIDTH), tok),
        pl.BlockSpec((tm, 2 * KV_WIDTH), tok),
    ] + [pl.BlockSpec((tm, REC_WIDTH), tok)] * 5 + [pl.BlockSpec((REC_STAT_ROWS, V7X_LANES), tok)]
    return pl.pallas_call(
        functools.partial(_mix_in_kernel, layer=layer, sub_rows=MIX_IN_SUB_ROWS),
        grid=(n_tok // tm,),
        in_specs=[
            pl.BlockSpec((tm, d), tok),
            pl.BlockSpec((1, d), const),
            pl.BlockSpec((d, d_in), const),
            pl.BlockSpec((1, HEAD_DIM), const),
            pl.BlockSpec((1, HEAD_DIM), const),
            pl.BlockSpec((tm, HEAD_DIM), table),
            pl.BlockSpec((tm, HEAD_DIM), table),
            pl.BlockSpec(lb_logits.shape, lambda i: (0, 0, 0)),
            pl.BlockSpec(chunk_sel.shape, const),
        ],
        out_specs=out_specs,
        out_shape=out_shape,
        compiler_params=pltpu.CompilerParams(
            dimension_semantics=("parallel",), vmem_limit_bytes=VMEM_LIMIT),
        name="mix_in",
    )(x2d, gain.reshape(1, d), w_in, q_gain.reshape(1, HEAD_DIM), k_gain.reshape(1, HEAD_DIM), cos, sin,
      lb_logits, chunk_sel)


def _attn_kernel(q_ref, k_ref, v_ref, o_ref, q2_scr, s_scr, p_scr, mt_scr, m_scr, acc_scr):
    tq = q_ref.shape[1]
    rows = KV_GROUPS * tq
    tk = s_scr.shape[2]
    n_kv = k_ref.shape[1] // tk

    for g in range(KV_GROUPS):
        q2_scr[g * tq:(g + 1) * tq, :] = q_ref[0, :, g * HEAD_DIM:(g + 1) * HEAD_DIM]
    m_scr[...] = jnp.full_like(m_scr, -jnp.inf)
    acc_scr[...] = jnp.zeros_like(acc_scr)

    def scores(j):
        s = _dot_nt(q2_scr[...], k_ref[0, j * tk:(j + 1) * tk, :])
        s_scr[j % 2] = s
        mt_scr[j % 2] = jnp.broadcast_to(jnp.max(s, axis=-1, keepdims=True), (rows, V7X_LANES))

    def accumulate(j):
        slot = j % 2
        m_prev = m_scr[...]
        m_new = jnp.maximum(m_prev, mt_scr[slot])
        m_scr[...] = m_new
        for r in range(0, rows, ATTN_ROW_CHUNK):
            m_r = m_new[r:r + ATTN_ROW_CHUNK]
            for c in range(0, tk, V7X_LANES):
                s_piece = s_scr[slot, r:r + ATTN_ROW_CHUNK, c:c + V7X_LANES]
                p_scr[slot, r:r + ATTN_ROW_CHUNK, c:c + V7X_LANES] = jnp.exp2(s_piece - m_r).astype(BF16)
        alpha = jnp.exp2(m_prev - m_new)
        pv = _dot(p_scr[slot], v_ref[0, j * tk:(j + 1) * tk, :])
        for half in range(2):
            sl = slice(half * HEAD_DIM, (half + 1) * HEAD_DIM)
            acc_scr[:, sl] = alpha * acc_scr[:, sl] + pv[:, sl]

    scores(0)
    for j in range(n_kv):
        if j + 1 < n_kv:
            scores(j + 1)
        accumulate(j)

    o = (acc_scr[:, 0:HEAD_DIM] / acc_scr[:, HEAD_DIM:2 * HEAD_DIM]).astype(o_ref.dtype)
    for g in range(KV_GROUPS):
        o_ref[0, :, g * HEAD_DIM:(g + 1) * HEAD_DIM] = o[g * tq:(g + 1) * tq]


def _attention(q, k, v_ext):
    bsz, seq_len, _ = q.shape
    tq, tk = ATTN_Q_TOKENS, ATTN_KV_TOKENS
    gw = KV_GROUPS * HEAD_DIM
    rows = KV_GROUPS * tq
    return pl.pallas_call(
        _attn_kernel,
        grid=(bsz, N_KV_HEADS, seq_len // tq),
        in_specs=[
            pl.BlockSpec((1, tq, gw), lambda b, h, qi: (b, qi, h)),
            pl.BlockSpec((1, seq_len, HEAD_DIM), lambda b, h, qi: (b, 0, h)),
            pl.BlockSpec((1, seq_len, 2 * HEAD_DIM), lambda b, h, qi: (b, 0, h)),
        ],
        out_specs=pl.BlockSpec((1, tq, gw), lambda b, h, qi: (b, qi, h)),
        out_shape=jax.ShapeDtypeStruct((bsz, seq_len, ATTN_WIDTH), BF16),
        scratch_shapes=[
            pltpu.VMEM((rows, HEAD_DIM), BF16),
            pltpu.VMEM((2, rows, tk), F32),
            pltpu.VMEM((2, rows, tk), BF16),
            pltpu.VMEM((2, rows, V7X_LANES), F32),
            pltpu.VMEM((rows, V7X_LANES), F32),
            pltpu.VMEM((rows, 2 * HEAD_DIM), F32),
        ],
        compiler_params=pltpu.CompilerParams(
            dimension_semantics=("parallel", "parallel", "arbitrary"),
            vmem_limit_bytes=VMEM_LIMIT),
        name="attn",
    )(q, k, v_ext)


def _rec_constants():
    c = REC_CHUNK
    t = np.arange(c)[:, None]
    u = np.arange(c)[None, :]
    blocks = [(u <= t)]
    masks = []
    for lvl in range(REC_LEVELS):
        h = 1 << lvl
        base_t = t - t % (2 * h)
        mid = base_t + h
        upper = t >= mid
        blocks.append(np.where(upper, (u >= mid) & (u <= t), (u > t) & (u < mid)))
        masks.append((t // (2 * h) == u // (2 * h)) & (t % (2 * h) >= h) & (u % (2 * h) < h))
    blocks.append(u > t)
    masks.append(t == u)
    fwd = np.concatenate([b.astype(np.float32) for b in blocks], axis=0)
    fwd_masks = np.stack([m.astype(np.float32) for m in masks])
    n_rows = fwd.shape[0]
    bwd = fwd.reshape(-1, c, c)[:, ::-1, ::-1].reshape(n_rows, c)
    bwd_masks = fwd_masks[:, ::-1, ::-1]
    total = np.ones((8, c), np.float32)
    mats = np.stack([np.concatenate([fwd, total]), np.concatenate([bwd, total])])
    mats = np.concatenate([mats, mats], axis=-1)
    return mats, np.stack([fwd_masks, bwd_masks])


def _split2(x):
    hi = x.astype(BF16)
    lo = (x - hi.astype(F32)).astype(BF16)
    return jnp.concatenate([hi, lo], axis=0)


def _rec_robust(d, rows, q_ref, v_ref, g_ref, o_ref, mat_ref, mask_ref, state_scr):
    c = REC_CHUNK
    g = g_ref[0, rows, :]
    kk = 1.0 - jnp.exp2(g)
    sums = _dot(mat_ref[d], _split2(g))
    q = q_ref[0, rows, :]
    v = v_ref[0, rows, :].astype(BF16)
    for hh in range(N_REC_HEADS):
        sl = slice(hh * REC_HEAD_DIM, (hh + 1) * REC_HEAD_DIM)
        qh = q[:, sl]
        kh = kk[:, sl]
        vh = v[:, sl]
        scores = mask_ref[d, REC_LEVELS] * _dot_nt(qh.astype(BF16), kh.astype(BF16))
        for lvl in range(REC_LEVELS):
            el = jnp.exp2(sums[(lvl + 1) * c:(lvl + 2) * c, sl])
            scores += mask_ref[d, lvl] * _dot_nt((qh * el).astype(BF16), (kh * el).astype(BF16))
        cum = sums[0:c, sl]
        rem = sums[(REC_LEVELS + 1) * c:(REC_LEVELS + 2) * c, sl]
        tot = sums[(REC_LEVELS + 2) * c:(REC_LEVELS + 2) * c + 1, sl]
        state_t = state_scr[d * N_REC_HEADS + hh]
        inter = _dot_nt((qh * jnp.exp2(cum)).astype(BF16), state_t.astype(BF16))
        o_ref[0, rows, sl] = (inter + _dot(scores.astype(BF16), vh)).astype(o_ref.dtype)
        k_tail = (kh * jnp.exp2(rem)).astype(BF16)
        state_scr[d * N_REC_HEADS + hh] = state_t * jnp.exp2(tot) + _dot_tn(vh, k_tail)


def _rec_fast_prep(d, rows, q_ref, v_ref, g_ref, cmat_ref, tri_ref):
    c = REC_CHUNK
    g = g_ref[0, rows, :]
    kk = 1.0 - jnp.exp2(g)
    cs = _dot(cmat_ref[d], _split2(g))
    cum = cs[0:c]
    tot = cs[c:c + 1]
    half = 0.5 * tot
    e_half = jnp.exp2(half)
    k_s = kk * jnp.exp2(half - cum)
    return dict(
        q_s=(q_ref[0, rows, :] * jnp.exp2(cum - half)).astype(BF16),
        k_s=k_s.astype(BF16),
        k_tail=(k_s * e_half).astype(BF16),
        e_half=e_half, e_tot=jnp.exp2(tot),
        v=v_ref[0, rows, :].astype(BF16), causal=tri_ref[d] > 0.0)


def _rec_fast(prep, rows_of, o_refs, state_scr):
    heads = [(d, hh) for d in range(len(o_refs)) for hh in range(N_REC_HEADS)]
    lanes = lambda hh: slice(hh * REC_HEAD_DIM, (hh + 1) * REC_HEAD_DIM)
    updates = [{(d, hh): _dot_tn(pj[d]["v"][:, lanes(hh)], pj[d]["k_tail"][:, lanes(hh)]) for d, hh in heads}
               for pj in prep]
    state = {(d, hh): state_scr[d * N_REC_HEADS + hh] for d, hh in heads}
    res = []
    for pj, uj in zip(prep, updates):
        res_j = {}
        for d, hh in heads:
            p, sl = pj[d], lanes(hh)
            rhs = jnp.concatenate([(state[d, hh] * p["e_half"][:, sl]).astype(BF16), p["k_s"][:, sl]], axis=0)
            res_j[d, hh] = _dot_nt(p["q_s"][:, sl], rhs)
            state[d, hh] = state[d, hh] * p["e_tot"][:, sl] + uj[d, hh]
        res.append(res_j)
    for d, hh in heads:
        state_scr[d * N_REC_HEADS + hh] = state[d, hh]
    for pj, rows_j, res_j in zip(prep, rows_of, res):
        for d, hh in heads:
            p, sl = pj[d], lanes(hh)
            scores = jnp.where(p["causal"], res_j[d, hh][:, REC_HEAD_DIM:], 0.0)
            o = res_j[d, hh][:, :REC_HEAD_DIM] + _dot(scores.astype(BF16), p["v"][:, sl])
            o_refs[d][0, rows_j[d], sl] = o.astype(o_refs[d].dtype)


def _rec_kernel(flag_ref, qf_ref, qb_ref, vf_ref, vb_ref, gf_ref, gb_ref, mat_ref, mask_ref, cmat_ref, tri_ref,
                of_ref, ob_ref, state_scr):
    b = pl.program_id(0)
    i = pl.program_id(1)
    n_steps = pl.num_programs(1)
    per_step = REC_STEP_CHUNKS

    @pl.when(i == 0)
    def _():
        state_scr[...] = jnp.zeros_like(state_scr)

    fwd_chunk0 = (b * n_steps + i) * per_step
    bwd_chunk0 = (b * n_steps + n_steps - 1 - i) * per_step
    fast = flag_ref[fwd_chunk0 * 2] == 1
    for j in range(per_step):
        if j:
            fast = jnp.logical_and(fast, flag_ref[(fwd_chunk0 + j) * 2] == 1)
        fast = jnp.logical_and(fast, flag_ref[(bwd_chunk0 + j) * 2 + 1] == 1)
    dirs = ((qf_ref, vf_ref, gf_ref, of_ref), (qb_ref, vb_ref, gb_ref, ob_ref))
    chunk_rows = lambda j: slice(j * REC_CHUNK, (j + 1) * REC_CHUNK)
    rows_of = [(chunk_rows(j), chunk_rows(per_step - 1 - j)) for j in range(per_step)]

    @pl.when(fast)
    def _():
        prep = [[_rec_fast_prep(d, rows_of[j][d], q_ref, v_ref, g_ref, cmat_ref, tri_ref)
                 for d, (q_ref, v_ref, g_ref, _) in enumerate(dirs)] for j in range(per_step)]
        _rec_fast(prep, rows_of, (of_ref, ob_ref), state_scr)

    @pl.when(jnp.logical_not(fast))
    def _():
        for j in range(per_step):
            for d, (q_ref, v_ref, g_ref, o_ref) in enumerate(dirs):
                _rec_robust(d, rows_of[j][d], q_ref, v_ref, g_ref, o_ref, mat_ref, mask_ref, state_scr)


def _hgrn2(rq, g_f, g_b, ri, fast_flags):
    bsz, seq_len, width = rq.shape
    c = REC_CHUNK
    n_steps = seq_len // (c * REC_STEP_CHUNKS)
    mats, masks = _rec_constants()
    n_lvl_rows = (REC_LEVELS + 2) * c
    cmat = np.concatenate([mats[:, 0:c], mats[:, n_lvl_rows:]], axis=1)
    tri = masks.sum(axis=1)
    mats, cmat = jnp.asarray(mats, BF16), jnp.asarray(cmat, BF16)
    masks, tri = jnp.asarray(masks, F32), jnp.asarray(tri, F32)
    fwd = lambda b, i, flags: (b, i, 0)
    bwd = lambda b, i, flags: (b, n_steps - 1 - i, 0)
    blk = (1, c * REC_STEP_CHUNKS, width)
    full3 = lambda b, i, flags: (0, 0, 0)
    full4 = lambda b, i, flags: (0, 0, 0, 0)
    grid_spec = pltpu.PrefetchScalarGridSpec(
        num_scalar_prefetch=1,
        grid=(bsz, n_steps),
        in_specs=[
            pl.BlockSpec(blk, fwd), pl.BlockSpec(blk, bwd),
            pl.BlockSpec(blk, fwd), pl.BlockSpec(blk, bwd),
            pl.BlockSpec(blk, fwd), pl.BlockSpec(blk, bwd),
            pl.BlockSpec(mats.shape, full3),
            pl.BlockSpec(masks.shape, full4),
            pl.BlockSpec(cmat.shape, full3),
            pl.BlockSpec(tri.shape, full3),
        ],
        out_specs=[pl.BlockSpec(blk, fwd), pl.BlockSpec(blk, bwd)],
        scratch_shapes=[pltpu.VMEM((2 * N_REC_HEADS, REC_HEAD_DIM, REC_HEAD_DIM), F32)],
    )
    return pl.pallas_call(
        _rec_kernel,
        grid_spec=grid_spec,
        out_shape=[jax.ShapeDtypeStruct((bsz, seq_len, width), BF16)] * 2,
        compiler_params=pltpu.CompilerParams(
            dimension_semantics=("parallel", "arbitrary"), vmem_limit_bytes=VMEM_LIMIT),
        name="hgrn2",
    )(fast_flags, rq, rq, ri, ri, g_f, g_b, mats, masks, cmat, tri)


def kernel(x, ffn1_norm, ffn1_w_in, ffn1_w_out, mix_norm, w_in_mix, attn_q_norm, attn_k_norm,
           attn_out_norm, rec_lb_logits, rec_out_norm, w_out_mix, ffn2_norm, ffn2_w_in,
           ffn2_w_out, final_norm):
    bsz, seq_len, d = x.shape
    depth = ffn1_norm.shape[0]
    n_tok = bsz * seq_len
    h = x.reshape(n_tok, d)
    to3 = lambda a: a.reshape(bsz, seq_len, a.shape[-1])
    to2 = lambda a: a.reshape(n_tok, a.shape[-1])
    for l in range(depth):
        h = _ffn(h, ffn1_norm[l], ffn1_w_in[l].astype(BF16), ffn1_w_out[l].astype(BF16),
                 tokens=FFN_TOKENS, sub_rows=FFN_SUB_ROWS)
        q, k, v_ext, rq, g_f, g_b, ri, rg, chunk_decay = _mix_in(
            h, mix_norm[l], w_in_mix[l].astype(BF16), attn_q_norm[l], attn_k_norm[l],
            rec_lb_logits, l, seq_len)
        attn = _attention(to3(q), to3(k), to3(v_ext))
        decay = chunk_decay.reshape(-1, REC_STAT_ROWS, V7X_LANES)[:, :MIX_IN_TOKENS // REC_CHUNK, :2]
        fast_flags = (decay <= REC_FAST_RANGE * LOG2_E).astype(jnp.int32).reshape(-1)
        o_f, o_b = _hgrn2(to3(rq), to3(g_f), to3(g_b), to3(ri), fast_flags)
        mixer_args = (to2(attn), to2(o_f), to2(o_b), rg, attn_out_norm[l], rec_out_norm[l],
                      w_out_mix[l].astype(BF16))
        h = _ffn(h, ffn2_norm[l], ffn2_w_in[l].astype(BF16), ffn2_w_out[l].astype(BF16),
                 tokens=FFN_MIX_TOKENS, sub_rows=FFN_MIX_SUB_ROWS, mixer_args=mixer_args,
                 final_gain=final_norm[l])
    return h.reshape(bsz, seq_len, d)
```

```python
import functools

import jax
import jax.numpy as jnp
import numpy as np
from jax import lax
from jax.experimental import pallas as pl
from jax.experimental.pallas import tpu as pltpu

F32 = jnp.float32
BF16 = jnp.bfloat16

EPS = 1e-6
GRID_W = 64
HEAD_DIM = 128
N_Q_HEADS = 4
N_KV_HEADS = 2
KV_GROUPS = N_Q_HEADS // N_KV_HEADS
ATTN_WIDTH = N_Q_HEADS * HEAD_DIM
KV_WIDTH = N_KV_HEADS * HEAD_DIM
ROPE_THETA = 10000.0
ROPE_AXIS_DIM = HEAD_DIM // 2
REC_HEAD_DIM = 128
N_REC_HEADS = 4
REC_WIDTH = N_REC_HEADS * REC_HEAD_DIM

V7X_LANES = 128

FFN_TOKENS = 1024
FFN_SUB_ROWS = 512
FFN_MIX_TOKENS = 1024
FFN_MIX_SUB_ROWS = 512
FFN_FF_CHUNK = 512
MIX_IN_TOKENS = 512
MIX_IN_SUB_ROWS = 256
ATTN_Q_TOKENS = 512
ATTN_KV_TOKENS = 2048
ATTN_ROW_CHUNK = 128
LOG2_E = 1.4426950408889634
REC_CHUNK = 128
REC_STEP_CHUNKS = 4
REC_LEVELS = 7
REC_STAT_ROWS = 8
REC_FAST_RANGE = 150.0

VMEM_LIMIT = 56 * 1024 * 1024


def _rms(x, gain):
    return x * lax.rsqrt(jnp.mean(x * x, axis=-1, keepdims=True) + EPS) * gain


def _dot(a, b):
    return jnp.dot(a, b, preferred_element_type=F32)


def _dot_nt(a, b):
    return lax.dot_general(a, b, (((1,), (1,)), ((), ())), preferred_element_type=F32)


def _dot_tn(a, b):
    return lax.dot_general(a, b, (((0,), (0,)), ((), ())), preferred_element_type=F32)


def _mixer_out_rows(rows, attn_ref, of_ref, ob_ref, rg_ref, ag_ref, rgain_ref, wmix_ref):
    parts = [_rms(attn_ref[rows, :].astype(F32), ag_ref[...]).astype(BF16)]
    for hh in range(N_REC_HEADS):
        sl = slice(hh * REC_HEAD_DIM, (hh + 1) * REC_HEAD_DIM)
        gate = rg_ref[rows, sl].astype(F32)
        o = of_ref[rows, sl].astype(F32) + ob_ref[rows, sl].astype(F32)
        parts.append((_rms(o, rgain_ref[...]) * (gate * jax.nn.sigmoid(gate))).astype(BF16))
    return _dot(jnp.concatenate(parts, axis=-1), wmix_ref[...])


def _ffn_kernel(*refs, mixer, final_norm, sub_rows, ff_chunks):
    refs = list(refs)
    x_ref = refs.pop(0)
    mix_refs = [refs.pop(0) for _ in range(7)] if mixer else None
    gain_ref, win_ref, wout_ref = refs.pop(0), refs.pop(0), refs.pop(0)
    fgain_ref = refs.pop(0) if final_norm else None
    out_ref, h_scr, acc_scr = refs
    d_ff = wout_ref.shape[0]
    n_sub = x_ref.shape[0] // sub_rows
    res_ref = out_ref if mixer else x_ref
    sub_slice = lambda sub: slice(sub * sub_rows, (sub + 1) * sub_rows)

    def prologue(sub):
        rows = sub_slice(sub)
        x = x_ref[rows, :]
        if mixer:
            x = x + _mixer_out_rows(rows, *mix_refs)
            out_ref[rows, :] = x
        h_scr[sub] = _rms(x, gain_ref[...]).astype(BF16)

    def epilogue(sub):
        rows = sub_slice(sub)
        y = res_ref[rows, :] + 0.5 * acc_scr[sub]
        if final_norm:
            y = _rms(y, fgain_ref[...])
        out_ref[rows, :] = y

    prologue(0)
    for sub in range(n_sub):
        h = h_scr[sub]
        c0 = 0
        for width in ff_chunks:
            gate = _dot(h, win_ref[:, c0:c0 + width])
            up = _dot(h, win_ref[:, d_ff + c0:d_ff + c0 + width])
            act = ((gate * jax.nn.sigmoid(gate)) * up).astype(BF16)
            contrib = _dot(act, wout_ref[c0:c0 + width, :])
            if c0 == 0:
                acc_scr[sub] = contrib
                if sub + 1 < n_sub:
                    prologue(sub + 1)
                if sub >= 1:
                    epilogue(sub - 1)
            else:
                acc_scr[sub] += contrib
            c0 += width
    epilogue(n_sub - 1)


def _ffn_chunks(d_ff):
    n_full, rest = divmod(d_ff, FFN_FF_CHUNK)
    assert rest % V7X_LANES == 0
    return (FFN_FF_CHUNK,) * n_full + ((rest,) if rest else ())


def _ffn(x2d, gain, w_in, w_out, *, tokens, sub_rows, mixer_args=None, final_gain=None):
    n_tok, d = x2d.shape
    d_ff = w_out.shape[0]
    tok = lambda i: (i, 0)
    const = lambda i: (0, 0)
    resident = lambda shape: pl.BlockSpec(shape, const, pipeline_mode=pl.Buffered(1))
    in_specs = [pl.BlockSpec((tokens, d), tok)]
    args = [x2d]
    if mixer_args is not None:
        attn, o_f, o_b, rg, attn_gain, rec_gain, w_mix = mixer_args
        in_specs += [pl.BlockSpec((tokens, ATTN_WIDTH), tok)] + [pl.BlockSpec((tokens, REC_WIDTH), tok)] * 3
        in_specs += [pl.BlockSpec((1, ATTN_WIDTH), const), pl.BlockSpec((1, REC_HEAD_DIM), const),
                     resident(w_mix.shape)]
        args += [attn, o_f, o_b, rg, attn_gain.reshape(1, ATTN_WIDTH), rec_gain.reshape(1, REC_HEAD_DIM), w_mix]
    in_specs += [pl.BlockSpec((1, d), const), resident(w_in.shape), resident(w_out.shape)]
    args += [gain.reshape(1, d), w_in, w_out]
    if final_gain is not None:
        in_specs.append(pl.BlockSpec((1, d), const))
        args.append(final_gain.reshape(1, d))
    body = functools.partial(
        _ffn_kernel, mixer=mixer_args is not None, final_norm=final_gain is not None,
        sub_rows=sub_rows, ff_chunks=_ffn_chunks(d_ff))
    return pl.pallas_call(
        body,
        grid=(n_tok // tokens,),
        in_specs=in_specs,
        out_specs=pl.BlockSpec((tokens, d), tok),
        out_shape=jax.ShapeDtypeStruct((n_tok, d), F32),
        scratch_shapes=[pltpu.VMEM((tokens // sub_rows, sub_rows, d), BF16),
                        pltpu.VMEM((tokens // sub_rows, sub_rows, d), F32)],
        compiler_params=pltpu.CompilerParams(
            dimension_semantics=("parallel",), vmem_limit_bytes=VMEM_LIMIT),
        name="ffn_mix" if mixer_args is not None else "ffn",
    )(*args)


def _rope(x, cos, sin_signed, first_half):
    swapped = jnp.where(first_half, pltpu.roll(x, 96, 1), pltpu.roll(x, 32, 1))
    return x * cos + swapped * sin_signed


def _mix_in_kernel(x_ref, gain_ref, w_ref, qg_ref, kg_ref, cos_ref, sin_ref, lbl_ref, csel_ref,
                   q_ref, k_ref, v_ref, rq_ref, gf_ref, gb_ref, ri_ref, rg_ref, ctot_ref, *, layer, sub_rows):
    col = {}
    c = 0
    for name, width in (("aq", ATTN_WIDTH), ("ak", KV_WIDTH), ("av", KV_WIDTH), ("rq", REC_WIDTH),
                        ("zf", REC_WIDTH), ("zb", REC_WIDTH), ("ri", REC_WIDTH), ("rg", REC_WIDTH)):
        col[name] = (c, width)
        c += width
    lane = lax.broadcasted_iota(jnp.int32, (sub_rows, HEAD_DIM), 1)
    first_half = (lane % (ROPE_AXIS_DIM)) < (ROPE_AXIS_DIM // 2)
    chunk_lane = lax.broadcasted_iota(jnp.int32, ctot_ref.shape, 1)
    ctot = jnp.zeros(ctot_ref.shape, F32)
    gates = []

    for sub in range(x_ref.shape[0] // sub_rows):
        rows = slice(sub * sub_rows, (sub + 1) * sub_rows)
        h = _rms(x_ref[rows, :], gain_ref[...]).astype(BF16)
        cos = cos_ref[rows, :]
        sin = sin_ref[rows, :]

        def proj(name):
            c0, width = col[name]
            return _dot(h, w_ref[:, c0:c0 + width])

        for d, (name, g_ref) in enumerate((("zf", gf_ref), ("zb", gb_ref))):
            logits = lbl_ref[d]
            e = jnp.exp(logits - jnp.max(logits, axis=0, keepdims=True))
            lb = jnp.sum(e[:layer + 1], axis=0, keepdims=True) / jnp.sum(e, axis=0, keepdims=True)
            g = jnp.log2(lb + (1.0 - lb) * jax.nn.sigmoid(proj(name)))
            g_ref[rows, :] = g
            gates.append((d, rows, g.astype(BF16)))

        aq = proj("aq")
        for hh in range(N_Q_HEADS):
            sl = slice(hh * HEAD_DIM, (hh + 1) * HEAD_DIM)
            qh = _rope(_rms(aq[:, sl], qg_ref[...]), cos, sin, first_half) * (HEAD_DIM ** -0.5 * LOG2_E)
            q_ref[rows, sl] = qh.astype(BF16)
        ak = proj("ak")
        for hh in range(N_KV_HEADS):
            sl = slice(hh * HEAD_DIM, (hh + 1) * HEAD_DIM)
            k_ref[rows, sl] = _rope(_rms(ak[:, sl], kg_ref[...]), cos, sin, first_half).astype(BF16)
        av = proj("av").astype(BF16)
        for hh in range(N_KV_HEADS):
            v_ref[rows, 2 * hh * HEAD_DIM:(2 * hh + 1) * HEAD_DIM] = av[:, hh * HEAD_DIM:(hh + 1) * HEAD_DIM]
            v_ref[rows, (2 * hh + 1) * HEAD_DIM:(2 * hh + 2) * HEAD_DIM] = jnp.ones((sub_rows, HEAD_DIM), BF16)
        rq_ref[rows, :] = proj("rq").astype(rq_ref.dtype)
        ri_ref[rows, :] = proj("ri").astype(ri_ref.dtype)
        rg_ref[rows, :] = proj("rg").astype(rg_ref.dtype)
    for d, rows, g_bf16 in gates:
        chunk_sum = _dot(csel_ref[:, rows], g_bf16)
        ctot = jnp.maximum(ctot, jnp.where(chunk_lane == d, jnp.max(-chunk_sum, axis=-1, keepdims=True), 0.0))
    ctot_ref[...] = ctot


def _rope_tables(seq_len):
    rows = seq_len // GRID_W
    inv_freq = ROPE_THETA ** (-jnp.arange(0, ROPE_AXIS_DIM, 2, dtype=F32) / ROPE_AXIS_DIM)
    ang_r = jnp.arange(rows, dtype=F32)[:, None] * inv_freq[None, :]
    ang_c = jnp.arange(GRID_W, dtype=F32)[:, None] * inv_freq[None, :]
    half = ROPE_AXIS_DIM // 2

    def expand(row_part, col_part):
        r = jnp.broadcast_to(row_part[:, None, :], (rows, GRID_W, 2 * half))
        c = jnp.broadcast_to(col_part[None, :, :], (rows, GRID_W, 2 * half))
        return jnp.concatenate([r, c], axis=-1).reshape(seq_len, 4 * half)

    cos = expand(jnp.concatenate([jnp.cos(ang_r)] * 2, -1), jnp.concatenate([jnp.cos(ang_c)] * 2, -1))
    sin = expand(jnp.concatenate([-jnp.sin(ang_r), jnp.sin(ang_r)], -1),
                 jnp.concatenate([-jnp.sin(ang_c), jnp.sin(ang_c)], -1))
    return cos, sin


def _mix_in(x2d, gain, w_in, q_gain, k_gain, lb_logits, layer, seq_len):
    n_tok, d = x2d.shape
    d_in = w_in.shape[1]
    tm = MIX_IN_TOKENS
    cos, sin = _rope_tables(seq_len)
    chunks = tm // REC_CHUNK
    assert chunks <= REC_STAT_ROWS
    chunk_sel = np.zeros((REC_STAT_ROWS, tm), np.float32)
    chunk_sel[:chunks] = np.repeat(np.eye(chunks, dtype=np.float32), REC_CHUNK, axis=1)
    chunk_sel = jnp.asarray(chunk_sel, BF16)
    tiles_per_seq = seq_len // tm
    tok = lambda i: (i, 0)
    const = lambda i: (0, 0)
    table = lambda i: (i % tiles_per_seq, 0)
    out_shape = [
        jax.ShapeDtypeStruct((n_tok, ATTN_WIDTH), BF16),
        jax.ShapeDtypeStruct((n_tok, KV_WIDTH), BF16),
        jax.ShapeDtypeStruct((n_tok, 2 * KV_WIDTH), BF16),
    ] + [jax.ShapeDtypeStruct((n_tok, REC_WIDTH), dt) for dt in (BF16, F32, F32, BF16, BF16)] + [
        jax.ShapeDtypeStruct((n_tok // tm * REC_STAT_ROWS, V7X_LANES), F32)]
    out_specs = [
        pl.BlockSpec((tm, ATTN_WIDTH), tok),
        pl.BlockSpec((tm, KV_WIDTH), tok),
        pl.BlockSpec((tm, 2 * KV_WIDTH), tok),
    ] + [pl.BlockSpec((tm, REC_WIDTH), tok)] * 5 + [pl.BlockSpec((REC_STAT_ROWS, V7X_LANES), tok)]
    return pl.pallas_call(
        functools.partial(_mix_in_kernel, layer=layer, sub_rows=MIX_IN_SUB_ROWS),
        grid=(n_tok // tm,),
        in_specs=[
            pl.BlockSpec((tm, d), tok),
            pl.BlockSpec((1, d), const),
            pl.BlockSpec((d, d_in), const),
            pl.BlockSpec((1, HEAD_DIM), const),
            pl.BlockSpec((1, HEAD_DIM), const),
            pl.BlockSpec((tm, HEAD_DIM), table),
            pl.BlockSpec((tm, HEAD_DIM), table),
            pl.BlockSpec(lb_logits.shape, lambda i: (0, 0, 0)),
            pl.BlockSpec(chunk_sel.shape, const),
        ],
        out_specs=out_specs,
        out_shape=out_shape,
        compiler_params=pltpu.CompilerParams(
            dimension_semantics=("parallel",), vmem_limit_bytes=VMEM_LIMIT),
        name="mix_in",
    )(x2d, gain.reshape(1, d), w_in, q_gain.reshape(1, HEAD_DIM), k_gain.reshape(1, HEAD_DIM), cos, sin,
      lb_logits, chunk_sel)


def _attn_kernel(q_ref, k_ref, v_ref, o_ref, q2_scr, s_scr, p_scr, mt_scr, m_scr, acc_scr):
    tq = q_ref.shape[1]
    rows = KV_GROUPS * tq
    tk = s_scr.shape[2]
    n_kv = k_ref.shape[1] // tk

    for g in range(KV_GROUPS):
        q2_scr[g * tq:(g + 1) * tq, :] = q_ref[0, :, g * HEAD_DIM:(g + 1) * HEAD_DIM]
    m_scr[...] = jnp.full_like(m_scr, -jnp.inf)
    acc_scr[...] = jnp.zeros_like(acc_scr)

    def scores(j):
        s = _dot_nt(q2_scr[...], k_ref[0, j * tk:(j + 1) * tk, :])
        s_scr[j % 2] = s
        mt_scr[j % 2] = jnp.broadcast_to(jnp.max(s, axis=-1, keepdims=True), (rows, V7X_LANES))

    def accumulate(j):
        slot = j % 2
        m_prev = m_scr[...]
        m_new = jnp.maximum(m_prev, mt_scr[slot])
        m_scr[...] = m_new
        for r in range(0, rows, ATTN_ROW_CHUNK):
            m_r = m_new[r:r + ATTN_ROW_CHUNK]
            for c in range(0, tk, V7X_LANES):
                s_piece = s_scr[slot, r:r + ATTN_ROW_CHUNK, c:c + V7X_LANES]
                p_scr[slot, r:r + ATTN_ROW_CHUNK, c:c + V7X_LANES] = jnp.exp2(s_piece - m_r).astype(BF16)
        alpha = jnp.exp2(m_prev - m_new)
        pv = _dot(p_scr[slot], v_ref[0, j * tk:(j + 1) * tk, :])
        for half in range(2):
            sl = slice(half * HEAD_DIM, (half + 1) * HEAD_DIM)
            acc_scr[:, sl] = alpha * acc_scr[:, sl] + pv[:, sl]

    scores(0)
    for j in range(n_kv):
        if j + 1 < n_kv:
            scores(j + 1)
        accumulate(j)

    o = (acc_scr[:, 0:HEAD_DIM] / acc_scr[:, HEAD_DIM:2 * HEAD_DIM]).astype(o_ref.dtype)
    for g in range(KV_GROUPS):
        o_ref[0, :, g * HEAD_DIM:(g + 1) * HEAD_DIM] = o[g * tq:(g + 1) * tq]


def _attention(q, k, v_ext):
    bsz, seq_len, _ = q.shape
    tq, tk = ATTN_Q_TOKENS, ATTN_KV_TOKENS
    gw = KV_GROUPS * HEAD_DIM
    rows = KV_GROUPS * tq
    return pl.pallas_call(
        _attn_kernel,
        grid=(bsz, N_KV_HEADS, seq_len // tq),
        in_specs=[
            pl.BlockSpec((1, tq, gw), lambda b, h, qi: (b, qi, h)),
            pl.BlockSpec((1, seq_len, HEAD_DIM), lambda b, h, qi: (b, 0, h)),
            pl.BlockSpec((1, seq_len, 2 * HEAD_DIM), lambda b, h, qi: (b, 0, h)),
        ],
        out_specs=pl.BlockSpec((1, tq, gw), lambda b, h, qi: (b, qi, h)),
        out_shape=jax.ShapeDtypeStruct((bsz, seq_len, ATTN_WIDTH), BF16),
        scratch_shapes=[
            pltpu.VMEM((rows, HEAD_DIM), BF16),
            pltpu.VMEM((2, rows, tk), F32),
            pltpu.VMEM((2, rows, tk), BF16),
            pltpu.VMEM((2, rows, V7X_LANES), F32),
            pltpu.VMEM((rows, V7X_LANES), F32),
            pltpu.VMEM((rows, 2 * HEAD_DIM), F32),
        ],
        compiler_params=pltpu.CompilerParams(
            dimension_semantics=("parallel", "parallel", "arbitrary"),
            vmem_limit_bytes=VMEM_LIMIT),
        name="attn",
    )(q, k, v_ext)


def _rec_constants():
    c = REC_CHUNK
    t = np.arange(c)[:, None]
    u = np.arange(c)[None, :]
    blocks = [(u <= t)]
    masks = []
    for lvl in range(REC_LEVELS):
        h = 1 << lvl
        base_t = t - t % (2 * h)
        mid = base_t + h
        upper = t >= mid
        blocks.append(np.where(upper, (u >= mid) & (u <= t), (u > t) & (u < mid)))
        masks.append((t // (2 * h) == u // (2 * h)) & (t % (2 * h) >= h) & (u % (2 * h) < h))
    blocks.append(u > t)
    masks.append(t == u)
    fwd = np.concatenate([b.astype(np.float32) for b in blocks], axis=0)
    fwd_masks = np.stack([m.astype(np.float32) for m in masks])
    n_rows = fwd.shape[0]
    bwd = fwd.reshape(-1, c, c)[:, ::-1, ::-1].reshape(n_rows, c)
    bwd_masks = fwd_masks[:, ::-1, ::-1]
    total = np.ones((8, c), np.float32)
    mats = np.stack([np.concatenate([fwd, total]), np.concatenate([bwd, total])])
    mats = np.concatenate([mats, mats], axis=-1)
    return mats, np.stack([fwd_masks, bwd_masks])


def _split2(x):
    hi = x.astype(BF16)
    lo = (x - hi.astype(F32)).astype(BF16)
    return jnp.concatenate([hi, lo], axis=0)


def _rec_robust(d, rows, q_ref, v_ref, g_ref, o_ref, mat_ref, mask_ref, state_scr):
    c = REC_CHUNK
    g = g_ref[0, rows, :]
    kk = 1.0 - jnp.exp2(g)
    sums = _dot(mat_ref[d], _split2(g))
    q = q_ref[0, rows, :]
    v = v_ref[0, rows, :].astype(BF16)
    for hh in range(N_REC_HEADS):
        sl = slice(hh * REC_HEAD_DIM, (hh + 1) * REC_HEAD_DIM)
        qh = q[:, sl]
        kh = kk[:, sl]
        vh = v[:, sl]
        scores = mask_ref[d, REC_LEVELS] * _dot_nt(qh.astype(BF16), kh.astype(BF16))
        for lvl in range(REC_LEVELS):
            el = jnp.exp2(sums[(lvl + 1) * c:(lvl + 2) * c, sl])
            scores += mask_ref[d, lvl] * _dot_nt((qh * el).astype(BF16), (kh * el).astype(BF16))
        cum = sums[0:c, sl]
        rem = sums[(REC_LEVELS + 1) * c:(REC_LEVELS + 2) * c, sl]
        tot = sums[(REC_LEVELS + 2) * c:(REC_LEVELS + 2) * c + 1, sl]
        state_t = state_scr[d * N_REC_HEADS + hh]
        inter = _dot_nt((qh * jnp.exp2(cum)).astype(BF16), state_t.astype(BF16))
        o_ref[0, rows, sl] = (inter + _dot(scores.astype(BF16), vh)).astype(o_ref.dtype)
        k_tail = (kh * jnp.exp2(rem)).astype(BF16)
        state_scr[d * N_REC_HEADS + hh] = state_t * jnp.exp2(tot) + _dot_tn(vh, k_tail)


def _rec_fast_prep(d, rows, q_ref, v_ref, g_ref, cmat_ref, tri_ref):
    c = REC_CHUNK
    g = g_ref[0, rows, :]
    kk = 1.0 - jnp.exp2(g)
    cs = _dot(cmat_ref[d], _split2(g))
    cum = cs[0:c]
    tot = cs[c:c + 1]
    half = 0.5 * tot
    e_half = jnp.exp2(half)
    k_s = kk * jnp.exp2(half - cum)
    return dict(
        q_s=(q_ref[0, rows, :] * jnp.exp2(cum - half)).astype(BF16),
        k_s=k_s.astype(BF16),
        k_tail=(k_s * e_half).astype(BF16),
        e_half=e_half, e_tot=jnp.exp2(tot),
        v=v_ref[0, rows, :].astype(BF16), causal=tri_ref[d] > 0.0)


def _rec_fast(prep, rows_of, o_refs, state_scr):
    heads = [(d, hh) for d in range(len(o_refs)) for hh in range(N_REC_HEADS)]
    lanes = lambda hh: slice(hh * REC_HEAD_DIM, (hh + 1) * REC_HEAD_DIM)
    updates = [{(d, hh): _dot_tn(pj[d]["v"][:, lanes(hh)], pj[d]["k_tail"][:, lanes(hh)]) for d, hh in heads}
               for pj in prep]
    state = {(d, hh): state_scr[d * N_REC_HEADS + hh] for d, hh in heads}
    res = []
    for pj, uj in zip(prep, updates):
        res_j = {}
        for d, hh in heads:
            p, sl = pj[d], lanes(hh)
            rhs = jnp.concatenate([(state[d, hh] * p["e_half"][:, sl]).astype(BF16), p["k_s"][:, sl]], axis=0)
            res_j[d, hh] = _dot_nt(p["q_s"][:, sl], rhs)
            state[d, hh] = state[d, hh] * p["e_tot"][:, sl] + uj[d, hh]
        res.append(res_j)
    for d, hh in heads:
        state_scr[d * N_REC_HEADS + hh] = state[d, hh]
    for pj, rows_j, res_j in zip(prep, rows_of, res):
        for d, hh in heads:
            p, sl = pj[d], lanes(hh)
            scores = jnp.where(p["causal"], res_j[d, hh][:, REC_HEAD_DIM:], 0.0)
            o = res_j[d, hh][:, :REC_HEAD_DIM] + _dot(scores.astype(BF16), p["v"][:, sl])
            o_refs[d][0, rows_j[d], sl] = o.astype(o_refs[d].dtype)


def _rec_kernel(flag_ref, qf_ref, qb_ref, vf_ref, vb_ref, gf_ref, gb_ref, mat_ref, mask_ref, cmat_ref, tri_ref,
                of_ref, ob_ref, state_scr):
    b = pl.program_id(0)
    i = pl.program_id(1)
    n_steps = pl.num_programs(1)
    per_step = REC_STEP_CHUNKS

    @pl.when(i == 0)
    def _():
        state_scr[...] = jnp.zeros_like(state_scr)

    fwd_chunk0 = (b * n_steps + i) * per_step
    bwd_chunk0 = (b * n_steps + n_steps - 1 - i) * per_step
    fast = flag_ref[fwd_chunk0 * 2] == 1
    for j in range(per_step):
        if j:
            fast = jnp.logical_and(fast, flag_ref[(fwd_chunk0 + j) * 2] == 1)
        fast = jnp.logical_and(fast, flag_ref[(bwd_chunk0 + j) * 2 + 1] == 1)
    dirs = ((qf_ref, vf_ref, gf_ref, of_ref), (qb_ref, vb_ref, gb_ref, ob_ref))
    chunk_rows = lambda j: slice(j * REC_CHUNK, (j + 1) * REC_CHUNK)
    rows_of = [(chunk_rows(j), chunk_rows(per_step - 1 - j)) for j in range(per_step)]

    @pl.when(fast)
    def _():
        prep = [[_rec_fast_prep(d, rows_of[j][d], q_ref, v_ref, g_ref, cmat_ref, tri_ref)
                 for d, (q_ref, v_ref, g_ref, _) in enumerate(dirs)] for j in range(per_step)]
        _rec_fast(prep, rows_of, (of_ref, ob_ref), state_scr)

    @pl.when(jnp.logical_not(fast))
    def _():
        for j in range(per_step):
            for d, (q_ref, v_ref, g_ref, o_ref) in enumerate(dirs):
                _rec_robust(d, rows_of[j][d], q_ref, v_ref, g_ref, o_ref, mat_ref, mask_ref, state_scr)


def _hgrn2(rq, g_f, g_b, ri, fast_flags):
    bsz, seq_len, width = rq.shape
    c = REC_CHUNK
    n_steps = seq_len // (c * REC_STEP_CHUNKS)
    mats, masks = _rec_constants()
    n_lvl_rows = (REC_LEVELS + 2) * c
    cmat = np.concatenate([mats[:, 0:c], mats[:, n_lvl_rows:]], axis=1)
    tri = masks.sum(axis=1)
    mats, cmat = jnp.asarray(mats, BF16), jnp.asarray(cmat, BF16)
    masks, tri = jnp.asarray(masks, F32), jnp.asarray(tri, F32)
    fwd = lambda b, i, flags: (b, i, 0)
    bwd = lambda b, i, flags: (b, n_steps - 1 - i, 0)
    blk = (1, c * REC_STEP_CHUNKS, width)
    full3 = lambda b, i, flags: (0, 0, 0)
    full4 = lambda b, i, flags: (0, 0, 0, 0)
    grid_spec = pltpu.PrefetchScalarGridSpec(
        num_scalar_prefetch=1,
        grid=(bsz, n_steps),
        in_specs=[
            pl.BlockSpec(blk, fwd), pl.BlockSpec(blk, bwd),
            pl.BlockSpec(blk, fwd), pl.BlockSpec(blk, bwd),
            pl.BlockSpec(blk, fwd), pl.BlockSpec(blk, bwd),
            pl.BlockSpec(mats.shape, full3),
            pl.BlockSpec(masks.shape, full4),
            pl.BlockSpec(cmat.shape, full3),
            pl.BlockSpec(tri.shape, full3),
        ],
        out_specs=[pl.BlockSpec(blk, fwd), pl.BlockSpec(blk, bwd)],
        scratch_shapes=[pltpu.VMEM((2 * N_REC_HEADS, REC_HEAD_DIM, REC_HEAD_DIM), F32)],
    )
    return pl.pallas_call(
        _rec_kernel,
        grid_spec=grid_spec,
        out_shape=[jax.ShapeDtypeStruct((bsz, seq_len, width), BF16)] * 2,
        compiler_params=pltpu.CompilerParams(
            dimension_semantics=("parallel", "arbitrary"), vmem_limit_bytes=VMEM_LIMIT),
        name="hgrn2",
    )(fast_flags, rq, rq, ri, ri, g_f, g_b, mats, masks, cmat, tri)


def kernel(x, ffn1_norm, ffn1_w_in, ffn1_w_out, mix_norm, w_in_mix, attn_q_norm, attn_k_norm,
           attn_out_norm, rec_lb_logits, rec_out_norm, w_out_mix, ffn2_norm, ffn2_w_in,
           ffn2_w_out, final_norm):
    bsz, seq_len, d = x.shape
    depth = ffn1_norm.shape[0]
    n_tok = bsz * seq_len
    h = x.reshape(n_tok, d)
    to3 = lambda a: a.reshape(bsz, seq_len, a.shape[-1])
    to2 = lambda a: a.reshape(n_tok, a.shape[-1])
    for l in range(depth):
        h = _ffn(h, ffn1_norm[l], ffn1_w_in[l].astype(BF16), ffn1_w_out[l].astype(BF16),
                 tokens=FFN_TOKENS, sub_rows=FFN_SUB_ROWS)
        q, k, v_ext, rq, g_f, g_b, ri, rg, chunk_decay = _mix_in(
            h, mix_norm[l], w_in_mix[l].astype(BF16), attn_q_norm[l], attn_k_norm[l],
            rec_lb_logits, l, seq_len)
        attn = _attention(to3(q), to3(k), to3(v_ext))
        decay = chunk_decay.reshape(-1, REC_STAT_ROWS, V7X_LANES)[:, :MIX_IN_TOKENS // REC_CHUNK, :2]
        fast_flags = (decay <= REC_FAST_RANGE * LOG2_E).astype(jnp.int32).reshape(-1)
        o_f, o_b = _hgrn2(to3(rq), to3(g_f), to3(g_b), to3(ri), fast_flags)
        mixer_args = (to2(attn), to2(o_f), to2(o_b), rg, attn_out_norm[l], rec_out_norm[l],
                      w_out_mix[l].astype(BF16))
        h = _ffn(h, ffn2_norm[l], ffn2_w_in[l].astype(BF16), ffn2_w_out[l].astype(BF16),
                 tokens=FFN_MIX_TOKENS, sub_rows=FFN_MIX_SUB_ROWS, mixer_args=mixer_args,
                 final_gain=final_norm[l])
    return h.reshape(bsz, seq_len, d)
```

```python
import functools

import jax
import jax.numpy as jnp
import numpy as np
from jax import lax
from jax.experimental import pallas as pl
from jax.experimental.pallas import tpu as pltpu

F32 = jnp.float32
BF16 = jnp.bfloat16

EPS = 1e-6
GRID_W = 64
HEAD_DIM = 128
N_Q_HEADS = 4
N_KV_HEADS = 2
KV_GROUPS = N_Q_HEADS // N_KV_HEADS
ATTN_WIDTH = N_Q_HEADS * HEAD_DIM
KV_WIDTH = N_KV_HEADS * HEAD_DIM
ROPE_THETA = 10000.0
ROPE_AXIS_DIM = HEAD_DIM // 2
REC_HEAD_DIM = 128
N_REC_HEADS = 4
REC_WIDTH = N_REC_HEADS * REC_HEAD_DIM

V7X_LANES = 128

FFN_TOKENS = 1024
FFN_SUB_ROWS = 512
FFN_MIX_TOKENS = 1024
FFN_MIX_SUB_ROWS = 512
FFN_FF_CHUNK = 512
MIX_IN_TOKENS = 512
MIX_IN_SUB_ROWS = 256
ATTN_Q_TOKENS = 512
ATTN_KV_TOKENS = 2048
ATTN_ROW_CHUNK = 128
LOG2_E = 1.4426950408889634
REC_CHUNK = 128
REC_STEP_CHUNKS = 4
REC_LEVELS = 7
REC_STAT_ROWS = 8
REC_FAST_RANGE = 150.0

VMEM_LIMIT = 56 * 1024 * 1024


def _rms(x, gain):
    return x * lax.rsqrt(jnp.mean(x * x, axis=-1, keepdims=True) + EPS) * gain


def _dot(a, b):
    return jnp.dot(a, b, preferred_element_type=F32)


def _dot_nt(a, b):
    return lax.dot_general(a, b, (((1,), (1,)), ((), ())), preferred_element_type=F32)


def _dot_tn(a, b):
    return lax.dot_general(a, b, (((0,), (0,)), ((), ())), preferred_element_type=F32)


def _mixer_out_rows(rows, attn_ref, of_ref, ob_ref, rg_ref, ag_ref, rgain_ref, wmix_ref):
    parts = [_rms(attn_ref[rows, :].astype(F32), ag_ref[...]).astype(BF16)]
    for hh in range(N_REC_HEADS):
        sl = slice(hh * REC_HEAD_DIM, (hh + 1) * REC_HEAD_DIM)
        gate = rg_ref[rows, sl].astype(F32)
        o = of_ref[rows, sl].astype(F32) + ob_ref[rows, sl].astype(F32)
        parts.append((_rms(o, rgain_ref[...]) * (gate * jax.nn.sigmoid(gate))).astype(BF16))
    return _dot(jnp.concatenate(parts, axis=-1), wmix_ref[...])


def _ffn_kernel(*refs, mixer, final_norm, sub_rows, ff_chunks):
    refs = list(refs)
    x_ref = refs.pop(0)
    mix_refs = [refs.pop(0) for _ in range(7)] if mixer else None
    gain_ref, win_ref, wout_ref = refs.pop(0), refs.pop(0), refs.pop(0)
    fgain_ref = refs.pop(0) if final_norm else None
    out_ref, h_scr, acc_scr = refs
    d_ff = wout_ref.shape[0]
    n_sub = x_ref.shape[0] // sub_rows
    res_ref = out_ref if mixer else x_ref
    sub_slice = lambda sub: slice(sub * sub_rows, (sub + 1) * sub_rows)

    def prologue(sub):
        rows = sub_slice(sub)
        x = x_ref[rows, :]
        if mixer:
            x = x + _mixer_out_rows(rows, *mix_refs)
            out_ref[rows, :] = x
        h_scr[sub] = _rms(x, gain_ref[...]).astype(BF16)

    def epilogue(sub):
        rows = sub_slice(sub)
        y = res_ref[rows, :] + 0.5 * acc_scr[sub]
        if final_norm:
            y = _rms(y, fgain_ref[...])
        out_ref[rows, :] = y

    prologue(0)
    for sub in range(n_sub):
        h = h_scr[sub]
        c0 = 0
        for width in ff_chunks:
            gate = _dot(h, win_ref[:, c0:c0 + width])
            up = _dot(h, win_ref[:, d_ff + c0:d_ff + c0 + width])
            act = ((gate * jax.nn.sigmoid(gate)) * up).astype(BF16)
            contrib = _dot(act, wout_ref[c0:c0 + width, :])
            if c0 == 0:
                acc_scr[sub] = contrib
                if sub + 1 < n_sub:
                    prologue(sub + 1)
                if sub >= 1:
                    epilogue(sub - 1)
            else:
                acc_scr[sub] += contrib
            c0 += width
    epilogue(n_sub - 1)


def _ffn_chunks(d_ff):
    n_full, rest = divmod(d_ff, FFN_FF_CHUNK)
    assert rest % V7X_LANES == 0
    return (FFN_FF_CHUNK,) * n_full + ((rest,) if rest else ())


def _ffn(x2d, gain, w_in, w_out, *, tokens, sub_rows, mixer_args=None, final_gain=None):
    n_tok, d = x2d.shape
    d_ff = w_out.shape[0]
    tok = lambda i: (i, 0)
    const = lambda i: (0, 0)
    resident = lambda shape: pl.BlockSpec(shape, const, pipeline_mode=pl.Buffered(1))
    in_specs = [pl.BlockSpec((tokens, d), tok)]
    args = [x2d]
    if mixer_args is not None:
        attn, o_f, o_b, rg, attn_gain, rec_gain, w_mix = mixer_args
        in_specs += [pl.BlockSpec((tokens, ATTN_WIDTH), tok)] + [pl.BlockSpec((tokens, REC_WIDTH), tok)] * 3
        in_specs += [pl.BlockSpec((1, ATTN_WIDTH), const), pl.BlockSpec((1, REC_HEAD_DIM), const),
                     resident(w_mix.shape)]
        args += [attn, o_f, o_b, rg, attn_gain.reshape(1, ATTN_WIDTH), rec_gain.reshape(1, REC_HEAD_DIM), w_mix]
    in_specs += [pl.BlockSpec((1, d), const), resident(w_in.shape), resident(w_out.shape)]
    args += [gain.reshape(1, d), w_in, w_out]
    if final_gain is not None:
        in_specs.append(pl.BlockSpec((1, d), const))
        args.append(final_gain.reshape(1, d))
    body = functools.partial(
        _ffn_kernel, mixer=mixer_args is not None, final_norm=final_gain is not None,
        sub_rows=sub_rows, ff_chunks=_ffn_chunks(d_ff))
    return pl.pallas_call(
        body,
        grid=(n_tok // tokens,),
        in_specs=in_specs,
        out_specs=pl.BlockSpec((tokens, d), tok),
        out_shape=jax.ShapeDtypeStruct((n_tok, d), F32),
        scratch_shapes=[pltpu.VMEM((tokens // sub_rows, sub_rows, d), BF16),
                        pltpu.VMEM((tokens // sub_rows, sub_rows, d), F32)],
        compiler_params=pltpu.CompilerParams(
            dimension_semantics=("parallel",), vmem_limit_bytes=VMEM_LIMIT),
        name="ffn_mix" if mixer_args is not None else "ffn",
    )(*args)


def _rope(x, cos, sin_signed, first_half):
    swapped = jnp.where(first_half, pltpu.roll(x, 96, 1), pltpu.roll(x, 32, 1))
    return x * cos + swapped * sin_signed


def _mix_in_kernel(x_ref, gain_ref, w_ref, qg_ref, kg_ref, cos_ref, sin_ref, lbl_ref, csel_ref,
                   q_ref, k_ref, v_ref, rq_ref, gf_ref, gb_ref, ri_ref, rg_ref, ctot_ref, *, layer, sub_rows):
    col = {}
    c = 0
    for name, width in (("aq", ATTN_WIDTH), ("ak", KV_WIDTH), ("av", KV_WIDTH), ("rq", REC_WIDTH),
                        ("zf", REC_WIDTH), ("zb", REC_WIDTH), ("ri", REC_WIDTH), ("rg", REC_WIDTH)):
        col[name] = (c, width)
        c += width
    lane = lax.broadcasted_iota(jnp.int32, (sub_rows, HEAD_DIM), 1)
    first_half = (lane % (ROPE_AXIS_DIM)) < (ROPE_AXIS_DIM // 2)
    chunk_lane = lax.broadcasted_iota(jnp.int32, ctot_ref.shape, 1)
    ctot = jnp.zeros(ctot_ref.shape, F32)
    gates = []

    for sub in range(x_ref.shape[0] // sub_rows):
        rows = slice(sub * sub_rows, (sub + 1) * sub_rows)
        h = _rms(x_ref[rows, :], gain_ref[...]).astype(BF16)
        cos = cos_ref[rows, :]
        sin = sin_ref[rows, :]

        def proj(name):
            c0, width = col[name]
            return _dot(h, w_ref[:, c0:c0 + width])

        for d, (name, g_ref) in enumerate((("zf", gf_ref), ("zb", gb_ref))):
            logits = lbl_ref[d]
            e = jnp.exp(logits - jnp.max(logits, axis=0, keepdims=True))
            lb = jnp.sum(e[:layer + 1], axis=0, keepdims=True) / jnp.sum(e, axis=0, keepdims=True)
            g = jnp.log2(lb + (1.0 - lb) * jax.nn.sigmoid(proj(name)))
            g_ref[rows, :] = g
            gates.append((d, rows, g.astype(BF16)))

        aq = proj("aq")
        for hh in range(N_Q_HEADS):
            sl = slice(hh * HEAD_DIM, (hh + 1) * HEAD_DIM)
            qh = _rope(_rms(aq[:, sl], qg_ref[...]), cos, sin, first_half) * (HEAD_DIM ** -0.5 * LOG2_E)
            q_ref[rows, sl] = qh.astype(BF16)
        ak = proj("ak")
        for hh in range(N_KV_HEADS):
            sl = slice(hh * HEAD_DIM, (hh + 1) * HEAD_DIM)
            k_ref[rows, sl] = _rope(_rms(ak[:, sl], kg_ref[...]), cos, sin, first_half).astype(BF16)
        av = proj("av").astype(BF16)
        for hh in range(N_KV_HEADS):
            v_ref[rows, 2 * hh * HEAD_DIM:(2 * hh + 1) * HEAD_DIM] = av[:, hh * HEAD_DIM:(hh + 1) * HEAD_DIM]
            v_ref[rows, (2 * hh + 1) * HEAD_DIM:(2 * hh + 2) * HEAD_DIM] = jnp.ones((sub_rows, HEAD_DIM), BF16)
        rq_ref[rows, :] = proj("rq").astype(rq_ref.dtype)
        ri_ref[rows, :] = proj("ri").astype(ri_ref.dtype)
        rg_ref[rows, :] = proj("rg").astype(rg_ref.dtype)
    for d, rows, g_bf16 in gates:
        chunk_sum = _dot(csel_ref[:, rows], g_bf16)
        ctot = jnp.maximum(ctot, jnp.where(chunk_lane == d, jnp.max(-chunk_sum, axis=-1, keepdims=True), 0.0))
    ctot_ref[...] = ctot


def _rope_tables(seq_len):
    rows = seq_len // GRID_W
    inv_freq = (ROPE_THETA ** (-np.arange(0, ROPE_AXIS_DIM, 2, dtype=np.float32) / ROPE_AXIS_DIM)).astype(np.float32)
    ang_r = np.arange(rows, dtype=np.float32)[:, None] * inv_freq[None, :]
    ang_c = np.arange(GRID_W, dtype=np.float32)[:, None] * inv_freq[None, :]
    half = ROPE_AXIS_DIM // 2

    def expand(row_part, col_part):
        r = np.broadcast_to(row_part[:, None, :], (rows, GRID_W, 2 * half))
        c = np.broadcast_to(col_part[None, :, :], (rows, GRID_W, 2 * half))
        return np.concatenate([r, c], axis=-1).reshape(seq_len, 4 * half).astype(np.float32)

    cos = expand(np.concatenate([np.cos(ang_r)] * 2, -1), np.concatenate([np.cos(ang_c)] * 2, -1))
    sin = expand(np.concatenate([-np.sin(ang_r), np.sin(ang_r)], -1),
                 np.concatenate([-np.sin(ang_c), np.sin(ang_c)], -1))
    return jnp.asarray(cos), jnp.asarray(sin)


def _mix_in(x2d, gain, w_in, q_gain, k_gain, lb_logits, layer, seq_len):
    n_tok, d = x2d.shape
    d_in = w_in.shape[1]
    tm = MIX_IN_TOKENS
    cos, sin = _rope_tables(seq_len)
    chunks = tm // REC_CHUNK
    assert chunks <= REC_STAT_ROWS
    chunk_sel = np.zeros((REC_STAT_ROWS, tm), np.float32)
    chunk_sel[:chunks] = np.repeat(np.eye(chunks, dtype=np.float32), REC_CHUNK, axis=1)
    chunk_sel = jnp.asarray(chunk_sel, BF16)
    tiles_per_seq = seq_len // tm
    tok = lambda i: (i, 0)
    const = lambda i: (0, 0)
    table = lambda i: (i % tiles_per_seq, 0)
    out_shape = [
        jax.ShapeDtypeStruct((n_tok, ATTN_WIDTH), BF16),
        jax.ShapeDtypeStruct((n_tok, KV_WIDTH), BF16),
        jax.ShapeDtypeStruct((n_tok, 2 * KV_WIDTH), BF16),
    ] + [jax.ShapeDtypeStruct((n_tok, REC_WIDTH), dt) for dt in (BF16, F32, F32, BF16, BF16)] + [
        jax.ShapeDtypeStruct((n_tok // tm * REC_STAT_ROWS, V7X_LANES), F32)]
    out_specs = [
        pl.BlockSpec((tm, ATTN_WIDTH), tok),
        pl.BlockSpec((tm, KV_WIDTH), tok),
        pl.BlockSpec((tm, 2 * KV_WIDTH), tok),
    ] + [pl.BlockSpec((tm, REC_WIDTH), tok)] * 5 + [pl.BlockSpec((REC_STAT_ROWS, V7X_LANES), tok)]
    return pl.pallas_call(
        functools.partial(_mix_in_kernel, layer=layer, sub_rows=MIX_IN_SUB_ROWS),
        grid=(n_tok // tm,),
        in_specs=[
            pl.BlockSpec((tm, d), tok),
            pl.BlockSpec((1, d), const),
            pl.BlockSpec((d, d_in), const),
            pl.BlockSpec((1, HEAD_DIM), const),
            pl.BlockSpec((1, HEAD_DIM), const),
            pl.BlockSpec((tm, HEAD_DIM), table),
            pl.BlockSpec((tm, HEAD_DIM), table),
            pl.BlockSpec(lb_logits.shape, lambda i: (0, 0, 0)),
            pl.BlockSpec(chunk_sel.shape, const),
        ],
        out_specs=out_specs,
        out_shape=out_shape,
        compiler_params=pltpu.CompilerParams(
            dimension_semantics=("parallel",), vmem_limit_bytes=VMEM_LIMIT),
        name="mix_in",
    )(x2d, gain.reshape(1, d), w_in, q_gain.reshape(1, HEAD_DIM), k_gain.reshape(1, HEAD_DIM), cos, sin,
      lb_logits, chunk_sel)


def _attn_kernel(q_ref, k_ref, v_ref, o_ref, q2_scr, s_scr, p_scr, mt_scr, m_scr, acc_scr):
    tq = q_ref.shape[1]
    rows = KV_GROUPS * tq
    tk = s_scr.shape[2]
    n_kv = k_ref.shape[1] // tk

    for g in range(KV_GROUPS):
        q2_scr[g * tq:(g + 1) * tq, :] = q_ref[0, :, g * HEAD_DIM:(g + 1) * HEAD_DIM]
    m_scr[...] = jnp.full_like(m_scr, -jnp.inf)
    acc_scr[...] = jnp.zeros_like(acc_scr)

    def scores(j):
        s = _dot_nt(q2_scr[...], k_ref[0, j * tk:(j + 1) * tk, :])
        s_scr[j % 2] = s
        mt_scr[j % 2] = jnp.broadcast_to(jnp.max(s, axis=-1, keepdims=True), (rows, V7X_LANES))

    def accumulate(j):
        slot = j % 2
        m_prev = m_scr[...]
        m_new = jnp.maximum(m_prev, mt_scr[slot])
        m_scr[...] = m_new
        for r in range(0, rows, ATTN_ROW_CHUNK):
            m_r = m_new[r:r + ATTN_ROW_CHUNK]
            for c in range(0, tk, V7X_LANES):
                s_piece = s_scr[slot, r:r + ATTN_ROW_CHUNK, c:c + V7X_LANES]
                p_scr[slot, r:r + ATTN_ROW_CHUNK, c:c + V7X_LANES] = jnp.exp2(s_piece - m_r).astype(BF16)
        alpha = jnp.exp2(m_prev - m_new)
        pv = _dot(p_scr[slot], v_ref[0, j * tk:(j + 1) * tk, :])
        for half in range(2):
            sl = slice(half * HEAD_DIM, (half + 1) * HEAD_DIM)
            acc_scr[:, sl] = alpha * acc_scr[:, sl] + pv[:, sl]

    scores(0)
    for j in range(n_kv):
        if j + 1 < n_kv:
            scores(j + 1)
        accumulate(j)

    o = (acc_scr[:, 0:HEAD_DIM] / acc_scr[:, HEAD_DIM:2 * HEAD_DIM]).astype(o_ref.dtype)
    for g in range(KV_GROUPS):
        o_ref[0, :, g * HEAD_DIM:(g + 1) * HEAD_DIM] = o[g * tq:(g + 1) * tq]


def _attention(q, k, v_ext):
    bsz, seq_len, _ = q.shape
    tq, tk = ATTN_Q_TOKENS, ATTN_KV_TOKENS
    gw = KV_GROUPS * HEAD_DIM
    rows = KV_GROUPS * tq
    return pl.pallas_call(
        _attn_kernel,
        grid=(bsz, N_KV_HEADS, seq_len // tq),
        in_specs=[
            pl.BlockSpec((1, tq, gw), lambda b, h, qi: (b, qi, h)),
            pl.BlockSpec((1, seq_len, HEAD_DIM), lambda b, h, qi: (b, 0, h)),
            pl.BlockSpec((1, seq_len, 2 * HEAD_DIM), lambda b, h, qi: (b, 0, h)),
        ],
        out_specs=pl.BlockSpec((1, tq, gw), lambda b, h, qi: (b, qi, h)),
        out_shape=jax.ShapeDtypeStruct((bsz, seq_len, ATTN_WIDTH), BF16),
        scratch_shapes=[
            pltpu.VMEM((rows, HEAD_DIM), BF16),
            pltpu.VMEM((2, rows, tk), F32),
            pltpu.VMEM((2, rows, tk), BF16),
            pltpu.VMEM((2, rows, V7X_LANES), F32),
            pltpu.VMEM((rows, V7X_LANES), F32),
            pltpu.VMEM((rows, 2 * HEAD_DIM), F32),
        ],
        compiler_params=pltpu.CompilerParams(
            dimension_semantics=("parallel", "parallel", "arbitrary"),
            vmem_limit_bytes=VMEM_LIMIT),
        name="attn",
    )(q, k, v_ext)


def _rec_constants():
    c = REC_CHUNK
    t = np.arange(c)[:, None]
    u = np.arange(c)[None, :]
    blocks = [(u <= t)]
    masks = []
    for lvl in range(REC_LEVELS):
        h = 1 << lvl
        base_t = t - t % (2 * h)
        mid = base_t + h
        upper = t >= mid
        blocks.append(np.where(upper, (u >= mid) & (u <= t), (u > t) & (u < mid)))
        masks.append((t // (2 * h) == u // (2 * h)) & (t % (2 * h) >= h) & (u % (2 * h) < h))
    blocks.append(u > t)
    masks.append(t == u)
    fwd = np.concatenate([b.astype(np.float32) for b in blocks], axis=0)
    fwd_masks = np.stack([m.astype(np.float32) for m in masks])
    n_rows = fwd.shape[0]
    bwd = fwd.reshape(-1, c, c)[:, ::-1, ::-1].reshape(n_rows, c)
    bwd_masks = fwd_masks[:, ::-1, ::-1]
    total = np.ones((8, c), np.float32)
    mats = np.stack([np.concatenate([fwd, total]), np.concatenate([bwd, total])])
    mats = np.concatenate([mats, mats], axis=-1)
    return mats, np.stack([fwd_masks, bwd_masks])


def _split2(x):
    hi = x.astype(BF16)
    lo = (x - hi.astype(F32)).astype(BF16)
    return jnp.concatenate([hi, lo], axis=0)


def _rec_robust(d, rows, q_ref, v_ref, g_ref, o_ref, mat_ref, mask_ref, state_scr):
    c = REC_CHUNK
    g = g_ref[0, rows, :]
    kk = 1.0 - jnp.exp2(g)
    sums = _dot(mat_ref[d], _split2(g))
    q = q_ref[0, rows, :]
    v = v_ref[0, rows, :].astype(BF16)
    for hh in range(N_REC_HEADS):
        sl = slice(hh * REC_HEAD_DIM, (hh + 1) * REC_HEAD_DIM)
        qh = q[:, sl]
        kh = kk[:, sl]
        vh = v[:, sl]
        scores = mask_ref[d, REC_LEVELS] * _dot_nt(qh.astype(BF16), kh.astype(BF16))
        for lvl in range(REC_LEVELS):
            el = jnp.exp2(sums[(lvl + 1) * c:(lvl + 2) * c, sl])
            scores += mask_ref[d, lvl] * _dot_nt((qh * el).astype(BF16), (kh * el).astype(BF16))
        cum = sums[0:c, sl]
        rem = sums[(REC_LEVELS + 1) * c:(REC_LEVELS + 2) * c, sl]
        tot = sums[(REC_LEVELS + 2) * c:(REC_LEVELS + 2) * c + 1, sl]
        state_t = state_scr[d * N_REC_HEADS + hh]
        inter = _dot_nt((qh * jnp.exp2(cum)).astype(BF16), state_t.astype(BF16))
        o_ref[0, rows, sl] = (inter + _dot(scores.astype(BF16), vh)).astype(o_ref.dtype)
        k_tail = (kh * jnp.exp2(rem)).astype(BF16)
        state_scr[d * N_REC_HEADS + hh] = state_t * jnp.exp2(tot) + _dot_tn(vh, k_tail)


def _rec_fast_prep(d, rows, q_ref, v_ref, g_ref, cmat_ref, tri_ref):
    c = REC_CHUNK
    g = g_ref[0, rows, :]
    kk = 1.0 - jnp.exp2(g)
    cs = _dot(cmat_ref[d], _split2(g))
    cum = cs[0:c]
    tot = cs[c:c + 1]
    half = 0.5 * tot
    e_half = jnp.exp2(half)
    k_s = kk * jnp.exp2(half - cum)
    return dict(
        q_s=(q_ref[0, rows, :] * jnp.exp2(cum - half)).astype(BF16),
        k_s=k_s.astype(BF16),
        k_tail=(k_s * e_half).astype(BF16),
        e_half=e_half, e_tot=jnp.exp2(tot),
        v=v_ref[0, rows, :].astype(BF16), causal=tri_ref[d] > 0.0)


def _rec_fast(prep, rows_of, o_refs, state_scr):
    heads = [(d, hh) for d in range(len(o_refs)) for hh in range(N_REC_HEADS)]
    lanes = lambda hh: slice(hh * REC_HEAD_DIM, (hh + 1) * REC_HEAD_DIM)
    updates = [{(d, hh): _dot_tn(pj[d]["v"][:, lanes(hh)], pj[d]["k_tail"][:, lanes(hh)]) for d, hh in heads}
               for pj in prep]
    state = {(d, hh): state_scr[d * N_REC_HEADS + hh] for d, hh in heads}
    res = []
    for pj, uj in zip(prep, updates):
        res_j = {}
        for d, hh in heads:
            p, sl = pj[d], lanes(hh)
            rhs = jnp.concatenate([(state[d, hh] * p["e_half"][:, sl]).astype(BF16), p["k_s"][:, sl]], axis=0)
            res_j[d, hh] = _dot_nt(p["q_s"][:, sl], rhs)
            state[d, hh] = state[d, hh] * p["e_tot"][:, sl] + uj[d, hh]
        res.append(res_j)
    for d, hh in heads:
        state_scr[d * N_REC_HEADS + hh] = state[d, hh]
    for pj, rows_j, res_j in zip(prep, rows_of, res):
        for d, hh in heads:
            p, sl = pj[d], lanes(hh)
            scores = jnp.where(p["causal"], res_j[d, hh][:, REC_HEAD_DIM:], 0.0)
            o = res_j[d, hh][:, :REC_HEAD_DIM] + _dot(scores.astype(BF16), p["v"][:, sl])
            o_refs[d][0, rows_j[d], sl] = o.astype(o_refs[d].dtype)


def _rec_kernel(flag_ref, qf_ref, qb_ref, vf_ref, vb_ref, gf_ref, gb_ref, mat_ref, mask_ref, cmat_ref, tri_ref,
                of_ref, ob_ref, state_scr):
    b = pl.program_id(0)
    i = pl.program_id(1)
    n_steps = pl.num_programs(1)
    per_step = REC_STEP_CHUNKS

    @pl.when(i == 0)
    def _():
        state_scr[...] = jnp.zeros_like(state_scr)

    fwd_chunk0 = (b * n_steps + i) * per_step
    bwd_chunk0 = (b * n_steps + n_steps - 1 - i) * per_step
    fast = flag_ref[fwd_chunk0 * 2] == 1
    for j in range(per_step):
        if j:
            fast = jnp.logical_and(fast, flag_ref[(fwd_chunk0 + j) * 2] == 1)
        fast = jnp.logical_and(fast, flag_ref[(bwd_chunk0 + j) * 2 + 1] == 1)
    dirs = ((qf_ref, vf_ref, gf_ref, of_ref), (qb_ref, vb_ref, gb_ref, ob_ref))
    chunk_rows = lambda j: slice(j * REC_CHUNK, (j + 1) * REC_CHUNK)
    rows_of = [(chunk_rows(j), chunk_rows(per_step - 1 - j)) for j in range(per_step)]

    @pl.when(fast)
    def _():
        prep = [[_rec_fast_prep(d, rows_of[j][d], q_ref, v_ref, g_ref, cmat_ref, tri_ref)
                 for d, (q_ref, v_ref, g_ref, _) in enumerate(dirs)] for j in range(per_step)]
        _rec_fast(prep, rows_of, (of_ref, ob_ref), state_scr)

    @pl.when(jnp.logical_not(fast))
    def _():
        for j in range(per_step):
            for d, (q_ref, v_ref, g_ref, o_ref) in enumerate(dirs):
                _rec_robust(d, rows_of[j][d], q_ref, v_ref, g_ref, o_ref, mat_ref, mask_ref, state_scr)


def _hgrn2(rq, g_f, g_b, ri, fast_flags):
    bsz, seq_len, width = rq.shape
    c = REC_CHUNK
    n_steps = seq_len // (c * REC_STEP_CHUNKS)
    mats, masks = _rec_constants()
    n_lvl_rows = (REC_LEVELS + 2) * c
    cmat = np.concatenate([mats[:, 0:c], mats[:, n_lvl_rows:]], axis=1)
    tri = masks.sum(axis=1)
    mats, cmat = jnp.asarray(mats, BF16), jnp.asarray(cmat, BF16)
    masks, tri = jnp.asarray(masks, F32), jnp.asarray(tri, F32)
    fwd = lambda b, i, flags: (b, i, 0)
    bwd = lambda b, i, flags: (b, n_steps - 1 - i, 0)
    blk = (1, c * REC_STEP_CHUNKS, width)
    full3 = lambda b, i, flags: (0, 0, 0)
    full4 = lambda b, i, flags: (0, 0, 0, 0)
    grid_spec = pltpu.PrefetchScalarGridSpec(
        num_scalar_prefetch=1,
        grid=(bsz, n_steps),
        in_specs=[
            pl.BlockSpec(blk, fwd), pl.BlockSpec(blk, bwd),
            pl.BlockSpec(blk, fwd), pl.BlockSpec(blk, bwd),
            pl.BlockSpec(blk, fwd), pl.BlockSpec(blk, bwd),
            pl.BlockSpec(mats.shape, full3),
            pl.BlockSpec(masks.shape, full4),
            pl.BlockSpec(cmat.shape, full3),
            pl.BlockSpec(tri.shape, full3),
        ],
        out_specs=[pl.BlockSpec(blk, fwd), pl.BlockSpec(blk, bwd)],
        scratch_shapes=[pltpu.VMEM((2 * N_REC_HEADS, REC_HEAD_DIM, REC_HEAD_DIM), F32)],
    )
    return pl.pallas_call(
        _rec_kernel,
        grid_spec=grid_spec,
        out_shape=[jax.ShapeDtypeStruct((bsz, seq_len, width), BF16)] * 2,
        compiler_params=pltpu.CompilerParams(
            dimension_semantics=("parallel", "arbitrary"), vmem_limit_bytes=VMEM_LIMIT),
        name="hgrn2",
    )(fast_flags, rq, rq, ri, ri, g_f, g_b, mats, masks, cmat, tri)


def kernel(x, ffn1_norm, ffn1_w_in, ffn1_w_out, mix_norm, w_in_mix, attn_q_norm, attn_k_norm,
           attn_out_norm, rec_lb_logits, rec_out_norm, w_out_mix, ffn2_norm, ffn2_w_in,
           ffn2_w_out, final_norm):
    bsz, seq_len, d = x.shape
    depth = ffn1_norm.shape[0]
    n_tok = bsz * seq_len
    h = x.reshape(n_tok, d)
    to3 = lambda a: a.reshape(bsz, seq_len, a.shape[-1])
    to2 = lambda a: a.reshape(n_tok, a.shape[-1])
    for l in range(depth):
        h = _ffn(h, ffn1_norm[l], ffn1_w_in[l].astype(BF16), ffn1_w_out[l].astype(BF16),
                 tokens=FFN_TOKENS, sub_rows=FFN_SUB_ROWS)
        q, k, v_ext, rq, g_f, g_b, ri, rg, chunk_decay = _mix_in(
            h, mix_norm[l], w_in_mix[l].astype(BF16), attn_q_norm[l], attn_k_norm[l],
            rec_lb_logits, l, seq_len)
        attn = _attention(to3(q), to3(k), to3(v_ext))
        decay = chunk_decay.reshape(-1, REC_STAT_ROWS, V7X_LANES)[:, :MIX_IN_TOKENS // REC_CHUNK, :2]
        fast_flags = (decay <= REC_FAST_RANGE * LOG2_E).astype(jnp.int32).reshape(-1)
        o_f, o_b = _hgrn2(to3(rq), to3(g_f), to3(g_b), to3(ri), fast_flags)
        mixer_args = (to2(attn), to2(o_f), to2(o_b), rg, attn_out_norm[l], rec_out_norm[l],
                      w_out_mix[l].astype(BF16))
        h = _ffn(h, ffn2_norm[l], ffn2_w_in[l].astype(BF16), ffn2_w_out[l].astype(BF16),
                 tokens=FFN_MIX_TOKENS, sub_rows=FFN_MIX_SUB_ROWS, mixer_args=mixer_args,
                 final_gain=final_norm[l])
    return h.reshape(bsz, seq_len, d)
```

```python
import functools

import jax
import jax.numpy as jnp
import numpy as np
from jax import lax
from jax.experimental import pallas as pl
from jax.experimental.pallas import tpu as pltpu

F32 = jnp.float32
BF16 = jnp.bfloat16

EPS = 1e-6
GRID_W = 64
HEAD_DIM = 128
N_Q_HEADS = 4
N_KV_HEADS = 2
KV_GROUPS = N_Q_HEADS // N_KV_HEADS
ATTN_WIDTH = N_Q_HEADS * HEAD_DIM
KV_WIDTH = N_KV_HEADS * HEAD_DIM
ROPE_THETA = 10000.0
ROPE_AXIS_DIM = HEAD_DIM // 2
REC_HEAD_DIM = 128
N_REC_HEADS = 4
REC_WIDTH = N_REC_HEADS * REC_HEAD_DIM

V7X_LANES = 128

FFN_TOKENS = 1024
FFN_SUB_ROWS = 512
FFN_MIX_TOKENS = 1024
FFN_MIX_SUB_ROWS = 512
FFN_FF_CHUNK = 512
MIX_IN_TOKENS = 512
MIX_IN_SUB_ROWS = 256
ATTN_Q_TOKENS = 512
ATTN_KV_TOKENS = 2048
ATTN_ROW_CHUNK = 128
LOG2_E = 1.4426950408889634
REC_CHUNK = 128
REC_STEP_CHUNKS = 4
REC_LEVELS = 7
REC_STAT_ROWS = 8
REC_FAST_RANGE = 150.0

VMEM_LIMIT = 56 * 1024 * 1024


def _rms(x, gain):
    return x * lax.rsqrt(jnp.mean(x * x, axis=-1, keepdims=True) + EPS) * gain


def _dot(a, b):
    return jnp.dot(a, b, preferred_element_type=F32)


def _dot_nt(a, b):
    return lax.dot_general(a, b, (((1,), (1,)), ((), ())), preferred_element_type=F32)


def _dot_tn(a, b):
    return lax.dot_general(a, b, (((0,), (0,)), ((), ())), preferred_element_type=F32)


def _mixer_out_rows(rows, attn_ref, of_ref, ob_ref, rg_ref, ag_ref, rgain_ref, wmix_ref):
    parts = [_rms(attn_ref[rows, :].astype(F32), ag_ref[...]).astype(BF16)]
    for hh in range(N_REC_HEADS):
        sl = slice(hh * REC_HEAD_DIM, (hh + 1) * REC_HEAD_DIM)
        gate = rg_ref[rows, sl].astype(F32)
        o = of_ref[rows, sl].astype(F32) + ob_ref[rows, sl].astype(F32)
        parts.append((_rms(o, rgain_ref[...]) * (gate * jax.nn.sigmoid(gate))).astype(BF16))
    return _dot(jnp.concatenate(parts, axis=-1), wmix_ref[...])


def _cast_blocks(src_refs, dst_refs):
    for src_ref, dst_ref in zip(src_refs, dst_refs):
        dst_ref[...] = src_ref[...].astype(dst_ref.dtype)


def _ffn_kernel(*refs, mixer, final_norm, sub_rows, ff_chunks, n_cast):
    refs = list(refs)
    x_ref = refs.pop(0)
    mix_refs = [refs.pop(0) for _ in range(7)] if mixer else None
    gain_ref, win_ref, wout_ref = refs.pop(0), refs.pop(0), refs.pop(0)
    fgain_ref = refs.pop(0) if final_norm else None
    cast_src = [refs.pop(0) for _ in range(n_cast)]
    out_ref = refs.pop(0)
    cast_dst = [refs.pop(0) for _ in range(n_cast)]
    h_scr, acc_scr = refs
    _cast_blocks(cast_src, cast_dst)
    d_ff = wout_ref.shape[0]
    n_sub = x_ref.shape[0] // sub_rows
    res_ref = out_ref if mixer else x_ref
    sub_slice = lambda sub: slice(sub * sub_rows, (sub + 1) * sub_rows)

    def prologue(sub):
        rows = sub_slice(sub)
        x = x_ref[rows, :]
        if mixer:
            x = x + _mixer_out_rows(rows, *mix_refs)
            out_ref[rows, :] = x
        h_scr[sub] = _rms(x, gain_ref[...]).astype(BF16)

    def epilogue(sub):
        rows = sub_slice(sub)
        y = res_ref[rows, :] + 0.5 * acc_scr[sub]
        if final_norm:
            y = _rms(y, fgain_ref[...])
        out_ref[rows, :] = y

    prologue(0)
    for sub in range(n_sub):
        h = h_scr[sub]
        c0 = 0
        for width in ff_chunks:
            gate = _dot(h, win_ref[:, c0:c0 + width])
            up = _dot(h, win_ref[:, d_ff + c0:d_ff + c0 + width])
            act = ((gate * jax.nn.sigmoid(gate)) * up).astype(BF16)
            contrib = _dot(act, wout_ref[c0:c0 + width, :])
            if c0 == 0:
                acc_scr[sub] = contrib
                if sub + 1 < n_sub:
                    prologue(sub + 1)
                if sub >= 1:
                    epilogue(sub - 1)
            else:
                acc_scr[sub] += contrib
            c0 += width
    epilogue(n_sub - 1)


def _ffn_chunks(d_ff):
    n_full, rest = divmod(d_ff, FFN_FF_CHUNK)
    assert rest % V7X_LANES == 0
    return (FFN_FF_CHUNK,) * n_full + ((rest,) if rest else ())


def _cast_specs(weights, n_steps, step_index):
    specs, shapes = [], []
    for w in weights:
        rows = w.shape[0] // n_steps
        assert rows * n_steps == w.shape[0] and rows % 16 == 0
        specs.append(pl.BlockSpec((rows, w.shape[1]), step_index))
        shapes.append(jax.ShapeDtypeStruct(w.shape, BF16))
    return specs, shapes


def _ffn(x2d, gain, w_in, w_out, *, tokens, sub_rows, mixer_args=None, final_gain=None, cast_weights=()):
    n_tok, d = x2d.shape
    d_ff = w_out.shape[0]
    tok = lambda i: (i, 0)
    const = lambda i: (0, 0)
    resident = lambda shape: pl.BlockSpec(shape, const, pipeline_mode=pl.Buffered(1))
    in_specs = [pl.BlockSpec((tokens, d), tok)]
    args = [x2d]
    if mixer_args is not None:
        attn, o_f, o_b, rg, attn_gain, rec_gain, w_mix = mixer_args
        in_specs += [pl.BlockSpec((tokens, ATTN_WIDTH), tok)] + [pl.BlockSpec((tokens, REC_WIDTH), tok)] * 3
        in_specs += [pl.BlockSpec((1, ATTN_WIDTH), const), pl.BlockSpec((1, REC_HEAD_DIM), const),
                     resident(w_mix.shape)]
        args += [attn, o_f, o_b, rg, attn_gain.reshape(1, ATTN_WIDTH), rec_gain.reshape(1, REC_HEAD_DIM), w_mix]
    in_specs += [pl.BlockSpec((1, d), const), resident(w_in.shape), resident(w_out.shape)]
    args += [gain.reshape(1, d), w_in, w_out]
    if final_gain is not None:
        in_specs.append(pl.BlockSpec((1, d), const))
        args.append(final_gain.reshape(1, d))
    cast_specs, cast_shapes = _cast_specs(cast_weights, n_tok // tokens, tok)
    in_specs += cast_specs
    args += list(cast_weights)
    body = functools.partial(
        _ffn_kernel, mixer=mixer_args is not None, final_norm=final_gain is not None,
        sub_rows=sub_rows, ff_chunks=_ffn_chunks(d_ff), n_cast=len(cast_weights))
    return pl.pallas_call(
        body,
        grid=(n_tok // tokens,),
        in_specs=in_specs,
        out_specs=[pl.BlockSpec((tokens, d), tok)] + cast_specs,
        out_shape=[jax.ShapeDtypeStruct((n_tok, d), F32)] + cast_shapes,
        scratch_shapes=[pltpu.VMEM((tokens // sub_rows, sub_rows, d), BF16),
                        pltpu.VMEM((tokens // sub_rows, sub_rows, d), F32)],
        compiler_params=pltpu.CompilerParams(
            dimension_semantics=("parallel",), vmem_limit_bytes=VMEM_LIMIT),
        name="ffn_mix" if mixer_args is not None else "ffn",
    )(*args)


def _rope(x, cos, sin_signed, first_half):
    swapped = jnp.where(first_half, pltpu.roll(x, 96, 1), pltpu.roll(x, 32, 1))
    return x * cos + swapped * sin_signed


def _mix_in_kernel(x_ref, gain_ref, w_ref, qg_ref, kg_ref, cos_ref, sin_ref, lbl_ref, csel_ref,
                   q_ref, k_ref, v_ref, rq_ref, gf_ref, gb_ref, ri_ref, rg_ref, ctot_ref, *, layer, sub_rows):
    col = {}
    c = 0
    for name, width in (("aq", ATTN_WIDTH), ("ak", KV_WIDTH), ("av", KV_WIDTH), ("rq", REC_WIDTH),
                        ("zf", REC_WIDTH), ("zb", REC_WIDTH), ("ri", REC_WIDTH), ("rg", REC_WIDTH)):
        col[name] = (c, width)
        c += width
    lane = lax.broadcasted_iota(jnp.int32, (sub_rows, HEAD_DIM), 1)
    first_half = (lane % (ROPE_AXIS_DIM)) < (ROPE_AXIS_DIM // 2)
    chunk_lane = lax.broadcasted_iota(jnp.int32, ctot_ref.shape, 1)
    ctot = jnp.zeros(ctot_ref.shape, F32)
    gates = []

    for sub in range(x_ref.shape[0] // sub_rows):
        rows = slice(sub * sub_rows, (sub + 1) * sub_rows)
        h = _rms(x_ref[rows, :], gain_ref[...]).astype(BF16)
        cos = cos_ref[rows, :]
        sin = sin_ref[rows, :]

        def proj(name):
            c0, width = col[name]
            return _dot(h, w_ref[:, c0:c0 + width])

        for d, (name, g_ref) in enumerate((("zf", gf_ref), ("zb", gb_ref))):
            logits = lbl_ref[d]
            e = jnp.exp(logits - jnp.max(logits, axis=0, keepdims=True))
            lb = jnp.sum(e[:layer + 1], axis=0, keepdims=True) / jnp.sum(e, axis=0, keepdims=True)
            g = jnp.log2(lb + (1.0 - lb) * jax.nn.sigmoid(proj(name)))
            g_ref[rows, :] = g
            gates.append((d, rows, g.astype(BF16)))

        aq = proj("aq")
        for hh in range(N_Q_HEADS):
            sl = slice(hh * HEAD_DIM, (hh + 1) * HEAD_DIM)
            qh = _rope(_rms(aq[:, sl], qg_ref[...]), cos, sin, first_half) * (HEAD_DIM ** -0.5 * LOG2_E)
            q_ref[rows, sl] = qh.astype(BF16)
        ak = proj("ak")
        for hh in range(N_KV_HEADS):
            sl = slice(hh * HEAD_DIM, (hh + 1) * HEAD_DIM)
            k_ref[rows, sl] = _rope(_rms(ak[:, sl], kg_ref[...]), cos, sin, first_half).astype(BF16)
        av = proj("av").astype(BF16)
        for hh in range(N_KV_HEADS):
            v_ref[rows, 2 * hh * HEAD_DIM:(2 * hh + 1) * HEAD_DIM] = av[:, hh * HEAD_DIM:(hh + 1) * HEAD_DIM]
            v_ref[rows, (2 * hh + 1) * HEAD_DIM:(2 * hh + 2) * HEAD_DIM] = jnp.ones((sub_rows, HEAD_DIM), BF16)
        rq_ref[rows, :] = proj("rq").astype(rq_ref.dtype)
        ri_ref[rows, :] = proj("ri").astype(ri_ref.dtype)
        rg_ref[rows, :] = proj("rg").astype(rg_ref.dtype)
    for d, rows, g_bf16 in gates:
        chunk_sum = _dot(csel_ref[:, rows], g_bf16)
        ctot = jnp.maximum(ctot, jnp.where(chunk_lane == d, jnp.max(-chunk_sum, axis=-1, keepdims=True), 0.0))
    ctot_ref[...] = ctot


def _rope_tables(seq_len):
    rows = seq_len // GRID_W
    inv_freq = ROPE_THETA ** (-jnp.arange(0, ROPE_AXIS_DIM, 2, dtype=F32) / ROPE_AXIS_DIM)
    ang_r = jnp.arange(rows, dtype=F32)[:, None] * inv_freq[None, :]
    ang_c = jnp.arange(GRID_W, dtype=F32)[:, None] * inv_freq[None, :]
    half = ROPE_AXIS_DIM // 2

    def expand(row_part, col_part):
        r = jnp.broadcast_to(row_part[:, None, :], (rows, GRID_W, 2 * half))
        c = jnp.broadcast_to(col_part[None, :, :], (rows, GRID_W, 2 * half))
        return jnp.concatenate([r, c], axis=-1).reshape(seq_len, 4 * half)

    cos = expand(jnp.concatenate([jnp.cos(ang_r)] * 2, -1), jnp.concatenate([jnp.cos(ang_c)] * 2, -1))
    sin = expand(jnp.concatenate([-jnp.sin(ang_r), jnp.sin(ang_r)], -1),
                 jnp.concatenate([-jnp.sin(ang_c), jnp.sin(ang_c)], -1))
    return cos, sin


def _mix_in(x2d, gain, w_in, q_gain, k_gain, lb_logits, layer, seq_len):
    n_tok, d = x2d.shape
    d_in = w_in.shape[1]
    tm = MIX_IN_TOKENS
    cos, sin = _rope_tables(seq_len)
    chunks = tm // REC_CHUNK
    assert chunks <= REC_STAT_ROWS
    chunk_sel = np.zeros((REC_STAT_ROWS, tm), np.float32)
    chunk_sel[:chunks] = np.repeat(np.eye(chunks, dtype=np.float32), REC_CHUNK, axis=1)
    chunk_sel = jnp.asarray(chunk_sel, BF16)
    tiles_per_seq = seq_len // tm
    tok = lambda i: (i, 0)
    const = lambda i: (0, 0)
    table = lambda i: (i % tiles_per_seq, 0)
    out_shape = [
        jax.ShapeDtypeStruct((n_tok, ATTN_WIDTH), BF16),
        jax.ShapeDtypeStruct((n_tok, KV_WIDTH), BF16),
        jax.ShapeDtypeStruct((n_tok, 2 * KV_WIDTH), BF16),
    ] + [jax.ShapeDtypeStruct((n_tok, REC_WIDTH), dt) for dt in (BF16, F32, F32, BF16, BF16)] + [
        jax.ShapeDtypeStruct((n_tok // tm * REC_STAT_ROWS, V7X_LANES), F32)]
    out_specs = [
        pl.BlockSpec((tm, ATTN_WIDTH), tok),
        pl.BlockSpec((tm, KV_WIDTH), tok),
        pl.BlockSpec((tm, 2 * KV_WIDTH), tok),
    ] + [pl.BlockSpec((tm, REC_WIDTH), tok)] * 5 + [pl.BlockSpec((REC_STAT_ROWS, V7X_LANES), tok)]
    return pl.pallas_call(
        functools.partial(_mix_in_kernel, layer=layer, sub_rows=MIX_IN_SUB_ROWS),
        grid=(n_tok // tm,),
        in_specs=[
            pl.BlockSpec((tm, d), tok),
            pl.BlockSpec((1, d), const),
            pl.BlockSpec((d, d_in), const),
            pl.BlockSpec((1, HEAD_DIM), const),
            pl.BlockSpec((1, HEAD_DIM), const),
            pl.BlockSpec((tm, HEAD_DIM), table),
            pl.BlockSpec((tm, HEAD_DIM), table),
            pl.BlockSpec(lb_logits.shape, lambda i: (0, 0, 0)),
            pl.BlockSpec(chunk_sel.shape, const),
        ],
        out_specs=out_specs,
        out_shape=out_shape,
        compiler_params=pltpu.CompilerParams(
            dimension_semantics=("parallel",), vmem_limit_bytes=VMEM_LIMIT),
        name="mix_in",
    )(x2d, gain.reshape(1, d), w_in, q_gain.reshape(1, HEAD_DIM), k_gain.reshape(1, HEAD_DIM), cos, sin,
      lb_logits, chunk_sel)


def _attn_kernel(q_ref, k_ref, v_ref, wsrc_ref, o_ref, wdst_ref, q2_scr, s_scr, p_scr, mt_scr, m_scr, acc_scr):
    _cast_blocks([wsrc_ref], [wdst_ref])
    tq = q_ref.shape[1]
    rows = KV_GROUPS * tq
    tk = s_scr.shape[2]
    n_kv = k_ref.shape[1] // tk

    for g in range(KV_GROUPS):
        q2_scr[g * tq:(g + 1) * tq, :] = q_ref[0, :, g * HEAD_DIM:(g + 1) * HEAD_DIM]
    m_scr[...] = jnp.full_like(m_scr, -jnp.inf)
    acc_scr[...] = jnp.zeros_like(acc_scr)

    def scores(j):
        s = _dot_nt(q2_scr[...], k_ref[0, j * tk:(j + 1) * tk, :])
        s_scr[j % 2] = s
        mt_scr[j % 2] = jnp.broadcast_to(jnp.max(s, axis=-1, keepdims=True), (rows, V7X_LANES))

    def accumulate(j):
        slot = j % 2
        m_prev = m_scr[...]
        m_new = jnp.maximum(m_prev, mt_scr[slot])
        m_scr[...] = m_new
        for r in range(0, rows, ATTN_ROW_CHUNK):
            m_r = m_new[r:r + ATTN_ROW_CHUNK]
            for c in range(0, tk, V7X_LANES):
                s_piece = s_scr[slot, r:r + ATTN_ROW_CHUNK, c:c + V7X_LANES]
                p_scr[slot, r:r + ATTN_ROW_CHUNK, c:c + V7X_LANES] = jnp.exp2(s_piece - m_r).astype(BF16)
        alpha = jnp.exp2(m_prev - m_new)
        pv = _dot(p_scr[slot], v_ref[0, j * tk:(j + 1) * tk, :])
        for half in range(2):
            sl = slice(half * HEAD_DIM, (half + 1) * HEAD_DIM)
            acc_scr[:, sl] = alpha * acc_scr[:, sl] + pv[:, sl]

    scores(0)
    for j in range(n_kv):
        if j + 1 < n_kv:
            scores(j + 1)
        accumulate(j)

    o = (acc_scr[:, 0:HEAD_DIM] / acc_scr[:, HEAD_DIM:2 * HEAD_DIM]).astype(o_ref.dtype)
    for g in range(KV_GROUPS):
        o_ref[0, :, g * HEAD_DIM:(g + 1) * HEAD_DIM] = o[g * tq:(g + 1) * tq]


def _attention(q, k, v_ext, cast_weight):
    bsz, seq_len, _ = q.shape
    tq, tk = ATTN_Q_TOKENS, ATTN_KV_TOKENS
    gw = KV_GROUPS * HEAD_DIM
    rows = KV_GROUPS * tq
    n_q = seq_len // tq
    step = lambda b, h, qi: ((b * N_KV_HEADS + h) * n_q + qi, 0)
    cast_specs, cast_shapes = _cast_specs([cast_weight], bsz * N_KV_HEADS * n_q, step)
    return pl.pallas_call(
        _attn_kernel,
        grid=(bsz, N_KV_HEADS, n_q),
        in_specs=[
            pl.BlockSpec((1, tq, gw), lambda b, h, qi: (b, qi, h)),
            pl.BlockSpec((1, seq_len, HEAD_DIM), lambda b, h, qi: (b, 0, h)),
            pl.BlockSpec((1, seq_len, 2 * HEAD_DIM), lambda b, h, qi: (b, 0, h)),
        ] + cast_specs,
        out_specs=[pl.BlockSpec((1, tq, gw), lambda b, h, qi: (b, qi, h))] + cast_specs,
        out_shape=[jax.ShapeDtypeStruct((bsz, seq_len, ATTN_WIDTH), BF16)] + cast_shapes,
        scratch_shapes=[
            pltpu.VMEM((rows, HEAD_DIM), BF16),
            pltpu.VMEM((2, rows, tk), F32),
            pltpu.VMEM((2, rows, tk), BF16),
            pltpu.VMEM((2, rows, V7X_LANES), F32),
            pltpu.VMEM((rows, V7X_LANES), F32),
            pltpu.VMEM((rows, 2 * HEAD_DIM), F32),
        ],
        compiler_params=pltpu.CompilerParams(
            dimension_semantics=("parallel", "parallel", "arbitrary"),
            vmem_limit_bytes=VMEM_LIMIT),
        name="attn",
    )(q, k, v_ext, cast_weight)


def _rec_constants():
    c = REC_CHUNK
    t = np.arange(c)[:, None]
    u = np.arange(c)[None, :]
    blocks = [(u <= t)]
    masks = []
    for lvl in range(REC_LEVELS):
        h = 1 << lvl
        base_t = t - t % (2 * h)
        mid = base_t + h
        upper = t >= mid
        blocks.append(np.where(upper, (u >= mid) & (u <= t), (u > t) & (u < mid)))
        masks.append((t // (2 * h) == u // (2 * h)) & (t % (2 * h) >= h) & (u % (2 * h) < h))
    blocks.append(u > t)
    masks.append(t == u)
    fwd = np.concatenate([b.astype(np.float32) for b in blocks], axis=0)
    fwd_masks = np.stack([m.astype(np.float32) for m in masks])
    n_rows = fwd.shape[0]
    bwd = fwd.reshape(-1, c, c)[:, ::-1, ::-1].reshape(n_rows, c)
    bwd_masks = fwd_masks[:, ::-1, ::-1]
    total = np.ones((8, c), np.float32)
    mats = np.stack([np.concatenate([fwd, total]), np.concatenate([bwd, total])])
    mats = np.concatenate([mats, mats], axis=-1)
    return mats, np.stack([fwd_masks, bwd_masks])


def _split2(x):
    hi = x.astype(BF16)
    lo = (x - hi.astype(F32)).astype(BF16)
    return jnp.concatenate([hi, lo], axis=0)


def _rec_robust(d, rows, q_ref, v_ref, g_ref, o_ref, mat_ref, mask_ref, state_scr):
    c = REC_CHUNK
    g = g_ref[0, rows, :]
    kk = 1.0 - jnp.exp2(g)
    sums = _dot(mat_ref[d], _split2(g))
    q = q_ref[0, rows, :]
    v = v_ref[0, rows, :].astype(BF16)
    for hh in range(N_REC_HEADS):
        sl = slice(hh * REC_HEAD_DIM, (hh + 1) * REC_HEAD_DIM)
        qh = q[:, sl]
        kh = kk[:, sl]
        vh = v[:, sl]
        scores = mask_ref[d, REC_LEVELS] * _dot_nt(qh.astype(BF16), kh.astype(BF16))
        for lvl in range(REC_LEVELS):
            el = jnp.exp2(sums[(lvl + 1) * c:(lvl + 2) * c, sl])
            scores += mask_ref[d, lvl] * _dot_nt((qh * el).astype(BF16), (kh * el).astype(BF16))
        cum = sums[0:c, sl]
        rem = sums[(REC_LEVELS + 1) * c:(REC_LEVELS + 2) * c, sl]
        tot = sums[(REC_LEVELS + 2) * c:(REC_LEVELS + 2) * c + 1, sl]
        state_t = state_scr[d * N_REC_HEADS + hh]
        inter = _dot_nt((qh * jnp.exp2(cum)).astype(BF16), state_t.astype(BF16))
        o_ref[0, rows, sl] = (inter + _dot(scores.astype(BF16), vh)).astype(o_ref.dtype)
        k_tail = (kh * jnp.exp2(rem)).astype(BF16)
        state_scr[d * N_REC_HEADS + hh] = state_t * jnp.exp2(tot) + _dot_tn(vh, k_tail)


def _rec_fast_prep(d, rows, q_ref, v_ref, g_ref, cmat_ref, tri_ref):
    c = REC_CHUNK
    g = g_ref[0, rows, :]
    kk = 1.0 - jnp.exp2(g)
    cs = _dot(cmat_ref[d], _split2(g))
    cum = cs[0:c]
    tot = cs[c:c + 1]
    half = 0.5 * tot
    e_half = jnp.exp2(half)
    k_s = kk * jnp.exp2(half - cum)
    return dict(
        q_s=(q_ref[0, rows, :] * jnp.exp2(cum - half)).astype(BF16),
        k_s=k_s.astype(BF16),
        k_tail=(k_s * e_half).astype(BF16),
        e_half=e_half, e_tot=jnp.exp2(tot),
        v=v_ref[0, rows, :].astype(BF16), causal=tri_ref[d] > 0.0)


def _rec_fast(prep, rows_of, o_refs, state_scr):
    heads = [(d, hh) for d in range(len(o_refs)) for hh in range(N_REC_HEADS)]
    lanes = lambda hh: slice(hh * REC_HEAD_DIM, (hh + 1) * REC_HEAD_DIM)
    updates = [{(d, hh): _dot_tn(pj[d]["v"][:, lanes(hh)], pj[d]["k_tail"][:, lanes(hh)]) for d, hh in heads}
               for pj in prep]
    state = {(d, hh): state_scr[d * N_REC_HEADS + hh] for d, hh in heads}
    res = []
    for pj, uj in zip(prep, updates):
        res_j = {}
        for d, hh in heads:
            p, sl = pj[d], lanes(hh)
            rhs = jnp.concatenate([(state[d, hh] * p["e_half"][:, sl]).astype(BF16), p["k_s"][:, sl]], axis=0)
            res_j[d, hh] = _dot_nt(p["q_s"][:, sl], rhs)
            state[d, hh] = state[d, hh] * p["e_tot"][:, sl] + uj[d, hh]
        res.append(res_j)
    for d, hh in heads:
        state_scr[d * N_REC_HEADS + hh] = state[d, hh]
    for pj, rows_j, res_j in zip(prep, rows_of, res):
        for d, hh in heads:
            p, sl = pj[d], lanes(hh)
            scores = jnp.where(p["causal"], res_j[d, hh][:, REC_HEAD_DIM:], 0.0)
            o = res_j[d, hh][:, :REC_HEAD_DIM] + _dot(scores.astype(BF16), p["v"][:, sl])
            o_refs[d][0, rows_j[d], sl] = o.astype(o_refs[d].dtype)


def _rec_kernel(flag_ref, qf_ref, qb_ref, vf_ref, vb_ref, gf_ref, gb_ref, mat_ref, mask_ref, cmat_ref, tri_ref,
                of_ref, ob_ref, state_scr):
    b = pl.program_id(0)
    i = pl.program_id(1)
    n_steps = pl.num_programs(1)
    per_step = REC_STEP_CHUNKS

    @pl.when(i == 0)
    def _():
        state_scr[...] = jnp.zeros_like(state_scr)

    fwd_chunk0 = (b * n_steps + i) * per_step
    bwd_chunk0 = (b * n_steps + n_steps - 1 - i) * per_step
    fast = flag_ref[fwd_chunk0 * 2] == 1
    for j in range(per_step):
        if j:
            fast = jnp.logical_and(fast, flag_ref[(fwd_chunk0 + j) * 2] == 1)
        fast = jnp.logical_and(fast, flag_ref[(bwd_chunk0 + j) * 2 + 1] == 1)
    dirs = ((qf_ref, vf_ref, gf_ref, of_ref), (qb_ref, vb_ref, gb_ref, ob_ref))
    chunk_rows = lambda j: slice(j * REC_CHUNK, (j + 1) * REC_CHUNK)
    rows_of = [(chunk_rows(j), chunk_rows(per_step - 1 - j)) for j in range(per_step)]

    @pl.when(fast)
    def _():
        prep = [[_rec_fast_prep(d, rows_of[j][d], q_ref, v_ref, g_ref, cmat_ref, tri_ref)
                 for d, (q_ref, v_ref, g_ref, _) in enumerate(dirs)] for j in range(per_step)]
        _rec_fast(prep, rows_of, (of_ref, ob_ref), state_scr)

    @pl.when(jnp.logical_not(fast))
    def _():
        for j in range(per_step):
            for d, (q_ref, v_ref, g_ref, o_ref) in enumerate(dirs):
                _rec_robust(d, rows_of[j][d], q_ref, v_ref, g_ref, o_ref, mat_ref, mask_ref, state_scr)


def _hgrn2(rq, g_f, g_b, ri, fast_flags):
    bsz, seq_len, width = rq.shape
    c = REC_CHUNK
    n_steps = seq_len // (c * REC_STEP_CHUNKS)
    mats, masks = _rec_constants()
    n_lvl_rows = (REC_LEVELS + 2) * c
    cmat = np.concatenate([mats[:, 0:c], mats[:, n_lvl_rows:]], axis=1)
    tri = masks.sum(axis=1)
    mats, cmat = jnp.asarray(mats, BF16), jnp.asarray(cmat, BF16)
    masks, tri = jnp.asarray(masks, F32), jnp.asarray(tri, F32)
    fwd = lambda b, i, flags: (b, i, 0)
    bwd = lambda b, i, flags: (b, n_steps - 1 - i, 0)
    blk = (1, c * REC_STEP_CHUNKS, width)
    full3 = lambda b, i, flags: (0, 0, 0)
    full4 = lambda b, i, flags: (0, 0, 0, 0)
    grid_spec = pltpu.PrefetchScalarGridSpec(
        num_scalar_prefetch=1,
        grid=(bsz, n_steps),
        in_specs=[
            pl.BlockSpec(blk, fwd), pl.BlockSpec(blk, bwd),
            pl.BlockSpec(blk, fwd), pl.BlockSpec(blk, bwd),
            pl.BlockSpec(blk, fwd), pl.BlockSpec(blk, bwd),
            pl.BlockSpec(mats.shape, full3),
            pl.BlockSpec(masks.shape, full4),
            pl.BlockSpec(cmat.shape, full3),
            pl.BlockSpec(tri.shape, full3),
        ],
        out_specs=[pl.BlockSpec(blk, fwd), pl.BlockSpec(blk, bwd)],
        scratch_shapes=[pltpu.VMEM((2 * N_REC_HEADS, REC_HEAD_DIM, REC_HEAD_DIM), F32)],
    )
    return pl.pallas_call(
        _rec_kernel,
        grid_spec=grid_spec,
        out_shape=[jax.ShapeDtypeStruct((bsz, seq_len, width), BF16)] * 2,
        compiler_params=pltpu.CompilerParams(
            dimension_semantics=("parallel", "arbitrary"), vmem_limit_bytes=VMEM_LIMIT),
        name="hgrn2",
    )(fast_flags, rq, rq, ri, ri, g_f, g_b, mats, masks, cmat, tri)


def kernel(x, ffn1_norm, ffn1_w_in, ffn1_w_out, mix_norm, w_in_mix, attn_q_norm, attn_k_norm,
           attn_out_norm, rec_lb_logits, rec_out_norm, w_out_mix, ffn2_norm, ffn2_w_in,
           ffn2_w_out, final_norm):
    bsz, seq_len, d = x.shape
    depth = ffn1_norm.shape[0]
    n_tok = bsz * seq_len
    h = x.reshape(n_tok, d)
    to3 = lambda a: a.reshape(bsz, seq_len, a.shape[-1])
    to2 = lambda a: a.reshape(n_tok, a.shape[-1])
    for l in range(depth):
        h, w_in_mix_bf, w_out_mix_bf, ffn2_w_out_bf = _ffn(
            h, ffn1_norm[l], ffn1_w_in[l].astype(BF16), ffn1_w_out[l].astype(BF16),
            tokens=FFN_TOKENS, sub_rows=FFN_SUB_ROWS, cast_weights=(w_in_mix[l], w_out_mix[l], ffn2_w_out[l]))
        q, k, v_ext, rq, g_f, g_b, ri, rg, chunk_decay = _mix_in(
            h, mix_norm[l], w_in_mix_bf, attn_q_norm[l], attn_k_norm[l],
            rec_lb_logits, l, seq_len)
        attn, ffn2_w_in_bf = _attention(to3(q), to3(k), to3(v_ext), ffn2_w_in[l])
        decay = chunk_decay.reshape(-1, REC_STAT_ROWS, V7X_LANES)[:, :MIX_IN_TOKENS // REC_CHUNK, :2]
        fast_flags = (decay <= REC_FAST_RANGE * LOG2_E).astype(jnp.int32).reshape(-1)
        o_f, o_b = _hgrn2(to3(rq), to3(g_f), to3(g_b), to3(ri), fast_flags)
        mixer_args = (to2(attn), to2(o_f), to2(o_b), rg, attn_out_norm[l], rec_out_norm[l], w_out_mix_bf)
        h, = _ffn(h, ffn2_norm[l], ffn2_w_in_bf, ffn2_w_out_bf,
                  tokens=FFN_MIX_TOKENS, sub_rows=FFN_MIX_SUB_ROWS, mixer_args=mixer_args,
                  final_gain=final_norm[l])
    return h.reshape(bsz, seq_len, d)
```

```python
import functools

import jax
import jax.numpy as jnp
import numpy as np
from jax import lax
from jax.experimental import pallas as pl
from jax.experimental.pallas import tpu as pltpu

F32 = jnp.float32
BF16 = jnp.bfloat16

EPS = 1e-6
GRID_W = 64
HEAD_DIM = 128
N_Q_HEADS = 4
N_KV_HEADS = 2
KV_GROUPS = N_Q_HEADS // N_KV_HEADS
ATTN_WIDTH = N_Q_HEADS * HEAD_DIM
KV_WIDTH = N_KV_HEADS * HEAD_DIM
ROPE_THETA = 10000.0
ROPE_AXIS_DIM = HEAD_DIM // 2
REC_HEAD_DIM = 128
N_REC_HEADS = 4
REC_WIDTH = N_REC_HEADS * REC_HEAD_DIM

V7X_LANES = 128

FFN_TOKENS = 1024
FFN_SUB_ROWS = 512
FFN_MIX_TOKENS = 1024
FFN_MIX_SUB_ROWS = 512
FFN_FF_CHUNK = 512
MIX_IN_TOKENS = 512
MIX_IN_SUB_ROWS = 256
ATTN_Q_TOKENS = 512
ATTN_KV_TOKENS = 2048
ATTN_ROW_CHUNK = 128
LOG2_E = 1.4426950408889634
REC_CHUNK = 128
REC_STEP_CHUNKS = 8
REC_LEVELS = 7
REC_STAT_ROWS = 8
REC_FAST_RANGE = 150.0

VMEM_LIMIT = 56 * 1024 * 1024


def _rms(x, gain):
    return x * lax.rsqrt(jnp.mean(x * x, axis=-1, keepdims=True) + EPS) * gain


def _dot(a, b):
    return jnp.dot(a, b, preferred_element_type=F32)


def _dot_nt(a, b):
    return lax.dot_general(a, b, (((1,), (1,)), ((), ())), preferred_element_type=F32)


def _dot_tn(a, b):
    return lax.dot_general(a, b, (((0,), (0,)), ((), ())), preferred_element_type=F32)


def _mixer_out_rows(rows, attn_ref, of_ref, ob_ref, rg_ref, ag_ref, rgain_ref, wmix_ref):
    parts = [_rms(attn_ref[rows, :].astype(F32), ag_ref[...]).astype(BF16)]
    for hh in range(N_REC_HEADS):
        sl = slice(hh * REC_HEAD_DIM, (hh + 1) * REC_HEAD_DIM)
        gate = rg_ref[rows, sl].astype(F32)
        o = of_ref[rows, sl].astype(F32) + ob_ref[rows, sl].astype(F32)
        parts.append((_rms(o, rgain_ref[...]) * (gate * jax.nn.sigmoid(gate))).astype(BF16))
    return _dot(jnp.concatenate(parts, axis=-1), wmix_ref[...])


def _cast_blocks(src_refs, dst_refs):
    for src_ref, dst_ref in zip(src_refs, dst_refs):
        dst_ref[...] = src_ref[...].astype(dst_ref.dtype)


def _ffn_kernel(*refs, mixer, final_norm, sub_rows, ff_chunks, n_cast):
    refs = list(refs)
    x_ref = refs.pop(0)
    mix_refs = [refs.pop(0) for _ in range(7)] if mixer else None
    gain_ref, win_ref, wout_ref = refs.pop(0), refs.pop(0), refs.pop(0)
    fgain_ref = refs.pop(0) if final_norm else None
    cast_src = [refs.pop(0) for _ in range(n_cast)]
    out_ref = refs.pop(0)
    cast_dst = [refs.pop(0) for _ in range(n_cast)]
    h_scr, acc_scr = refs
    _cast_blocks(cast_src, cast_dst)
    d_ff = wout_ref.shape[0]
    n_sub = x_ref.shape[0] // sub_rows
    res_ref = out_ref if mixer else x_ref
    sub_slice = lambda sub: slice(sub * sub_rows, (sub + 1) * sub_rows)

    def prologue(sub):
        rows = sub_slice(sub)
        x = x_ref[rows, :]
        if mixer:
            x = x + _mixer_out_rows(rows, *mix_refs)
            out_ref[rows, :] = x
        h_scr[sub] = _rms(x, gain_ref[...]).astype(BF16)

    def epilogue(sub):
        rows = sub_slice(sub)
        y = res_ref[rows, :] + 0.5 * acc_scr[sub]
        if final_norm:
            y = _rms(y, fgain_ref[...])
        out_ref[rows, :] = y

    prologue(0)
    for sub in range(n_sub):
        h = h_scr[sub]
        c0 = 0
        for width in ff_chunks:
            gate = _dot(h, win_ref[:, c0:c0 + width])
            up = _dot(h, win_ref[:, d_ff + c0:d_ff + c0 + width])
            act = ((gate * jax.nn.sigmoid(gate)) * up).astype(BF16)
            contrib = _dot(act, wout_ref[c0:c0 + width, :])
            if c0 == 0:
                acc_scr[sub] = contrib
                if sub + 1 < n_sub:
                    prologue(sub + 1)
                if sub >= 1:
                    epilogue(sub - 1)
            else:
                acc_scr[sub] += contrib
            c0 += width
    epilogue(n_sub - 1)


def _ffn_chunks(d_ff):
    n_full, rest = divmod(d_ff, FFN_FF_CHUNK)
    assert rest % V7X_LANES == 0
    return (FFN_FF_CHUNK,) * n_full + ((rest,) if rest else ())


def _cast_specs(weights, n_steps, step_index):
    specs, shapes = [], []
    for w in weights:
        rows = w.shape[0] // n_steps
        assert rows * n_steps == w.shape[0] and rows % 16 == 0
        specs.append(pl.BlockSpec((rows, w.shape[1]), step_index))
        shapes.append(jax.ShapeDtypeStruct(w.shape, BF16))
    return specs, shapes


def _ffn(x2d, gain, w_in, w_out, *, tokens, sub_rows, mixer_args=None, final_gain=None, cast_weights=()):
    n_tok, d = x2d.shape
    d_ff = w_out.shape[0]
    tok = lambda i: (i, 0)
    const = lambda i: (0, 0)
    resident = lambda shape: pl.BlockSpec(shape, const, pipeline_mode=pl.Buffered(1))
    in_specs = [pl.BlockSpec((tokens, d), tok)]
    args = [x2d]
    if mixer_args is not None:
        attn, o_f, o_b, rg, attn_gain, rec_gain, w_mix = mixer_args
        in_specs += [pl.BlockSpec((tokens, ATTN_WIDTH), tok)] + [pl.BlockSpec((tokens, REC_WIDTH), tok)] * 3
        in_specs += [pl.BlockSpec((1, ATTN_WIDTH), const), pl.BlockSpec((1, REC_HEAD_DIM), const),
                     resident(w_mix.shape)]
        args += [attn, o_f, o_b, rg, attn_gain.reshape(1, ATTN_WIDTH), rec_gain.reshape(1, REC_HEAD_DIM), w_mix]
    in_specs += [pl.BlockSpec((1, d), const), resident(w_in.shape), resident(w_out.shape)]
    args += [gain.reshape(1, d), w_in, w_out]
    if final_gain is not None:
        in_specs.append(pl.BlockSpec((1, d), const))
        args.append(final_gain.reshape(1, d))
    cast_specs, cast_shapes = _cast_specs(cast_weights, n_tok // tokens, tok)
    in_specs += cast_specs
    args += list(cast_weights)
    body = functools.partial(
        _ffn_kernel, mixer=mixer_args is not None, final_norm=final_gain is not None,
        sub_rows=sub_rows, ff_chunks=_ffn_chunks(d_ff), n_cast=len(cast_weights))
    return pl.pallas_call(
        body,
        grid=(n_tok // tokens,),
        in_specs=in_specs,
        out_specs=[pl.BlockSpec((tokens, d), tok)] + cast_specs,
        out_shape=[jax.ShapeDtypeStruct((n_tok, d), F32)] + cast_shapes,
        scratch_shapes=[pltpu.VMEM((tokens // sub_rows, sub_rows, d), BF16),
                        pltpu.VMEM((tokens // sub_rows, sub_rows, d), F32)],
        compiler_params=pltpu.CompilerParams(
            dimension_semantics=("parallel",), vmem_limit_bytes=VMEM_LIMIT),
        name="ffn_mix" if mixer_args is not None else "ffn",
    )(*args)


def _rope(x, cos, sin_signed, first_half):
    swapped = jnp.where(first_half, pltpu.roll(x, 96, 1), pltpu.roll(x, 32, 1))
    return x * cos + swapped * sin_signed


def _mix_in_kernel(x_ref, gain_ref, w_ref, qg_ref, kg_ref, cos_ref, sin_ref, lbl_ref, csel_ref,
                   q_ref, k_ref, v_ref, rq_ref, gf_ref, gb_ref, ri_ref, rg_ref, ctot_ref, *, layer, sub_rows):
    col = {}
    c = 0
    for name, width in (("aq", ATTN_WIDTH), ("ak", KV_WIDTH), ("av", KV_WIDTH), ("rq", REC_WIDTH),
                        ("zf", REC_WIDTH), ("zb", REC_WIDTH), ("ri", REC_WIDTH), ("rg", REC_WIDTH)):
        col[name] = (c, width)
        c += width
    lane = lax.broadcasted_iota(jnp.int32, (sub_rows, HEAD_DIM), 1)
    first_half = (lane % (ROPE_AXIS_DIM)) < (ROPE_AXIS_DIM // 2)
    chunk_lane = lax.broadcasted_iota(jnp.int32, ctot_ref.shape, 1)
    ctot = jnp.zeros(ctot_ref.shape, F32)
    gates = []

    for sub in range(x_ref.shape[0] // sub_rows):
        rows = slice(sub * sub_rows, (sub + 1) * sub_rows)
        h = _rms(x_ref[rows, :], gain_ref[...]).astype(BF16)
        cos = cos_ref[rows, :]
        sin = sin_ref[rows, :]

        def proj(name):
            c0, width = col[name]
            return _dot(h, w_ref[:, c0:c0 + width])

        for d, (name, g_ref) in enumerate((("zf", gf_ref), ("zb", gb_ref))):
            logits = lbl_ref[d]
            e = jnp.exp(logits - jnp.max(logits, axis=0, keepdims=True))
            lb = jnp.sum(e[:layer + 1], axis=0, keepdims=True) / jnp.sum(e, axis=0, keepdims=True)
            g = jnp.log2(lb + (1.0 - lb) * jax.nn.sigmoid(proj(name)))
            g_ref[rows, :] = g
            gates.append((d, rows, g.astype(BF16)))

        aq = proj("aq")
        for hh in range(N_Q_HEADS):
            sl = slice(hh * HEAD_DIM, (hh + 1) * HEAD_DIM)
            qh = _rope(_rms(aq[:, sl], qg_ref[...]), cos, sin, first_half) * (HEAD_DIM ** -0.5 * LOG2_E)
            q_ref[rows, sl] = qh.astype(BF16)
        ak = proj("ak")
        for hh in range(N_KV_HEADS):
            sl = slice(hh * HEAD_DIM, (hh + 1) * HEAD_DIM)
            k_ref[rows, sl] = _rope(_rms(ak[:, sl], kg_ref[...]), cos, sin, first_half).astype(BF16)
        av = proj("av").astype(BF16)
        for hh in range(N_KV_HEADS):
            v_ref[rows, 2 * hh * HEAD_DIM:(2 * hh + 1) * HEAD_DIM] = av[:, hh * HEAD_DIM:(hh + 1) * HEAD_DIM]
            v_ref[rows, (2 * hh + 1) * HEAD_DIM:(2 * hh + 2) * HEAD_DIM] = jnp.ones((sub_rows, HEAD_DIM), BF16)
        rq_ref[rows, :] = proj("rq").astype(rq_ref.dtype)
        ri_ref[rows, :] = proj("ri").astype(ri_ref.dtype)
        rg_ref[rows, :] = proj("rg").astype(rg_ref.dtype)
    for d, rows, g_bf16 in gates:
        chunk_sum = _dot(csel_ref[:, rows], g_bf16)
        ctot = jnp.maximum(ctot, jnp.where(chunk_lane == d, jnp.max(-chunk_sum, axis=-1, keepdims=True), 0.0))
    ctot_ref[...] = ctot


def _rope_tables(seq_len):
    rows = seq_len // GRID_W
    inv_freq = ROPE_THETA ** (-jnp.arange(0, ROPE_AXIS_DIM, 2, dtype=F32) / ROPE_AXIS_DIM)
    ang_r = jnp.arange(rows, dtype=F32)[:, None] * inv_freq[None, :]
    ang_c = jnp.arange(GRID_W, dtype=F32)[:, None] * inv_freq[None, :]
    half = ROPE_AXIS_DIM // 2

    def expand(row_part, col_part):
        r = jnp.broadcast_to(row_part[:, None, :], (rows, GRID_W, 2 * half))
        c = jnp.broadcast_to(col_part[None, :, :], (rows, GRID_W, 2 * half))
        return jnp.concatenate([r, c], axis=-1).reshape(seq_len, 4 * half)

    cos = expand(jnp.concatenate([jnp.cos(ang_r)] * 2, -1), jnp.concatenate([jnp.cos(ang_c)] * 2, -1))
    sin = expand(jnp.concatenate([-jnp.sin(ang_r), jnp.sin(ang_r)], -1),
                 jnp.concatenate([-jnp.sin(ang_c), jnp.sin(ang_c)], -1))
    return cos, sin


def _mix_in(x2d, gain, w_in, q_gain, k_gain, lb_logits, layer, seq_len):
    n_tok, d = x2d.shape
    d_in = w_in.shape[1]
    tm = MIX_IN_TOKENS
    cos, sin = _rope_tables(seq_len)
    chunks = tm // REC_CHUNK
    assert chunks <= REC_STAT_ROWS
    chunk_sel = np.zeros((REC_STAT_ROWS, tm), np.float32)
    chunk_sel[:chunks] = np.repeat(np.eye(chunks, dtype=np.float32), REC_CHUNK, axis=1)
    chunk_sel = jnp.asarray(chunk_sel, BF16)
    tiles_per_seq = seq_len // tm
    tok = lambda i: (i, 0)
    const = lambda i: (0, 0)
    table = lambda i: (i % tiles_per_seq, 0)
    out_shape = [
        jax.ShapeDtypeStruct((n_tok, ATTN_WIDTH), BF16),
        jax.ShapeDtypeStruct((n_tok, KV_WIDTH), BF16),
        jax.ShapeDtypeStruct((n_tok, 2 * KV_WIDTH), BF16),
    ] + [jax.ShapeDtypeStruct((n_tok, REC_WIDTH), dt) for dt in (BF16, F32, F32, BF16, BF16)] + [
        jax.ShapeDtypeStruct((n_tok // tm * REC_STAT_ROWS, V7X_LANES), F32)]
    out_specs = [
        pl.BlockSpec((tm, ATTN_WIDTH), tok),
        pl.BlockSpec((tm, KV_WIDTH), tok),
        pl.BlockSpec((tm, 2 * KV_WIDTH), tok),
    ] + [pl.BlockSpec((tm, REC_WIDTH), tok)] * 5 + [pl.BlockSpec((REC_STAT_ROWS, V7X_LANES), tok)]
    return pl.pallas_call(
        functools.partial(_mix_in_kernel, layer=layer, sub_rows=MIX_IN_SUB_ROWS),
        grid=(n_tok // tm,),
        in_specs=[
            pl.BlockSpec((tm, d), tok),
            pl.BlockSpec((1, d), const),
            pl.BlockSpec((d, d_in), const),
            pl.BlockSpec((1, HEAD_DIM), const),
            pl.BlockSpec((1, HEAD_DIM), const),
            pl.BlockSpec((tm, HEAD_DIM), table),
            pl.BlockSpec((tm, HEAD_DIM), table),
            pl.BlockSpec(lb_logits.shape, lambda i: (0, 0, 0)),
            pl.BlockSpec(chunk_sel.shape, const),
        ],
        out_specs=out_specs,
        out_shape=out_shape,
        compiler_params=pltpu.CompilerParams(
            dimension_semantics=("parallel",), vmem_limit_bytes=VMEM_LIMIT),
        name="mix_in",
    )(x2d, gain.reshape(1, d), w_in, q_gain.reshape(1, HEAD_DIM), k_gain.reshape(1, HEAD_DIM), cos, sin,
      lb_logits, chunk_sel)


def _attn_kernel(q_ref, k_ref, v_ref, wsrc_ref, o_ref, wdst_ref, q2_scr, s_scr, p_scr, mt_scr, m_scr, acc_scr):
    _cast_blocks([wsrc_ref], [wdst_ref])
    tq = q_ref.shape[1]
    rows = KV_GROUPS * tq
    tk = s_scr.shape[2]
    n_kv = k_ref.shape[1] // tk

    for g in range(KV_GROUPS):
        q2_scr[g * tq:(g + 1) * tq, :] = q_ref[0, :, g * HEAD_DIM:(g + 1) * HEAD_DIM]
    m_scr[...] = jnp.full_like(m_scr, -jnp.inf)
    acc_scr[...] = jnp.zeros_like(acc_scr)

    def scores(j):
        s = _dot_nt(q2_scr[...], k_ref[0, j * tk:(j + 1) * tk, :])
        s_scr[j % 2] = s
        mt_scr[j % 2] = jnp.broadcast_to(jnp.max(s, axis=-1, keepdims=True), (rows, V7X_LANES))

    def accumulate(j):
        slot = j % 2
        m_prev = m_scr[...]
        m_new = jnp.maximum(m_prev, mt_scr[slot])
        m_scr[...] = m_new
        for r in range(0, rows, ATTN_ROW_CHUNK):
            m_r = m_new[r:r + ATTN_ROW_CHUNK]
            for c in range(0, tk, V7X_LANES):
                s_piece = s_scr[slot, r:r + ATTN_ROW_CHUNK, c:c + V7X_LANES]
                p_scr[slot, r:r + ATTN_ROW_CHUNK, c:c + V7X_LANES] = jnp.exp2(s_piece - m_r).astype(BF16)
        alpha = jnp.exp2(m_prev - m_new)
        pv = _dot(p_scr[slot], v_ref[0, j * tk:(j + 1) * tk, :])
        for half in range(2):
            sl = slice(half * HEAD_DIM, (half + 1) * HEAD_DIM)
            acc_scr[:, sl] = alpha * acc_scr[:, sl] + pv[:, sl]

    scores(0)
    for j in range(n_kv):
        if j + 1 < n_kv:
            scores(j + 1)
        accumulate(j)

    o = (acc_scr[:, 0:HEAD_DIM] / acc_scr[:, HEAD_DIM:2 * HEAD_DIM]).astype(o_ref.dtype)
    for g in range(KV_GROUPS):
        o_ref[0, :, g * HEAD_DIM:(g + 1) * HEAD_DIM] = o[g * tq:(g + 1) * tq]


def _attention(q, k, v_ext, cast_weight):
    bsz, seq_len, _ = q.shape
    tq, tk = ATTN_Q_TOKENS, ATTN_KV_TOKENS
    gw = KV_GROUPS * HEAD_DIM
    rows = KV_GROUPS * tq
    n_q = seq_len // tq
    step = lambda b, h, qi: ((b * N_KV_HEADS + h) * n_q + qi, 0)
    cast_specs, cast_shapes = _cast_specs([cast_weight], bsz * N_KV_HEADS * n_q, step)
    return pl.pallas_call(
        _attn_kernel,
        grid=(bsz, N_KV_HEADS, n_q),
        in_specs=[
            pl.BlockSpec((1, tq, gw), lambda b, h, qi: (b, qi, h)),
            pl.BlockSpec((1, seq_len, HEAD_DIM), lambda b, h, qi: (b, 0, h)),
            pl.BlockSpec((1, seq_len, 2 * HEAD_DIM), lambda b, h, qi: (b, 0, h)),
        ] + cast_specs,
        out_specs=[pl.BlockSpec((1, tq, gw), lambda b, h, qi: (b, qi, h))] + cast_specs,
        out_shape=[jax.ShapeDtypeStruct((bsz, seq_len, ATTN_WIDTH), BF16)] + cast_shapes,
        scratch_shapes=[
            pltpu.VMEM((rows, HEAD_DIM), BF16),
            pltpu.VMEM((2, rows, tk), F32),
            pltpu.VMEM((2, rows, tk), BF16),
            pltpu.VMEM((2, rows, V7X_LANES), F32),
            pltpu.VMEM((rows, V7X_LANES), F32),
            pltpu.VMEM((rows, 2 * HEAD_DIM), F32),
        ],
        compiler_params=pltpu.CompilerParams(
            dimension_semantics=("parallel", "parallel", "arbitrary"),
            vmem_limit_bytes=VMEM_LIMIT),
        name="attn",
    )(q, k, v_ext, cast_weight)


def _rec_constants():
    c = REC_CHUNK
    t = np.arange(c)[:, None]
    u = np.arange(c)[None, :]
    blocks = [(u <= t)]
    masks = []
    for lvl in range(REC_LEVELS):
        h = 1 << lvl
        base_t = t - t % (2 * h)
        mid = base_t + h
        upper = t >= mid
        blocks.append(np.where(upper, (u >= mid) & (u <= t), (u > t) & (u < mid)))
        masks.append((t // (2 * h) == u // (2 * h)) & (t % (2 * h) >= h) & (u % (2 * h) < h))
    blocks.append(u > t)
    masks.append(t == u)
    fwd = np.concatenate([b.astype(np.float32) for b in blocks], axis=0)
    fwd_masks = np.stack([m.astype(np.float32) for m in masks])
    n_rows = fwd.shape[0]
    bwd = fwd.reshape(-1, c, c)[:, ::-1, ::-1].reshape(n_rows, c)
    bwd_masks = fwd_masks[:, ::-1, ::-1]
    total = np.ones((8, c), np.float32)
    mats = np.stack([np.concatenate([fwd, total]), np.concatenate([bwd, total])])
    mats = np.concatenate([mats, mats], axis=-1)
    return mats, np.stack([fwd_masks, bwd_masks])


def _split2(x):
    hi = x.astype(BF16)
    lo = (x - hi.astype(F32)).astype(BF16)
    return jnp.concatenate([hi, lo], axis=0)


def _rec_robust(d, rows, q_ref, v_ref, g_ref, o_ref, mat_ref, mask_ref, state_scr):
    c = REC_CHUNK
    g = g_ref[0, rows, :]
    kk = 1.0 - jnp.exp2(g)
    sums = _dot(mat_ref[d], _split2(g))
    q = q_ref[0, rows, :]
    v = v_ref[0, rows, :].astype(BF16)
    for hh in range(N_REC_HEADS):
        sl = slice(hh * REC_HEAD_DIM, (hh + 1) * REC_HEAD_DIM)
        qh = q[:, sl]
        kh = kk[:, sl]
        vh = v[:, sl]
        scores = mask_ref[d, REC_LEVELS] * _dot_nt(qh.astype(BF16), kh.astype(BF16))
        for lvl in range(REC_LEVELS):
            el = jnp.exp2(sums[(lvl + 1) * c:(lvl + 2) * c, sl])
            scores += mask_ref[d, lvl] * _dot_nt((qh * el).astype(BF16), (kh * el).astype(BF16))
        cum = sums[0:c, sl]
        rem = sums[(REC_LEVELS + 1) * c:(REC_LEVELS + 2) * c, sl]
        tot = sums[(REC_LEVELS + 2) * c:(REC_LEVELS + 2) * c + 1, sl]
        state_t = state_scr[d * N_REC_HEADS + hh]
        inter = _dot_nt((qh * jnp.exp2(cum)).astype(BF16), state_t.astype(BF16))
        o_ref[0, rows, sl] = (inter + _dot(scores.astype(BF16), vh)).astype(o_ref.dtype)
        k_tail = (kh * jnp.exp2(rem)).astype(BF16)
        state_scr[d * N_REC_HEADS + hh] = state_t * jnp.exp2(tot) + _dot_tn(vh, k_tail)


def _rec_fast_prep(d, rows, q_ref, v_ref, g_ref, cmat_ref, tri_ref):
    c = REC_CHUNK
    g = g_ref[0, rows, :]
    kk = 1.0 - jnp.exp2(g)
    cs = _dot(cmat_ref[d], _split2(g))
    cum = cs[0:c]
    tot = cs[c:c + 1]
    half = 0.5 * tot
    e_half = jnp.exp2(half)
    k_s = kk * jnp.exp2(half - cum)
    return dict(
        q_s=(q_ref[0, rows, :] * jnp.exp2(cum - half)).astype(BF16),
        k_s=k_s.astype(BF16),
        k_tail=(k_s * e_half).astype(BF16),
        e_half=e_half, e_tot=jnp.exp2(tot),
        v=v_ref[0, rows, :].astype(BF16), causal=tri_ref[d] > 0.0)


def _rec_fast(prep, rows_of, o_refs, state_scr):
    heads = [(d, hh) for d in range(len(o_refs)) for hh in range(N_REC_HEADS)]
    lanes = lambda hh: slice(hh * REC_HEAD_DIM, (hh + 1) * REC_HEAD_DIM)
    updates = [{(d, hh): _dot_tn(pj[d]["v"][:, lanes(hh)], pj[d]["k_tail"][:, lanes(hh)]) for d, hh in heads}
               for pj in prep]
    state = {(d, hh): state_scr[d * N_REC_HEADS + hh] for d, hh in heads}
    res = []
    for pj, uj in zip(prep, updates):
        res_j = {}
        for d, hh in heads:
            p, sl = pj[d], lanes(hh)
            rhs = jnp.concatenate([(state[d, hh] * p["e_half"][:, sl]).astype(BF16), p["k_s"][:, sl]], axis=0)
            res_j[d, hh] = _dot_nt(p["q_s"][:, sl], rhs)
            state[d, hh] = state[d, hh] * p["e_tot"][:, sl] + uj[d, hh]
        res.append(res_j)
    for d, hh in heads:
        state_scr[d * N_REC_HEADS + hh] = state[d, hh]
    for pj, rows_j, res_j in zip(prep, rows_of, res):
        for d, hh in heads:
            p, sl = pj[d], lanes(hh)
            scores = jnp.where(p["causal"], res_j[d, hh][:, REC_HEAD_DIM:], 0.0)
            o = res_j[d, hh][:, :REC_HEAD_DIM] + _dot(scores.astype(BF16), p["v"][:, sl])
            o_refs[d][0, rows_j[d], sl] = o.astype(o_refs[d].dtype)


def _rec_kernel(flag_ref, qf_ref, qb_ref, vf_ref, vb_ref, gf_ref, gb_ref, mat_ref, mask_ref, cmat_ref, tri_ref,
                of_ref, ob_ref, state_scr):
    b = pl.program_id(0)
    i = pl.program_id(1)
    n_steps = pl.num_programs(1)
    per_step = REC_STEP_CHUNKS

    @pl.when(i == 0)
    def _():
        state_scr[...] = jnp.zeros_like(state_scr)

    fwd_chunk0 = (b * n_steps + i) * per_step
    bwd_chunk0 = (b * n_steps + n_steps - 1 - i) * per_step
    fast = flag_ref[fwd_chunk0 * 2] == 1
    for j in range(per_step):
        if j:
            fast = jnp.logical_and(fast, flag_ref[(fwd_chunk0 + j) * 2] == 1)
        fast = jnp.logical_and(fast, flag_ref[(bwd_chunk0 + j) * 2 + 1] == 1)
    dirs = ((qf_ref, vf_ref, gf_ref, of_ref), (qb_ref, vb_ref, gb_ref, ob_ref))
    chunk_rows = lambda j: slice(j * REC_CHUNK, (j + 1) * REC_CHUNK)
    rows_of = [(chunk_rows(j), chunk_rows(per_step - 1 - j)) for j in range(per_step)]

    @pl.when(fast)
    def _():
        prep = [[_rec_fast_prep(d, rows_of[j][d], q_ref, v_ref, g_ref, cmat_ref, tri_ref)
                 for d, (q_ref, v_ref, g_ref, _) in enumerate(dirs)] for j in range(per_step)]
        _rec_fast(prep, rows_of, (of_ref, ob_ref), state_scr)

    @pl.when(jnp.logical_not(fast))
    def _():
        for j in range(per_step):
            for d, (q_ref, v_ref, g_ref, o_ref) in enumerate(dirs):
                _rec_robust(d, rows_of[j][d], q_ref, v_ref, g_ref, o_ref, mat_ref, mask_ref, state_scr)


def _hgrn2(rq, g_f, g_b, ri, fast_flags):
    bsz, seq_len, width = rq.shape
    c = REC_CHUNK
    n_steps = seq_len // (c * REC_STEP_CHUNKS)
    mats, masks = _rec_constants()
    n_lvl_rows = (REC_LEVELS + 2) * c
    cmat = np.concatenate([mats[:, 0:c], mats[:, n_lvl_rows:]], axis=1)
    tri = masks.sum(axis=1)
    mats, cmat = jnp.asarray(mats, BF16), jnp.asarray(cmat, BF16)
    masks, tri = jnp.asarray(masks, F32), jnp.asarray(tri, F32)
    fwd = lambda b, i, flags: (b, i, 0)
    bwd = lambda b, i, flags: (b, n_steps - 1 - i, 0)
    blk = (1, c * REC_STEP_CHUNKS, width)
    full3 = lambda b, i, flags: (0, 0, 0)
    full4 = lambda b, i, flags: (0, 0, 0, 0)
    grid_spec = pltpu.PrefetchScalarGridSpec(
        num_scalar_prefetch=1,
        grid=(bsz, n_steps),
        in_specs=[
            pl.BlockSpec(blk, fwd), pl.BlockSpec(blk, bwd),
            pl.BlockSpec(blk, fwd), pl.BlockSpec(blk, bwd),
            pl.BlockSpec(blk, fwd), pl.BlockSpec(blk, bwd),
            pl.BlockSpec(mats.shape, full3),
            pl.BlockSpec(masks.shape, full4),
            pl.BlockSpec(cmat.shape, full3),
            pl.BlockSpec(tri.shape, full3),
        ],
        out_specs=[pl.BlockSpec(blk, fwd), pl.BlockSpec(blk, bwd)],
        scratch_shapes=[pltpu.VMEM((2 * N_REC_HEADS, REC_HEAD_DIM, REC_HEAD_DIM), F32)],
    )
    return pl.pallas_call(
        _rec_kernel,
        grid_spec=grid_spec,
        out_shape=[jax.ShapeDtypeStruct((bsz, seq_len, width), BF16)] * 2,
        compiler_params=pltpu.CompilerParams(
            dimension_semantics=("parallel", "arbitrary"), vmem_limit_bytes=VMEM_LIMIT),
        name="hgrn2",
    )(fast_flags, rq, rq, ri, ri, g_f, g_b, mats, masks, cmat, tri)


def kernel(x, ffn1_norm, ffn1_w_in, ffn1_w_out, mix_norm, w_in_mix, attn_q_norm, attn_k_norm,
           attn_out_norm, rec_lb_logits, rec_out_norm, w_out_mix, ffn2_norm, ffn2_w_in,
           ffn2_w_out, final_norm):
    bsz, seq_len, d = x.shape
    depth = ffn1_norm.shape[0]
    n_tok = bsz * seq_len
    h = x.reshape(n_tok, d)
    to3 = lambda a: a.reshape(bsz, seq_len, a.shape[-1])
    to2 = lambda a: a.reshape(n_tok, a.shape[-1])
    for l in range(depth):
        h, w_in_mix_bf, w_out_mix_bf, ffn2_w_out_bf = _ffn(
            h, ffn1_norm[l], ffn1_w_in[l].astype(BF16), ffn1_w_out[l].astype(BF16),
            tokens=FFN_TOKENS, sub_rows=FFN_SUB_ROWS, cast_weights=(w_in_mix[l], w_out_mix[l], ffn2_w_out[l]))
        q, k, v_ext, rq, g_f, g_b, ri, rg, chunk_decay = _mix_in(
            h, mix_norm[l], w_in_mix_bf, attn_q_norm[l], attn_k_norm[l],
            rec_lb_logits, l, seq_len)
        attn, ffn2_w_in_bf = _attention(to3(q), to3(k), to3(v_ext), ffn2_w_in[l])
        decay = chunk_decay.reshape(-1, REC_STAT_ROWS, V7X_LANES)[:, :MIX_IN_TOKENS // REC_CHUNK, :2]
        fast_flags = (decay <= REC_FAST_RANGE * LOG2_E).astype(jnp.int32).reshape(-1)
        o_f, o_b = _hgrn2(to3(rq), to3(g_f), to3(g_b), to3(ri), fast_flags)
        mixer_args = (to2(attn), to2(o_f), to2(o_b), rg, attn_out_norm[l], rec_out_norm[l], w_out_mix_bf)
        h, = _ffn(h, ffn2_norm[l], ffn2_w_in_bf, ffn2_w_out_bf,
                  tokens=FFN_MIX_TOKENS, sub_rows=FFN_MIX_SUB_ROWS, mixer_args=mixer_args,
                  final_gain=final_norm[l])
    return h.reshape(bsz, seq_len, d)
```

```python
import functools

import jax
import jax.numpy as jnp
import numpy as np
from jax import lax
from jax.experimental import pallas as pl
from jax.experimental.pallas import tpu as pltpu

F32 = jnp.float32
BF16 = jnp.bfloat16

EPS = 1e-6
GRID_W = 64
HEAD_DIM = 128
N_Q_HEADS = 4
N_KV_HEADS = 2
KV_GROUPS = N_Q_HEADS // N_KV_HEADS
ATTN_WIDTH = N_Q_HEADS * HEAD_DIM
KV_WIDTH = N_KV_HEADS * HEAD_DIM
ROPE_THETA = 10000.0
ROPE_AXIS_DIM = HEAD_DIM // 2
REC_HEAD_DIM = 128
N_REC_HEADS = 4
REC_WIDTH = N_REC_HEADS * REC_HEAD_DIM

V7X_LANES = 128

FFN_TOKENS = 1024
FFN_SUB_ROWS = 512
FFN_MIX_TOKENS = 1024
FFN_MIX_SUB_ROWS = 512
FFN_FF_CHUNK = 512
MIX_IN_TOKENS = 512
MIX_IN_SUB_ROWS = 256
ATTN_Q_TOKENS = 512
ATTN_KV_TOKENS = 1024
ATTN_ROW_CHUNK = 128
LOG2_E = 1.4426950408889634
REC_CHUNK = 128
REC_STEP_CHUNKS = 2
REC_LEVELS = 7
REC_STAT_ROWS = 8
REC_FAST_RANGE = 150.0

VMEM_LIMIT = 56 * 1024 * 1024


def _rms(x, gain):
    return x * lax.rsqrt(jnp.mean(x * x, axis=-1, keepdims=True) + EPS) * gain


def _dot(a, b):
    return jnp.dot(a, b, preferred_element_type=F32)


def _dot_nt(a, b):
    return lax.dot_general(a, b, (((1,), (1,)), ((), ())), preferred_element_type=F32)


def _dot_tn(a, b):
    return lax.dot_general(a, b, (((0,), (0,)), ((), ())), preferred_element_type=F32)


def _mixer_out_rows(rows, attn_ref, of_ref, ob_ref, rg_ref, ag_ref, rgain_ref, wmix_ref):
    parts = [_rms(attn_ref[rows, :].astype(F32), ag_ref[...]).astype(BF16)]
    for hh in range(N_REC_HEADS):
        sl = slice(hh * REC_HEAD_DIM, (hh + 1) * REC_HEAD_DIM)
        gate = rg_ref[rows, sl].astype(F32)
        o = of_ref[rows, sl].astype(F32) + ob_ref[rows, sl].astype(F32)
        parts.append((_rms(o, rgain_ref[...]) * (gate * jax.nn.sigmoid(gate))).astype(BF16))
    return _dot(jnp.concatenate(parts, axis=-1), wmix_ref[...])


def _cast_blocks(src_refs, dst_refs):
    for src_ref, dst_ref in zip(src_refs, dst_refs):
        dst_ref[...] = src_ref[...].astype(dst_ref.dtype)


def _ffn_kernel(*refs, mixer, final_norm, sub_rows, ff_chunks, n_cast):
    refs = list(refs)
    x_ref = refs.pop(0)
    mix_refs = [refs.pop(0) for _ in range(7)] if mixer else None
    gain_ref, win_ref, wout_ref = refs.pop(0), refs.pop(0), refs.pop(0)
    fgain_ref = refs.pop(0) if final_norm else None
    cast_src = [refs.pop(0) for _ in range(n_cast)]
    out_ref = refs.pop(0)
    cast_dst = [refs.pop(0) for _ in range(n_cast)]
    h_scr, acc_scr = refs
    _cast_blocks(cast_src, cast_dst)
    d_ff = wout_ref.shape[0]
    n_sub = x_ref.shape[0] // sub_rows
    res_ref = out_ref if mixer else x_ref
    sub_slice = lambda sub: slice(sub * sub_rows, (sub + 1) * sub_rows)

    def prologue(sub):
        rows = sub_slice(sub)
        x = x_ref[rows, :]
        if mixer:
            x = x + _mixer_out_rows(rows, *mix_refs)
            out_ref[rows, :] = x
        h_scr[sub] = _rms(x, gain_ref[...]).astype(BF16)

    def epilogue(sub):
        rows = sub_slice(sub)
        y = res_ref[rows, :] + 0.5 * acc_scr[sub]
        if final_norm:
            y = _rms(y, fgain_ref[...])
        out_ref[rows, :] = y

    prologue(0)
    for sub in range(n_sub):
        h = h_scr[sub]
        c0 = 0
        for width in ff_chunks:
            gate = _dot(h, win_ref[:, c0:c0 + width])
            up = _dot(h, win_ref[:, d_ff + c0:d_ff + c0 + width])
            act = ((gate * jax.nn.sigmoid(gate)) * up).astype(BF16)
            contrib = _dot(act, wout_ref[c0:c0 + width, :])
            if c0 == 0:
                acc_scr[sub] = contrib
                if sub + 1 < n_sub:
                    prologue(sub + 1)
                if sub >= 1:
                    epilogue(sub - 1)
            else:
                acc_scr[sub] += contrib
            c0 += width
    epilogue(n_sub - 1)


def _ffn_chunks(d_ff):
    n_full, rest = divmod(d_ff, FFN_FF_CHUNK)
    assert rest % V7X_LANES == 0
    return (FFN_FF_CHUNK,) * n_full + ((rest,) if rest else ())


def _cast_specs(weights, n_steps, step_index):
    specs, shapes = [], []
    for w in weights:
        rows = w.shape[0] // n_steps
        assert rows * n_steps == w.shape[0] and rows % 16 == 0
        specs.append(pl.BlockSpec((rows, w.shape[1]), step_index))
        shapes.append(jax.ShapeDtypeStruct(w.shape, BF16))
    return specs, shapes


def _ffn(x2d, gain, w_in, w_out, *, tokens, sub_rows, mixer_args=None, final_gain=None, cast_weights=()):
    n_tok, d = x2d.shape
    d_ff = w_out.shape[0]
    tok = lambda i: (i, 0)
    const = lambda i: (0, 0)
    resident = lambda shape: pl.BlockSpec(shape, const, pipeline_mode=pl.Buffered(1))
    in_specs = [pl.BlockSpec((tokens, d), tok)]
    args = [x2d]
    if mixer_args is not None:
        attn, o_f, o_b, rg, attn_gain, rec_gain, w_mix = mixer_args
        in_specs += [pl.BlockSpec((tokens, ATTN_WIDTH), tok)] + [pl.BlockSpec((tokens, REC_WIDTH), tok)] * 3
        in_specs += [pl.BlockSpec((1, ATTN_WIDTH), const), pl.BlockSpec((1, REC_HEAD_DIM), const),
                     resident(w_mix.shape)]
        args += [attn, o_f, o_b, rg, attn_gain.reshape(1, ATTN_WIDTH), rec_gain.reshape(1, REC_HEAD_DIM), w_mix]
    in_specs += [pl.BlockSpec((1, d), const), resident(w_in.shape), resident(w_out.shape)]
    args += [gain.reshape(1, d), w_in, w_out]
    if final_gain is not None:
        in_specs.append(pl.BlockSpec((1, d), const))
        args.append(final_gain.reshape(1, d))
    cast_specs, cast_shapes = _cast_specs(cast_weights, n_tok // tokens, tok)
    in_specs += cast_specs
    args += list(cast_weights)
    body = functools.partial(
        _ffn_kernel, mixer=mixer_args is not None, final_norm=final_gain is not None,
        sub_rows=sub_rows, ff_chunks=_ffn_chunks(d_ff), n_cast=len(cast_weights))
    return pl.pallas_call(
        body,
        grid=(n_tok // tokens,),
        in_specs=in_specs,
        out_specs=[pl.BlockSpec((tokens, d), tok)] + cast_specs,
        out_shape=[jax.ShapeDtypeStruct((n_tok, d), F32)] + cast_shapes,
        scratch_shapes=[pltpu.VMEM((tokens // sub_rows, sub_rows, d), BF16),
                        pltpu.VMEM((tokens // sub_rows, sub_rows, d), F32)],
        compiler_params=pltpu.CompilerParams(
            dimension_semantics=("parallel",), vmem_limit_bytes=VMEM_LIMIT),
        name="ffn_mix" if mixer_args is not None else "ffn",
    )(*args)


def _rope(x, cos, sin_signed, first_half):
    swapped = jnp.where(first_half, pltpu.roll(x, 96, 1), pltpu.roll(x, 32, 1))
    return x * cos + swapped * sin_signed


def _mix_in_kernel(x_ref, gain_ref, w_ref, qg_ref, kg_ref, cos_ref, sin_ref, lbl_ref, csel_ref,
                   q_ref, k_ref, v_ref, rq_ref, gf_ref, gb_ref, ri_ref, rg_ref, ctot_ref, *, layer, sub_rows):
    col = {}
    c = 0
    for name, width in (("aq", ATTN_WIDTH), ("ak", KV_WIDTH), ("av", KV_WIDTH), ("rq", REC_WIDTH),
                        ("zf", REC_WIDTH), ("zb", REC_WIDTH), ("ri", REC_WIDTH), ("rg", REC_WIDTH)):
        col[name] = (c, width)
        c += width
    lane = lax.broadcasted_iota(jnp.int32, (sub_rows, HEAD_DIM), 1)
    first_half = (lane % (ROPE_AXIS_DIM)) < (ROPE_AXIS_DIM // 2)
    chunk_lane = lax.broadcasted_iota(jnp.int32, ctot_ref.shape, 1)
    ctot = jnp.zeros(ctot_ref.shape, F32)
    gates = []

    for sub in range(x_ref.shape[0] // sub_rows):
        rows = slice(sub * sub_rows, (sub + 1) * sub_rows)
        h = _rms(x_ref[rows, :], gain_ref[...]).astype(BF16)
        cos = cos_ref[rows, :]
        sin = sin_ref[rows, :]

        def proj(name):
            c0, width = col[name]
            return _dot(h, w_ref[:, c0:c0 + width])

        for d, (name, g_ref) in enumerate((("zf", gf_ref), ("zb", gb_ref))):
            logits = lbl_ref[d]
            e = jnp.exp(logits - jnp.max(logits, axis=0, keepdims=True))
            lb = jnp.sum(e[:layer + 1], axis=0, keepdims=True) / jnp.sum(e, axis=0, keepdims=True)
            g = jnp.log2(lb + (1.0 - lb) * jax.nn.sigmoid(proj(name)))
            g_ref[rows, :] = g
            gates.append((d, rows, g.astype(BF16)))

        aq = proj("aq")
        for hh in range(N_Q_HEADS):
            sl = slice(hh * HEAD_DIM, (hh + 1) * HEAD_DIM)
            qh = _rope(_rms(aq[:, sl], qg_ref[...]), cos, sin, first_half) * (HEAD_DIM ** -0.5 * LOG2_E)
            q_ref[rows, sl] = qh.astype(BF16)
        ak = proj("ak")
        for hh in range(N_KV_HEADS):
            sl = slice(hh * HEAD_DIM, (hh + 1) * HEAD_DIM)
            k_ref[rows, sl] = _rope(_rms(ak[:, sl], kg_ref[...]), cos, sin, first_half).astype(BF16)
        av = proj("av").astype(BF16)
        for hh in range(N_KV_HEADS):
            v_ref[rows, 2 * hh * HEAD_DIM:(2 * hh + 1) * HEAD_DIM] = av[:, hh * HEAD_DIM:(hh + 1) * HEAD_DIM]
            v_ref[rows, (2 * hh + 1) * HEAD_DIM:(2 * hh + 2) * HEAD_DIM] = jnp.ones((sub_rows, HEAD_DIM), BF16)
        rq_ref[rows, :] = proj("rq").astype(rq_ref.dtype)
        ri_ref[rows, :] = proj("ri").astype(ri_ref.dtype)
        rg_ref[rows, :] = proj("rg").astype(rg_ref.dtype)
    for d, rows, g_bf16 in gates:
        chunk_sum = _dot(csel_ref[:, rows], g_bf16)
        ctot = jnp.maximum(ctot, jnp.where(chunk_lane == d, jnp.max(-chunk_sum, axis=-1, keepdims=True), 0.0))
    ctot_ref[...] = ctot


def _rope_tables(seq_len):
    rows = seq_len // GRID_W
    inv_freq = ROPE_THETA ** (-jnp.arange(0, ROPE_AXIS_DIM, 2, dtype=F32) / ROPE_AXIS_DIM)
    ang_r = jnp.arange(rows, dtype=F32)[:, None] * inv_freq[None, :]
    ang_c = jnp.arange(GRID_W, dtype=F32)[:, None] * inv_freq[None, :]
    half = ROPE_AXIS_DIM // 2

    def expand(row_part, col_part):
        r = jnp.broadcast_to(row_part[:, None, :], (rows, GRID_W, 2 * half))
        c = jnp.broadcast_to(col_part[None, :, :], (rows, GRID_W, 2 * half))
        return jnp.concatenate([r, c], axis=-1).reshape(seq_len, 4 * half)

    cos = expand(jnp.concatenate([jnp.cos(ang_r)] * 2, -1), jnp.concatenate([jnp.cos(ang_c)] * 2, -1))
    sin = expand(jnp.concatenate([-jnp.sin(ang_r), jnp.sin(ang_r)], -1),
                 jnp.concatenate([-jnp.sin(ang_c), jnp.sin(ang_c)], -1))
    return cos, sin


def _mix_in(x2d, gain, w_in, q_gain, k_gain, lb_logits, layer, seq_len):
    n_tok, d = x2d.shape
    d_in = w_in.shape[1]
    tm = MIX_IN_TOKENS
    cos, sin = _rope_tables(seq_len)
    chunks = tm // REC_CHUNK
    assert chunks <= REC_STAT_ROWS
    chunk_sel = np.zeros((REC_STAT_ROWS, tm), np.float32)
    chunk_sel[:chunks] = np.repeat(np.eye(chunks, dtype=np.float32), REC_CHUNK, axis=1)
    chunk_sel = jnp.asarray(chunk_sel, BF16)
    tiles_per_seq = seq_len // tm
    tok = lambda i: (i, 0)
    const = lambda i: (0, 0)
    table = lambda i: (i % tiles_per_seq, 0)
    out_shape = [
        jax.ShapeDtypeStruct((n_tok, ATTN_WIDTH), BF16),
        jax.ShapeDtypeStruct((n_tok, KV_WIDTH), BF16),
        jax.ShapeDtypeStruct((n_tok, 2 * KV_WIDTH), BF16),
    ] + [jax.ShapeDtypeStruct((n_tok, REC_WIDTH), dt) for dt in (BF16, F32, F32, BF16, BF16)] + [
        jax.ShapeDtypeStruct((n_tok // tm * REC_STAT_ROWS, V7X_LANES), F32)]
    out_specs = [
        pl.BlockSpec((tm, ATTN_WIDTH), tok),
        pl.BlockSpec((tm, KV_WIDTH), tok),
        pl.BlockSpec((tm, 2 * KV_WIDTH), tok),
    ] + [pl.BlockSpec((tm, REC_WIDTH), tok)] * 5 + [pl.BlockSpec((REC_STAT_ROWS, V7X_LANES), tok)]
    return pl.pallas_call(
        functools.partial(_mix_in_kernel, layer=layer, sub_rows=MIX_IN_SUB_ROWS),
        grid=(n_tok // tm,),
        in_specs=[
            pl.BlockSpec((tm, d), tok),
            pl.BlockSpec((1, d), const),
            pl.BlockSpec((d, d_in), const),
            pl.BlockSpec((1, HEAD_DIM), const),
            pl.BlockSpec((1, HEAD_DIM), const),
            pl.BlockSpec((tm, HEAD_DIM), table),
            pl.BlockSpec((tm, HEAD_DIM), table),
            pl.BlockSpec(lb_logits.shape, lambda i: (0, 0, 0)),
            pl.BlockSpec(chunk_sel.shape, const),
        ],
        out_specs=out_specs,
        out_shape=out_shape,
        compiler_params=pltpu.CompilerParams(
            dimension_semantics=("parallel",), vmem_limit_bytes=VMEM_LIMIT),
        name="mix_in",
    )(x2d, gain.reshape(1, d), w_in, q_gain.reshape(1, HEAD_DIM), k_gain.reshape(1, HEAD_DIM), cos, sin,
      lb_logits, chunk_sel)


def _rec_constants():
    c = REC_CHUNK
    t = np.arange(c)[:, None]
    u = np.arange(c)[None, :]
    blocks = [(u <= t)]
    masks = []
    for lvl in range(REC_LEVELS):
        h = 1 << lvl
        base_t = t - t % (2 * h)
        mid = base_t + h
        upper = t >= mid
        blocks.append(np.where(upper, (u >= mid) & (u <= t), (u > t) & (u < mid)))
        masks.append((t // (2 * h) == u // (2 * h)) & (t % (2 * h) >= h) & (u % (2 * h) < h))
    blocks.append(u > t)
    masks.append(t == u)
    fwd = np.concatenate([b.astype(np.float32) for b in blocks], axis=0)
    fwd_masks = np.stack([m.astype(np.float32) for m in masks])
    n_rows = fwd.shape[0]
    bwd = fwd.reshape(-1, c, c)[:, ::-1, ::-1].reshape(n_rows, c)
    bwd_masks = fwd_masks[:, ::-1, ::-1]
    total = np.ones((8, c), np.float32)
    mats = np.stack([np.concatenate([fwd, total]), np.concatenate([bwd, total])])
    mats = np.concatenate([mats, mats], axis=-1)
    return mats, np.stack([fwd_masks, bwd_masks])


def _split2(x):
    hi = x.astype(BF16)
    lo = (x - hi.astype(F32)).astype(BF16)
    return jnp.concatenate([hi, lo], axis=0)


def _rec_robust(d, rows, q_ref, v_ref, g_ref, o_ref, mat_ref, mask_ref, state_scr):
    c = REC_CHUNK
    g = g_ref[0, rows, :]
    kk = 1.0 - jnp.exp2(g)
    sums = _dot(mat_ref[d], _split2(g))
    q = q_ref[0, rows, :]
    v = v_ref[0, rows, :].astype(BF16)
    for hh in range(N_REC_HEADS):
        sl = slice(hh * REC_HEAD_DIM, (hh + 1) * REC_HEAD_DIM)
        qh = q[:, sl]
        kh = kk[:, sl]
        vh = v[:, sl]
        scores = mask_ref[d, REC_LEVELS] * _dot_nt(qh.astype(BF16), kh.astype(BF16))
        for lvl in range(REC_LEVELS):
            el = jnp.exp2(sums[(lvl + 1) * c:(lvl + 2) * c, sl])
            scores += mask_ref[d, lvl] * _dot_nt((qh * el).astype(BF16), (kh * el).astype(BF16))
        cum = sums[0:c, sl]
        rem = sums[(REC_LEVELS + 1) * c:(REC_LEVELS + 2) * c, sl]
        tot = sums[(REC_LEVELS + 2) * c:(REC_LEVELS + 2) * c + 1, sl]
        state_t = state_scr[d * N_REC_HEADS + hh]
        inter = _dot_nt((qh * jnp.exp2(cum)).astype(BF16), state_t.astype(BF16))
        o_ref[0, rows, sl] = (inter + _dot(scores.astype(BF16), vh)).astype(o_ref.dtype)
        k_tail = (kh * jnp.exp2(rem)).astype(BF16)
        state_scr[d * N_REC_HEADS + hh] = state_t * jnp.exp2(tot) + _dot_tn(vh, k_tail)


def _rec_fast_prep(d, rows, q_ref, v_ref, g_ref, cmat_ref, tri_ref):
    c = REC_CHUNK
    g = g_ref[0, rows, :]
    kk = 1.0 - jnp.exp2(g)
    cs = _dot(cmat_ref[d], _split2(g))
    cum = cs[0:c]
    tot = cs[c:c + 1]
    half = 0.5 * tot
    e_half = jnp.exp2(half)
    k_s = kk * jnp.exp2(half - cum)
    return dict(
        q_s=(q_ref[0, rows, :] * jnp.exp2(cum - half)).astype(BF16),
        k_s=k_s.astype(BF16),
        k_tail=(k_s * e_half).astype(BF16),
        e_half=e_half, e_tot=jnp.exp2(tot),
        v=v_ref[0, rows, :].astype(BF16), causal=tri_ref[d] > 0.0)


def _rec_fast_matmuls(prep, n_dirs, state_scr):
    heads = [(d, hh) for d in range(n_dirs) for hh in range(N_REC_HEADS)]
    lanes = lambda hh: slice(hh * REC_HEAD_DIM, (hh + 1) * REC_HEAD_DIM)
    updates = [{(d, hh): _dot_tn(pj[d]["v"][:, lanes(hh)], pj[d]["k_tail"][:, lanes(hh)]) for d, hh in heads}
               for pj in prep]
    state = {(d, hh): state_scr[d * N_REC_HEADS + hh] for d, hh in heads}
    res = []
    for pj, uj in zip(prep, updates):
        res_j = {}
        for d, hh in heads:
            p, sl = pj[d], lanes(hh)
            rhs = jnp.concatenate([(state[d, hh] * p["e_half"][:, sl]).astype(BF16), p["k_s"][:, sl]], axis=0)
            res_j[d, hh] = _dot_nt(p["q_s"][:, sl], rhs)
            state[d, hh] = state[d, hh] * p["e_tot"][:, sl] + uj[d, hh]
        res.append(res_j)
    for d, hh in heads:
        state_scr[d * N_REC_HEADS + hh] = state[d, hh]
    return res


def _rec_fast_outputs(prep, rows_of, res, o_refs):
    heads = [(d, hh) for d in range(len(o_refs)) for hh in range(N_REC_HEADS)]
    lanes = lambda hh: slice(hh * REC_HEAD_DIM, (hh + 1) * REC_HEAD_DIM)
    for pj, rows_j, res_j in zip(prep, rows_of, res):
        for d, hh in heads:
            p, sl = pj[d], lanes(hh)
            scores = jnp.where(p["causal"], res_j[d, hh][:, REC_HEAD_DIM:], 0.0)
            o = res_j[d, hh][:, :REC_HEAD_DIM] + _dot(scores.astype(BF16), p["v"][:, sl])
            o_refs[d][0, rows_j[d], sl] = o.astype(o_refs[d].dtype)


def _attn_rec_kernel(flag_ref, q_ref, k_ref, v_ref, wsrc_ref,
                     qf_ref, qb_ref, vf_ref, vb_ref, gf_ref, gb_ref, mat_ref, mask_ref, cmat_ref, tri_ref,
                     o_ref, wdst_ref, of_ref, ob_ref,
                     q2_scr, s_scr, p_scr, mt_scr, m_scr, acc_scr, state_scr):
    b, h, qi = pl.program_id(0), pl.program_id(1), pl.program_id(2)
    n_q = pl.num_programs(2)
    i = h * n_q + qi
    n_steps = N_KV_HEADS * n_q
    per_step = REC_STEP_CHUNKS
    tq = q_ref.shape[1]
    rows = KV_GROUPS * tq
    tk = s_scr.shape[2]
    n_kv = k_ref.shape[1] // tk

    @pl.when(i == 0)
    def _():
        state_scr[...] = jnp.zeros_like(state_scr)

    _cast_blocks([wsrc_ref], [wdst_ref])
    for g in range(KV_GROUPS):
        q2_scr[g * tq:(g + 1) * tq, :] = q_ref[0, :, g * HEAD_DIM:(g + 1) * HEAD_DIM]

    fwd_chunk0 = (b * n_steps + i) * per_step
    bwd_chunk0 = (b * n_steps + n_steps - 1 - i) * per_step
    fast = flag_ref[fwd_chunk0 * 2] == 1
    for j in range(per_step):
        if j:
            fast = jnp.logical_and(fast, flag_ref[(fwd_chunk0 + j) * 2] == 1)
        fast = jnp.logical_and(fast, flag_ref[(bwd_chunk0 + j) * 2 + 1] == 1)
    dirs = ((qf_ref, vf_ref, gf_ref, of_ref), (qb_ref, vb_ref, gb_ref, ob_ref))
    chunk_rows = lambda j: slice(j * REC_CHUNK, (j + 1) * REC_CHUNK)
    rows_of = [(chunk_rows(j), chunk_rows(per_step - 1 - j)) for j in range(per_step)]

    def scores(j):
        s = _dot_nt(q2_scr[...], k_ref[0, j * tk:(j + 1) * tk, :])
        s_scr[j % 2] = s
        mt_scr[j % 2] = jnp.broadcast_to(jnp.max(s, axis=-1, keepdims=True), (rows, V7X_LANES))

    def accumulate(j):
        slot = j % 2
        m_prev = m_scr[...]
        m_new = jnp.maximum(m_prev, mt_scr[slot])
        m_scr[...] = m_new
        for r in range(0, rows, ATTN_ROW_CHUNK):
            m_r = m_new[r:r + ATTN_ROW_CHUNK]
            for c in range(0, tk, V7X_LANES):
                s_piece = s_scr[slot, r:r + ATTN_ROW_CHUNK, c:c + V7X_LANES]
                p_scr[slot, r:r + ATTN_ROW_CHUNK, c:c + V7X_LANES] = jnp.exp2(s_piece - m_r).astype(BF16)
        alpha = jnp.exp2(m_prev - m_new)
        pv = _dot(p_scr[slot], v_ref[0, j * tk:(j + 1) * tk, :])
        for half in range(2):
            sl = slice(half * HEAD_DIM, (half + 1) * HEAD_DIM)
            acc_scr[:, sl] = alpha * acc_scr[:, sl] + pv[:, sl]

    def attention(between_tiles):
        m_scr[...] = jnp.full_like(m_scr, -jnp.inf)
        acc_scr[...] = jnp.zeros_like(acc_scr)
        hooks = list(between_tiles)
        scores(0)
        if hooks:
            hooks.pop(0)()
        for j in range(n_kv):
            if j + 1 < n_kv:
                scores(j + 1)
            accumulate(j)
            if hooks:
                hooks.pop(0)()
        assert not hooks

    @pl.when(fast)
    def _():
        carry = {}

        def prepare():
            carry["prep"] = [[_rec_fast_prep(d, rows_of[j][d], q_ref_, v_ref_, g_ref_, cmat_ref, tri_ref)
                              for d, (q_ref_, v_ref_, g_ref_, _) in enumerate(dirs)] for j in range(per_step)]

        def recur():
            carry["res"] = _rec_fast_matmuls(carry["prep"], len(dirs), state_scr)

        def emit():
            _rec_fast_outputs(carry["prep"], rows_of, carry["res"], (of_ref, ob_ref))

        attention([prepare, recur, emit])

    @pl.when(jnp.logical_not(fast))
    def _():
        def robust():
            for j in range(per_step):
                for d, (q_ref_, v_ref_, g_ref_, o_ref_) in enumerate(dirs):
                    _rec_robust(d, rows_of[j][d], q_ref_, v_ref_, g_ref_, o_ref_, mat_ref, mask_ref, state_scr)

        attention([robust])

    o = (acc_scr[:, 0:HEAD_DIM] / acc_scr[:, HEAD_DIM:2 * HEAD_DIM]).astype(o_ref.dtype)
    for g in range(KV_GROUPS):
        o_ref[0, :, g * HEAD_DIM:(g + 1) * HEAD_DIM] = o[g * tq:(g + 1) * tq]


def _attention_and_hgrn2(q, k, v_ext, cast_weight, rq, g_f, g_b, ri, fast_flags):
    bsz, seq_len, _ = q.shape
    tq, tk = ATTN_Q_TOKENS, ATTN_KV_TOKENS
    gw = KV_GROUPS * HEAD_DIM
    rows = KV_GROUPS * tq
    n_q = seq_len // tq
    n_steps = N_KV_HEADS * n_q
    width = rq.shape[-1]
    c = REC_CHUNK
    assert n_steps * c * REC_STEP_CHUNKS == seq_len and seq_len // tk >= 3
    mats, masks = _rec_constants()
    n_lvl_rows = (REC_LEVELS + 2) * c
    cmat = np.concatenate([mats[:, 0:c], mats[:, n_lvl_rows:]], axis=1)
    tri = masks.sum(axis=1)
    mats, cmat = jnp.asarray(mats, BF16), jnp.asarray(cmat, BF16)
    masks, tri = jnp.asarray(masks, F32), jnp.asarray(tri, F32)
    step = lambda b, h, qi, flags: ((b * N_KV_HEADS + h) * n_q + qi, 0)
    fwd = lambda b, h, qi, flags: (b, h * n_q + qi, 0)
    bwd = lambda b, h, qi, flags: (b, n_steps - 1 - (h * n_q + qi), 0)
    blk = (1, c * REC_STEP_CHUNKS, width)
    full3 = lambda b, h, qi, flags: (0, 0, 0)
    full4 = lambda b, h, qi, flags: (0, 0, 0, 0)
    cast_specs, cast_shapes = _cast_specs([cast_weight], bsz * n_steps, step)
    grid_spec = pltpu.PrefetchScalarGridSpec(
        num_scalar_prefetch=1,
        grid=(bsz, N_KV_HEADS, n_q),
        in_specs=[
            pl.BlockSpec((1, tq, gw), lambda b, h, qi, flags: (b, qi, h)),
            pl.BlockSpec((1, seq_len, HEAD_DIM), lambda b, h, qi, flags: (b, 0, h)),
            pl.BlockSpec((1, seq_len, 2 * HEAD_DIM), lambda b, h, qi, flags: (b, 0, h)),
        ] + cast_specs + [
            pl.BlockSpec(blk, fwd), pl.BlockSpec(blk, bwd),
            pl.BlockSpec(blk, fwd), pl.BlockSpec(blk, bwd),
            pl.BlockSpec(blk, fwd), pl.BlockSpec(blk, bwd),
            pl.BlockSpec(mats.shape, full3),
            pl.BlockSpec(masks.shape, full4),
            pl.BlockSpec(cmat.shape, full3),
            pl.BlockSpec(tri.shape, full3),
        ],
        out_specs=[pl.BlockSpec((1, tq, gw), lambda b, h, qi, flags: (b, qi, h))] + cast_specs
        + [pl.BlockSpec(blk, fwd), pl.BlockSpec(blk, bwd)],
        scratch_shapes=[
            pltpu.VMEM((rows, HEAD_DIM), BF16),
            pltpu.VMEM((2, rows, tk), F32),
            pltpu.VMEM((2, rows, tk), BF16),
            pltpu.VMEM((2, rows, V7X_LANES), F32),
            pltpu.VMEM((rows, V7X_LANES), F32),
            pltpu.VMEM((rows, 2 * HEAD_DIM), F32),
            pltpu.VMEM((2 * N_REC_HEADS, REC_HEAD_DIM, REC_HEAD_DIM), F32),
        ],
    )
    return pl.pallas_call(
        _attn_rec_kernel,
        grid_spec=grid_spec,
        out_shape=[jax.ShapeDtypeStruct((bsz, seq_len, ATTN_WIDTH), BF16)] + cast_shapes
        + [jax.ShapeDtypeStruct((bsz, seq_len, width), BF16)] * 2,
        compiler_params=pltpu.CompilerParams(
            dimension_semantics=("parallel", "arbitrary", "arbitrary"),
            vmem_limit_bytes=VMEM_LIMIT),
        name="attn_hgrn2",
    )(fast_flags, q, k, v_ext, cast_weight, rq, rq, ri, ri, g_f, g_b, mats, masks, cmat, tri)


def kernel(x, ffn1_norm, ffn1_w_in, ffn1_w_out, mix_norm, w_in_mix, attn_q_norm, attn_k_norm,
           attn_out_norm, rec_lb_logits, rec_out_norm, w_out_mix, ffn2_norm, ffn2_w_in,
           ffn2_w_out, final_norm):
    bsz, seq_len, d = x.shape
    depth = ffn1_norm.shape[0]
    n_tok = bsz * seq_len
    h = x.reshape(n_tok, d)
    to3 = lambda a: a.reshape(bsz, seq_len, a.shape[-1])
    to2 = lambda a: a.reshape(n_tok, a.shape[-1])
    for l in range(depth):
        h, w_in_mix_bf, w_out_mix_bf, ffn2_w_out_bf = _ffn(
            h, ffn1_norm[l], ffn1_w_in[l].astype(BF16), ffn1_w_out[l].astype(BF16),
            tokens=FFN_TOKENS, sub_rows=FFN_SUB_ROWS, cast_weights=(w_in_mix[l], w_out_mix[l], ffn2_w_out[l]))
        q, k, v_ext, rq, g_f, g_b, ri, rg, chunk_decay = _mix_in(
            h, mix_norm[l], w_in_mix_bf, attn_q_norm[l], attn_k_norm[l],
            rec_lb_logits, l, seq_len)
        decay = chunk_decay.reshape(-1, REC_STAT_ROWS, V7X_LANES)[:, :MIX_IN_TOKENS // REC_CHUNK, :2]
        fast_flags = (decay <= REC_FAST_RANGE * LOG2_E).astype(jnp.int32).reshape(-1)
        attn, ffn2_w_in_bf, o_f, o_b = _attention_and_hgrn2(
            to3(q), to3(k), to3(v_ext), ffn2_w_in[l], to3(rq), to3(g_f), to3(g_b), to3(ri), fast_flags)
        mixer_args = (to2(attn), to2(o_f), to2(o_b), rg, attn_out_norm[l], rec_out_norm[l], w_out_mix_bf)
        h, = _ffn(h, ffn2_norm[l], ffn2_w_in_bf, ffn2_w_out_bf,
                  tokens=FFN_MIX_TOKENS, sub_rows=FFN_MIX_SUB_ROWS, mixer_args=mixer_args,
                  final_gain=final_norm[l])
    return h.reshape(bsz, seq_len, d)
```

```python
import functools

import jax
import jax.numpy as jnp
import numpy as np
from jax import lax
from jax.experimental import pallas as pl
from jax.experimental.pallas import tpu as pltpu

F32 = jnp.float32
BF16 = jnp.bfloat16

EPS = 1e-6
GRID_W = 64
HEAD_DIM = 128
N_Q_HEADS = 4
N_KV_HEADS = 2
KV_GROUPS = N_Q_HEADS // N_KV_HEADS
ATTN_WIDTH = N_Q_HEADS * HEAD_DIM
KV_WIDTH = N_KV_HEADS * HEAD_DIM
ROPE_THETA = 10000.0
ROPE_AXIS_DIM = HEAD_DIM // 2
REC_HEAD_DIM = 128
N_REC_HEADS = 4
REC_WIDTH = N_REC_HEADS * REC_HEAD_DIM

V7X_LANES = 128

FFN_TOKENS = 1024
FFN_SUB_ROWS = 512
FFN_MIX_TOKENS = 1024
FFN_MIX_SUB_ROWS = 512
FFN_FF_CHUNK = 1024
MIX_IN_TOKENS = 512
MIX_IN_SUB_ROWS = 256
ATTN_Q_TOKENS = 512
ATTN_KV_TOKENS = 2048
ATTN_ROW_CHUNK = 128
LOG2_E = 1.4426950408889634
REC_CHUNK = 128
REC_STEP_CHUNKS = 4
REC_LEVELS = 7
REC_STAT_ROWS = 8
REC_FAST_RANGE = 150.0

VMEM_LIMIT = 56 * 1024 * 1024


def _rms(x, gain):
    return x * lax.rsqrt(jnp.mean(x * x, axis=-1, keepdims=True) + EPS) * gain


def _dot(a, b):
    return jnp.dot(a, b, preferred_element_type=F32)


def _dot_nt(a, b):
    return lax.dot_general(a, b, (((1,), (1,)), ((), ())), preferred_element_type=F32)


def _dot_tn(a, b):
    return lax.dot_general(a, b, (((0,), (0,)), ((), ())), preferred_element_type=F32)


def _mixer_out_rows(rows, attn_ref, of_ref, ob_ref, rg_ref, ag_ref, rgain_ref, wmix_ref):
    parts = [_rms(attn_ref[rows, :].astype(F32), ag_ref[...]).astype(BF16)]
    for hh in range(N_REC_HEADS):
        sl = slice(hh * REC_HEAD_DIM, (hh + 1) * REC_HEAD_DIM)
        gate = rg_ref[rows, sl].astype(F32)
        o = of_ref[rows, sl].astype(F32) + ob_ref[rows, sl].astype(F32)
        parts.append((_rms(o, rgain_ref[...]) * (gate * jax.nn.sigmoid(gate))).astype(BF16))
    return _dot(jnp.concatenate(parts, axis=-1), wmix_ref[...])


def _cast_blocks(src_refs, dst_refs):
    for src_ref, dst_ref in zip(src_refs, dst_refs):
        dst_ref[...] = src_ref[...].astype(dst_ref.dtype)


def _ffn_kernel(*refs, mixer, final_norm, sub_rows, ff_chunks, n_cast):
    refs = list(refs)
    x_ref = refs.pop(0)
    mix_refs = [refs.pop(0) for _ in range(7)] if mixer else None
    gain_ref, win_ref, wout_ref = refs.pop(0), refs.pop(0), refs.pop(0)
    fgain_ref = refs.pop(0) if final_norm else None
    cast_src = [refs.pop(0) for _ in range(n_cast)]
    out_ref = refs.pop(0)
    cast_dst = [refs.pop(0) for _ in range(n_cast)]
    h_scr, acc_scr = refs
    _cast_blocks(cast_src, cast_dst)
    d_ff = wout_ref.shape[0]
    n_sub = x_ref.shape[0] // sub_rows
    res_ref = out_ref if mixer else x_ref
    sub_slice = lambda sub: slice(sub * sub_rows, (sub + 1) * sub_rows)

    def prologue(sub):
        rows = sub_slice(sub)
        x = x_ref[rows, :]
        if mixer:
            x = x + _mixer_out_rows(rows, *mix_refs)
            out_ref[rows, :] = x
        h_scr[sub] = _rms(x, gain_ref[...]).astype(BF16)

    def epilogue(sub):
        rows = sub_slice(sub)
        y = res_ref[rows, :] + 0.5 * acc_scr[sub]
        if final_norm:
            y = _rms(y, fgain_ref[...])
        out_ref[rows, :] = y

    prologue(0)
    for sub in range(n_sub):
        h = h_scr[sub]
        c0 = 0
        for width in ff_chunks:
            gate = _dot(h, win_ref[:, c0:c0 + width])
            up = _dot(h, win_ref[:, d_ff + c0:d_ff + c0 + width])
            act = ((gate * jax.nn.sigmoid(gate)) * up).astype(BF16)
            contrib = _dot(act, wout_ref[c0:c0 + width, :])
            if c0 == 0:
                acc_scr[sub] = contrib
                if sub + 1 < n_sub:
                    prologue(sub + 1)
                if sub >= 1:
                    epilogue(sub - 1)
            else:
                acc_scr[sub] += contrib
            c0 += width
    epilogue(n_sub - 1)


def _ffn_chunks(d_ff):
    n_full, rest = divmod(d_ff, FFN_FF_CHUNK)
    assert rest % V7X_LANES == 0
    return (FFN_FF_CHUNK,) * n_full + ((rest,) if rest else ())


def _cast_specs(weights, n_steps, step_index):
    specs, shapes = [], []
    for w in weights:
        rows = w.shape[0] // n_steps
        assert rows * n_steps == w.shape[0] and rows % 16 == 0
        specs.append(pl.BlockSpec((rows, w.shape[1]), step_index))
        shapes.append(jax.ShapeDtypeStruct(w.shape, BF16))
    return specs, shapes


def _ffn(x2d, gain, w_in, w_out, *, tokens, sub_rows, mixer_args=None, final_gain=None, cast_weights=()):
    n_tok, d = x2d.shape
    d_ff = w_out.shape[0]
    tok = lambda i: (i, 0)
    const = lambda i: (0, 0)
    resident = lambda shape: pl.BlockSpec(shape, const, pipeline_mode=pl.Buffered(1))
    in_specs = [pl.BlockSpec((tokens, d), tok)]
    args = [x2d]
    if mixer_args is not None:
        attn, o_f, o_b, rg, attn_gain, rec_gain, w_mix = mixer_args
        in_specs += [pl.BlockSpec((tokens, ATTN_WIDTH), tok)] + [pl.BlockSpec((tokens, REC_WIDTH), tok)] * 3
        in_specs += [pl.BlockSpec((1, ATTN_WIDTH), const), pl.BlockSpec((1, REC_HEAD_DIM), const),
                     resident(w_mix.shape)]
        args += [attn, o_f, o_b, rg, attn_gain.reshape(1, ATTN_WIDTH), rec_gain.reshape(1, REC_HEAD_DIM), w_mix]
    in_specs += [pl.BlockSpec((1, d), const), resident(w_in.shape), resident(w_out.shape)]
    args += [gain.reshape(1, d), w_in, w_out]
    if final_gain is not None:
        in_specs.append(pl.BlockSpec((1, d), const))
        args.append(final_gain.reshape(1, d))
    cast_specs, cast_shapes = _cast_specs(cast_weights, n_tok // tokens, tok)
    in_specs += cast_specs
    args += list(cast_weights)
    body = functools.partial(
        _ffn_kernel, mixer=mixer_args is not None, final_norm=final_gain is not None,
        sub_rows=sub_rows, ff_chunks=_ffn_chunks(d_ff), n_cast=len(cast_weights))
    return pl.pallas_call(
        body,
        grid=(n_tok // tokens,),
        in_specs=in_specs,
        out_specs=[pl.BlockSpec((tokens, d), tok)] + cast_specs,
        out_shape=[jax.ShapeDtypeStruct((n_tok, d), F32)] + cast_shapes,
        scratch_shapes=[pltpu.VMEM((tokens // sub_rows, sub_rows, d), BF16),
                        pltpu.VMEM((tokens // sub_rows, sub_rows, d), F32)],
        compiler_params=pltpu.CompilerParams(
            dimension_semantics=("parallel",), vmem_limit_bytes=VMEM_LIMIT),
        name="ffn_mix" if mixer_args is not None else "ffn",
    )(*args)


def _rope(x, cos, sin_signed, first_half):
    swapped = jnp.where(first_half, pltpu.roll(x, 96, 1), pltpu.roll(x, 32, 1))
    return x * cos + swapped * sin_signed


def _mix_in_kernel(x_ref, gain_ref, w_ref, qg_ref, kg_ref, cos_ref, sin_ref, lbl_ref, csel_ref,
                   q_ref, k_ref, v_ref, rq_ref, gf_ref, gb_ref, ri_ref, rg_ref, ctot_ref, *, layer, sub_rows):
    col = {}
    c = 0
    for name, width in (("aq", ATTN_WIDTH), ("ak", KV_WIDTH), ("av", KV_WIDTH), ("rq", REC_WIDTH),
                        ("zf", REC_WIDTH), ("zb", REC_WIDTH), ("ri", REC_WIDTH), ("rg", REC_WIDTH)):
        col[name] = (c, width)
        c += width
    lane = lax.broadcasted_iota(jnp.int32, (sub_rows, HEAD_DIM), 1)
    first_half = (lane % (ROPE_AXIS_DIM)) < (ROPE_AXIS_DIM // 2)
    chunk_lane = lax.broadcasted_iota(jnp.int32, ctot_ref.shape, 1)
    ctot = jnp.zeros(ctot_ref.shape, F32)
    gates = []

    for sub in range(x_ref.shape[0] // sub_rows):
        rows = slice(sub * sub_rows, (sub + 1) * sub_rows)
        h = _rms(x_ref[rows, :], gain_ref[...]).astype(BF16)
        cos = cos_ref[rows, :]
        sin = sin_ref[rows, :]

        def proj(name):
            c0, width = col[name]
            return _dot(h, w_ref[:, c0:c0 + width])

        for d, (name, g_ref) in enumerate((("zf", gf_ref), ("zb", gb_ref))):
            logits = lbl_ref[d]
            e = jnp.exp(logits - jnp.max(logits, axis=0, keepdims=True))
            lb = jnp.sum(e[:layer + 1], axis=0, keepdims=True) / jnp.sum(e, axis=0, keepdims=True)
            g = jnp.log2(lb + (1.0 - lb) * jax.nn.sigmoid(proj(name)))
            g_ref[rows, :] = g
            gates.append((d, rows, g.astype(BF16)))

        aq = proj("aq")
        for hh in range(N_Q_HEADS):
            sl = slice(hh * HEAD_DIM, (hh + 1) * HEAD_DIM)
            qh = _rope(_rms(aq[:, sl], qg_ref[...]), cos, sin, first_half) * (HEAD_DIM ** -0.5 * LOG2_E)
            q_ref[rows, sl] = qh.astype(BF16)
        ak = proj("ak")
        for hh in range(N_KV_HEADS):
            sl = slice(hh * HEAD_DIM, (hh + 1) * HEAD_DIM)
            k_ref[rows, sl] = _rope(_rms(ak[:, sl], kg_ref[...]), cos, sin, first_half).astype(BF16)
        av = proj("av").astype(BF16)
        for hh in range(N_KV_HEADS):
            v_ref[rows, 2 * hh * HEAD_DIM:(2 * hh + 1) * HEAD_DIM] = av[:, hh * HEAD_DIM:(hh + 1) * HEAD_DIM]
            v_ref[rows, (2 * hh + 1) * HEAD_DIM:(2 * hh + 2) * HEAD_DIM] = jnp.ones((sub_rows, HEAD_DIM), BF16)
        rq_ref[rows, :] = proj("rq").astype(rq_ref.dtype)
        ri_ref[rows, :] = proj("ri").astype(ri_ref.dtype)
        rg_ref[rows, :] = proj("rg").astype(rg_ref.dtype)
    for d, rows, g_bf16 in gates:
        chunk_sum = _dot(csel_ref[:, rows], g_bf16)
        ctot = jnp.maximum(ctot, jnp.where(chunk_lane == d, jnp.max(-chunk_sum, axis=-1, keepdims=True), 0.0))
    ctot_ref[...] = ctot


def _rope_tables(seq_len):
    rows = seq_len // GRID_W
    inv_freq = ROPE_THETA ** (-jnp.arange(0, ROPE_AXIS_DIM, 2, dtype=F32) / ROPE_AXIS_DIM)
    ang_r = jnp.arange(rows, dtype=F32)[:, None] * inv_freq[None, :]
    ang_c = jnp.arange(GRID_W, dtype=F32)[:, None] * inv_freq[None, :]
    half = ROPE_AXIS_DIM // 2

    def expand(row_part, col_part):
        r = jnp.broadcast_to(row_part[:, None, :], (rows, GRID_W, 2 * half))
        c = jnp.broadcast_to(col_part[None, :, :], (rows, GRID_W, 2 * half))
        return jnp.concatenate([r, c], axis=-1).reshape(seq_len, 4 * half)

    cos = expand(jnp.concatenate([jnp.cos(ang_r)] * 2, -1), jnp.concatenate([jnp.cos(ang_c)] * 2, -1))
    sin = expand(jnp.concatenate([-jnp.sin(ang_r), jnp.sin(ang_r)], -1),
                 jnp.concatenate([-jnp.sin(ang_c), jnp.sin(ang_c)], -1))
    return cos, sin


def _mix_in(x2d, gain, w_in, q_gain, k_gain, lb_logits, layer, seq_len):
    n_tok, d = x2d.shape
    d_in = w_in.shape[1]
    tm = MIX_IN_TOKENS
    cos, sin = _rope_tables(seq_len)
    chunks = tm // REC_CHUNK
    assert chunks <= REC_STAT_ROWS
    chunk_sel = np.zeros((REC_STAT_ROWS, tm), np.float32)
    chunk_sel[:chunks] = np.repeat(np.eye(chunks, dtype=np.float32), REC_CHUNK, axis=1)
    chunk_sel = jnp.asarray(chunk_sel, BF16)
    tiles_per_seq = seq_len // tm
    tok = lambda i: (i, 0)
    const = lambda i: (0, 0)
    table = lambda i: (i % tiles_per_seq, 0)
    out_shape = [
        jax.ShapeDtypeStruct((n_tok, ATTN_WIDTH), BF16),
        jax.ShapeDtypeStruct((n_tok, KV_WIDTH), BF16),
        jax.ShapeDtypeStruct((n_tok, 2 * KV_WIDTH), BF16),
    ] + [jax.ShapeDtypeStruct((n_tok, REC_WIDTH), dt) for dt in (BF16, F32, F32, BF16, BF16)] + [
        jax.ShapeDtypeStruct((n_tok // tm * REC_STAT_ROWS, V7X_LANES), F32)]
    out_specs = [
        pl.BlockSpec((tm, ATTN_WIDTH), tok),
        pl.BlockSpec((tm, KV_WIDTH), tok),
        pl.BlockSpec((tm, 2 * KV_WIDTH), tok),
    ] + [pl.BlockSpec((tm, REC_WIDTH), tok)] * 5 + [pl.BlockSpec((REC_STAT_ROWS, V7X_LANES), tok)]
    return pl.pallas_call(
        functools.partial(_mix_in_kernel, layer=layer, sub_rows=MIX_IN_SUB_ROWS),
        grid=(n_tok // tm,),
        in_specs=[
            pl.BlockSpec((tm, d), tok),
            pl.BlockSpec((1, d), const),
            pl.BlockSpec((d, d_in), const),
            pl.BlockSpec((1, HEAD_DIM), const),
            pl.BlockSpec((1, HEAD_DIM), const),
            pl.BlockSpec((tm, HEAD_DIM), table),
            pl.BlockSpec((tm, HEAD_DIM), table),
            pl.BlockSpec(lb_logits.shape, lambda i: (0, 0, 0)),
            pl.BlockSpec(chunk_sel.shape, const),
        ],
        out_specs=out_specs,
        out_shape=out_shape,
        compiler_params=pltpu.CompilerParams(
            dimension_semantics=("parallel",), vmem_limit_bytes=VMEM_LIMIT),
        name="mix_in",
    )(x2d, gain.reshape(1, d), w_in, q_gain.reshape(1, HEAD_DIM), k_gain.reshape(1, HEAD_DIM), cos, sin,
      lb_logits, chunk_sel)


def _attn_kernel(q_ref, k_ref, v_ref, wsrc_ref, o_ref, wdst_ref, q2_scr, s_scr, p_scr, mt_scr, m_scr, acc_scr):
    _cast_blocks([wsrc_ref], [wdst_ref])
    tq = q_ref.shape[1]
    rows = KV_GROUPS * tq
    tk = s_scr.shape[2]
    n_kv = k_ref.shape[1] // tk

    for g in range(KV_GROUPS):
        q2_scr[g * tq:(g + 1) * tq, :] = q_ref[0, :, g * HEAD_DIM:(g + 1) * HEAD_DIM]
    m_scr[...] = jnp.full_like(m_scr, -jnp.inf)
    acc_scr[...] = jnp.zeros_like(acc_scr)

    def scores(j):
        s = _dot_nt(q2_scr[...], k_ref[0, j * tk:(j + 1) * tk, :])
        s_scr[j % 2] = s
        mt_scr[j % 2] = jnp.broadcast_to(jnp.max(s, axis=-1, keepdims=True), (rows, V7X_LANES))

    def accumulate(j):
        slot = j % 2
        m_prev = m_scr[...]
        m_new = jnp.maximum(m_prev, mt_scr[slot])
        m_scr[...] = m_new
        for r in range(0, rows, ATTN_ROW_CHUNK):
            m_r = m_new[r:r + ATTN_ROW_CHUNK]
            for c in range(0, tk, V7X_LANES):
                s_piece = s_scr[slot, r:r + ATTN_ROW_CHUNK, c:c + V7X_LANES]
                p_scr[slot, r:r + ATTN_ROW_CHUNK, c:c + V7X_LANES] = jnp.exp2(s_piece - m_r).astype(BF16)
        alpha = jnp.exp2(m_prev - m_new)
        pv = _dot(p_scr[slot], v_ref[0, j * tk:(j + 1) * tk, :])
        for half in range(2):
            sl = slice(half * HEAD_DIM, (half + 1) * HEAD_DIM)
            acc_scr[:, sl] = alpha * acc_scr[:, sl] + pv[:, sl]

    scores(0)
    for j in range(n_kv):
        if j + 1 < n_kv:
            scores(j + 1)
        accumulate(j)

    o = (acc_scr[:, 0:HEAD_DIM] / acc_scr[:, HEAD_DIM:2 * HEAD_DIM]).astype(o_ref.dtype)
    for g in range(KV_GROUPS):
        o_ref[0, :, g * HEAD_DIM:(g + 1) * HEAD_DIM] = o[g * tq:(g + 1) * tq]


def _attention(q, k, v_ext, cast_weight):
    bsz, seq_len, _ = q.shape
    tq, tk = ATTN_Q_TOKENS, ATTN_KV_TOKENS
    gw = KV_GROUPS * HEAD_DIM
    rows = KV_GROUPS * tq
    n_q = seq_len // tq
    step = lambda b, h, qi: ((b * N_KV_HEADS + h) * n_q + qi, 0)
    cast_specs, cast_shapes = _cast_specs([cast_weight], bsz * N_KV_HEADS * n_q, step)
    return pl.pallas_call(
        _attn_kernel,
        grid=(bsz, N_KV_HEADS, n_q),
        in_specs=[
            pl.BlockSpec((1, tq, gw), lambda b, h, qi: (b, qi, h)),
            pl.BlockSpec((1, seq_len, HEAD_DIM), lambda b, h, qi: (b, 0, h)),
            pl.BlockSpec((1, seq_len, 2 * HEAD_DIM), lambda b, h, qi: (b, 0, h)),
        ] + cast_specs,
        out_specs=[pl.BlockSpec((1, tq, gw), lambda b, h, qi: (b, qi, h))] + cast_specs,
        out_shape=[jax.ShapeDtypeStruct((bsz, seq_len, ATTN_WIDTH), BF16)] + cast_shapes,
        scratch_shapes=[
            pltpu.VMEM((rows, HEAD_DIM), BF16),
            pltpu.VMEM((2, rows, tk), F32),
            pltpu.VMEM((2, rows, tk), BF16),
            pltpu.VMEM((2, rows, V7X_LANES), F32),
            pltpu.VMEM((rows, V7X_LANES), F32),
            pltpu.VMEM((rows, 2 * HEAD_DIM), F32),
        ],
        compiler_params=pltpu.CompilerParams(
            dimension_semantics=("parallel", "parallel", "arbitrary"),
            vmem_limit_bytes=VMEM_LIMIT),
        name="attn",
    )(q, k, v_ext, cast_weight)


def _rec_constants():
    c = REC_CHUNK
    t = np.arange(c)[:, None]
    u = np.arange(c)[None, :]
    blocks = [(u <= t)]
    masks = []
    for lvl in range(REC_LEVELS):
        h = 1 << lvl
        base_t = t - t % (2 * h)
        mid = base_t + h
        upper = t >= mid
        blocks.append(np.where(upper, (u >= mid) & (u <= t), (u > t) & (u < mid)))
        masks.append((t // (2 * h) == u // (2 * h)) & (t % (2 * h) >= h) & (u % (2 * h) < h))
    blocks.append(u > t)
    masks.append(t == u)
    fwd = np.concatenate([b.astype(np.float32) for b in blocks], axis=0)
    fwd_masks = np.stack([m.astype(np.float32) for m in masks])
    n_rows = fwd.shape[0]
    bwd = fwd.reshape(-1, c, c)[:, ::-1, ::-1].reshape(n_rows, c)
    bwd_masks = fwd_masks[:, ::-1, ::-1]
    total = np.ones((8, c), np.float32)
    mats = np.stack([np.concatenate([fwd, total]), np.concatenate([bwd, total])])
    mats = np.concatenate([mats, mats], axis=-1)
    return mats, np.stack([fwd_masks, bwd_masks])


def _split2(x):
    hi = x.astype(BF16)
    lo = (x - hi.astype(F32)).astype(BF16)
    return jnp.concatenate([hi, lo], axis=0)


def _rec_robust(d, rows, q_ref, v_ref, g_ref, o_ref, mat_ref, mask_ref, state_scr):
    c = REC_CHUNK
    g = g_ref[0, rows, :]
    kk = 1.0 - jnp.exp2(g)
    sums = _dot(mat_ref[d], _split2(g))
    q = q_ref[0, rows, :]
    v = v_ref[0, rows, :].astype(BF16)
    for hh in range(N_REC_HEADS):
        sl = slice(hh * REC_HEAD_DIM, (hh + 1) * REC_HEAD_DIM)
        qh = q[:, sl]
        kh = kk[:, sl]
        vh = v[:, sl]
        scores = mask_ref[d, REC_LEVELS] * _dot_nt(qh.astype(BF16), kh.astype(BF16))
        for lvl in range(REC_LEVELS):
            el = jnp.exp2(sums[(lvl + 1) * c:(lvl + 2) * c, sl])
            scores += mask_ref[d, lvl] * _dot_nt((qh * el).astype(BF16), (kh * el).astype(BF16))
        cum = sums[0:c, sl]
        rem = sums[(REC_LEVELS + 1) * c:(REC_LEVELS + 2) * c, sl]
        tot = sums[(REC_LEVELS + 2) * c:(REC_LEVELS + 2) * c + 1, sl]
        state_t = state_scr[d * N_REC_HEADS + hh]
        inter = _dot_nt((qh * jnp.exp2(cum)).astype(BF16), state_t.astype(BF16))
        o_ref[0, rows, sl] = (inter + _dot(scores.astype(BF16), vh)).astype(o_ref.dtype)
        k_tail = (kh * jnp.exp2(rem)).astype(BF16)
        state_scr[d * N_REC_HEADS + hh] = state_t * jnp.exp2(tot) + _dot_tn(vh, k_tail)


def _rec_fast_prep(d, rows, q_ref, v_ref, g_ref, cmat_ref, tri_ref):
    c = REC_CHUNK
    g = g_ref[0, rows, :]
    kk = 1.0 - jnp.exp2(g)
    cs = _dot(cmat_ref[d], _split2(g))
    cum = cs[0:c]
    tot = cs[c:c + 1]
    half = 0.5 * tot
    e_half = jnp.exp2(half)
    k_s = kk * jnp.exp2(half - cum)
    return dict(
        q_s=(q_ref[0, rows, :] * jnp.exp2(cum - half)).astype(BF16),
        k_s=k_s.astype(BF16),
        k_tail=(k_s * e_half).astype(BF16),
        e_half=e_half, e_tot=jnp.exp2(tot),
        v=v_ref[0, rows, :].astype(BF16), causal=tri_ref[d] > 0.0)


def _rec_fast(prep, rows_of, o_refs, state_scr):
    heads = [(d, hh) for d in range(len(o_refs)) for hh in range(N_REC_HEADS)]
    lanes = lambda hh: slice(hh * REC_HEAD_DIM, (hh + 1) * REC_HEAD_DIM)
    updates = [{(d, hh): _dot_tn(pj[d]["v"][:, lanes(hh)], pj[d]["k_tail"][:, lanes(hh)]) for d, hh in heads}
               for pj in prep]
    state = {(d, hh): state_scr[d * N_REC_HEADS + hh] for d, hh in heads}
    res = []
    for pj, uj in zip(prep, updates):
        res_j = {}
        for d, hh in heads:
            p, sl = pj[d], lanes(hh)
            rhs = jnp.concatenate([(state[d, hh] * p["e_half"][:, sl]).astype(BF16), p["k_s"][:, sl]], axis=0)
            res_j[d, hh] = _dot_nt(p["q_s"][:, sl], rhs)
            state[d, hh] = state[d, hh] * p["e_tot"][:, sl] + uj[d, hh]
        res.append(res_j)
    for d, hh in heads:
        state_scr[d * N_REC_HEADS + hh] = state[d, hh]
    for pj, rows_j, res_j in zip(prep, rows_of, res):
        for d, hh in heads:
            p, sl = pj[d], lanes(hh)
            scores = jnp.where(p["causal"], res_j[d, hh][:, REC_HEAD_DIM:], 0.0)
            o = res_j[d, hh][:, :REC_HEAD_DIM] + _dot(scores.astype(BF16), p["v"][:, sl])
            o_refs[d][0, rows_j[d], sl] = o.astype(o_refs[d].dtype)


def _rec_kernel(flag_ref, qf_ref, qb_ref, vf_ref, vb_ref, gf_ref, gb_ref, mat_ref, mask_ref, cmat_ref, tri_ref,
                of_ref, ob_ref, state_scr):
    b = pl.program_id(0)
    i = pl.program_id(1)
    n_steps = pl.num_programs(1)
    per_step = REC_STEP_CHUNKS

    @pl.when(i == 0)
    def _():
        state_scr[...] = jnp.zeros_like(state_scr)

    fwd_chunk0 = (b * n_steps + i) * per_step
    bwd_chunk0 = (b * n_steps + n_steps - 1 - i) * per_step
    fast = flag_ref[fwd_chunk0 * 2] == 1
    for j in range(per_step):
        if j:
            fast = jnp.logical_and(fast, flag_ref[(fwd_chunk0 + j) * 2] == 1)
        fast = jnp.logical_and(fast, flag_ref[(bwd_chunk0 + j) * 2 + 1] == 1)
    dirs = ((qf_ref, vf_ref, gf_ref, of_ref), (qb_ref, vb_ref, gb_ref, ob_ref))
    chunk_rows = lambda j: slice(j * REC_CHUNK, (j + 1) * REC_CHUNK)
    rows_of = [(chunk_rows(j), chunk_rows(per_step - 1 - j)) for j in range(per_step)]

    @pl.when(fast)
    def _():
        prep = [[_rec_fast_prep(d, rows_of[j][d], q_ref, v_ref, g_ref, cmat_ref, tri_ref)
                 for d, (q_ref, v_ref, g_ref, _) in enumerate(dirs)] for j in range(per_step)]
        _rec_fast(prep, rows_of, (of_ref, ob_ref), state_scr)

    @pl.when(jnp.logical_not(fast))
    def _():
        for j in range(per_step):
            for d, (q_ref, v_ref, g_ref, o_ref) in enumerate(dirs):
                _rec_robust(d, rows_of[j][d], q_ref, v_ref, g_ref, o_ref, mat_ref, mask_ref, state_scr)


def _hgrn2(rq, g_f, g_b, ri, fast_flags):
    bsz, seq_len, width = rq.shape
    c = REC_CHUNK
    n_steps = seq_len // (c * REC_STEP_CHUNKS)
    mats, masks = _rec_constants()
    n_lvl_rows = (REC_LEVELS + 2) * c
    cmat = np.concatenate([mats[:, 0:c], mats[:, n_lvl_rows:]], axis=1)
    tri = masks.sum(axis=1)
    mats, cmat = jnp.asarray(mats, BF16), jnp.asarray(cmat, BF16)
    masks, tri = jnp.asarray(masks, F32), jnp.asarray(tri, F32)
    fwd = lambda b, i, flags: (b, i, 0)
    bwd = lambda b, i, flags: (b, n_steps - 1 - i, 0)
    blk = (1, c * REC_STEP_CHUNKS, width)
    full3 = lambda b, i, flags: (0, 0, 0)
    full4 = lambda b, i, flags: (0, 0, 0, 0)
    grid_spec = pltpu.PrefetchScalarGridSpec(
        num_scalar_prefetch=1,
        grid=(bsz, n_steps),
        in_specs=[
            pl.BlockSpec(blk, fwd), pl.BlockSpec(blk, bwd),
            pl.BlockSpec(blk, fwd), pl.BlockSpec(blk, bwd),
            pl.BlockSpec(blk, fwd), pl.BlockSpec(blk, bwd),
            pl.BlockSpec(mats.shape, full3),
            pl.BlockSpec(masks.shape, full4),
            pl.BlockSpec(cmat.shape, full3),
            pl.BlockSpec(tri.shape, full3),
        ],
        out_specs=[pl.BlockSpec(blk, fwd), pl.BlockSpec(blk, bwd)],
        scratch_shapes=[pltpu.VMEM((2 * N_REC_HEADS, REC_HEAD_DIM, REC_HEAD_DIM), F32)],
    )
    return pl.pallas_call(
        _rec_kernel,
        grid_spec=grid_spec,
        out_shape=[jax.ShapeDtypeStruct((bsz, seq_len, width), BF16)] * 2,
        compiler_params=pltpu.CompilerParams(
            dimension_semantics=("parallel", "arbitrary"), vmem_limit_bytes=VMEM_LIMIT),
        name="hgrn2",
    )(fast_flags, rq, rq, ri, ri, g_f, g_b, mats, masks, cmat, tri)


def kernel(x, ffn1_norm, ffn1_w_in, ffn1_w_out, mix_norm, w_in_mix, attn_q_norm, attn_k_norm,
           attn_out_norm, rec_lb_logits, rec_out_norm, w_out_mix, ffn2_norm, ffn2_w_in,
           ffn2_w_out, final_norm):
    bsz, seq_len, d = x.shape
    depth = ffn1_norm.shape[0]
    n_tok = bsz * seq_len
    h = x.reshape(n_tok, d)
    to3 = lambda a: a.reshape(bsz, seq_len, a.shape[-1])
    to2 = lambda a: a.reshape(n_tok, a.shape[-1])
    for l in range(depth):
        h, w_in_mix_bf, w_out_mix_bf, ffn2_w_out_bf = _ffn(
            h, ffn1_norm[l], ffn1_w_in[l].astype(BF16), ffn1_w_out[l].astype(BF16),
            tokens=FFN_TOKENS, sub_rows=FFN_SUB_ROWS, cast_weights=(w_in_mix[l], w_out_mix[l], ffn2_w_out[l]))
        q, k, v_ext, rq, g_f, g_b, ri, rg, chunk_decay = _mix_in(
            h, mix_norm[l], w_in_mix_bf, attn_q_norm[l], attn_k_norm[l],
            rec_lb_logits, l, seq_len)
        attn, ffn2_w_in_bf = _attention(to3(q), to3(k), to3(v_ext), ffn2_w_in[l])
        decay = chunk_decay.reshape(-1, REC_STAT_ROWS, V7X_LANES)[:, :MIX_IN_TOKENS // REC_CHUNK, :2]
        fast_flags = (decay <= REC_FAST_RANGE * LOG2_E).astype(jnp.int32).reshape(-1)
        o_f, o_b = _hgrn2(to3(rq), to3(g_f), to3(g_b), to3(ri), fast_flags)
        mixer_args = (to2(attn), to2(o_f), to2(o_b), rg, attn_out_norm[l], rec_out_norm[l], w_out_mix_bf)
        h, = _ffn(h, ffn2_norm[l], ffn2_w_in_bf, ffn2_w_out_bf,
                  tokens=FFN_MIX_TOKENS, sub_rows=FFN_MIX_SUB_ROWS, mixer_args=mixer_args,
                  final_gain=final_norm[l])
    return h.reshape(bsz, seq_len, d)
```
